```python
import jax, jax.numpy as jnp
from jax import lax
import numpy as np

D_MODEL = 1024
BATCH = 8
SEQ = 2048
DEPTH = 4
DEC_BATCH = 128
DEC_SEQ = 4
PAST_LEN = 16384
PAGE_SIZE = 128

N_MIXERS = 2
N_ML_LAYERS = (DEPTH + 1) // 2
N_CM_LAYERS = DEPTH // 2
ML_PROJ_FACTOR = 2
ML_INNER = ML_PROJ_FACTOR * D_MODEL
ML_HEADS = 4
ML_HEAD_DIM = ML_INNER // ML_HEADS
ML_QKV_BLOCK = 4
ML_N_BLOCKS = ML_INNER // ML_QKV_BLOCK
ML_CONV_W = 4
ML_CHUNK = 128
CM_CHUNK = 128
CM_GROUPS = 4
CM_WIDTH = D_MODEL
CM_GROUP_DIM = CM_WIDTH // CM_GROUPS
D_FF = 2816
EPS = 1e-6

kernel_name = 'hybrid_mlstm_chunkmlp_macaron_step'


def _rmsnorm(x, g):
    xf = x.astype(jnp.float32)
    y = xf * lax.rsqrt(jnp.mean(xf * xf, axis=-1, keepdims=True) + EPS)
    return (y * g.astype(jnp.float32)).astype(x.dtype)


def _swiglu(x, w_gate, w_up, w_down):
    return (jax.nn.silu(x @ w_gate) * (x @ w_up)) @ w_down


def _blockdiag(x, w):
    B, S, _ = x.shape
    xb = x.reshape(B, S, ML_N_BLOCKS, ML_QKV_BLOCK)
    return jnp.einsum('bsnc,ncd->bsnd', xb, w).reshape(B, S, ML_INNER)


def _mlstm_chunk(C0, n0, m0, q, k, v, ig, lf):
    L = q.shape[1]
    bt = jnp.cumsum(lf, axis=1).transpose(0, 2, 1)
    it = ig.transpose(0, 2, 1)
    causal = jnp.tril(jnp.ones((L, L), dtype=bool))
    dmat = jnp.where(causal, bt[..., :, None] - bt[..., None, :] + it[..., None, :], -jnp.inf)
    m_inter = m0[..., None] + bt
    m = jnp.maximum(jnp.max(dmat, axis=-1), m_inter)
    w_intra = jnp.exp(dmat - m[..., None])
    w_inter = jnp.exp(m_inter - m)
    sc = jnp.einsum('bthd,bshd->bhts', q, k) * w_intra
    num = (jnp.einsum('bhts,bshe->bthe', sc, v)
           + jnp.einsum('bthd,bhde->bthe', q, C0) * w_inter.transpose(0, 2, 1)[..., None])
    nq = jnp.sum(sc, axis=-1) + w_inter * jnp.einsum('bthd,bhd->bht', q, n0)
    den = jnp.maximum(jnp.abs(nq), jnp.exp(-m)).transpose(0, 2, 1)[..., None]
    h = num / den
    bl = bt[..., -1]
    ml = m[..., -1]
    g_inter = jnp.exp(m0 + bl - ml)
    g_s = jnp.exp(bl[..., None] - bt + it - ml[..., None])
    C = g_inter[..., None, None] * C0 + jnp.einsum('bhs,bshd,bshe->bhde', g_s, k, v)
    n = g_inter[..., None] * n0 + jnp.einsum('bhs,bshd->bhd', g_s, k)
    return h, C, n, ml


def _mlstm_core(q, k, v, ig, lf, C0, n0, m0, chunk):
    B, S, H, DH = q.shape
    nc = S // chunk

    def to_chunks(a):
        return a.reshape((B, nc, chunk) + a.shape[2:]).swapaxes(0, 1)

    def step(carry, blk):
        C, n, m = carry
        h, C, n, m = _mlstm_chunk(C, n, m, *blk)
        return (C, n, m), h

    (C, n, m), hs = lax.scan(step, (C0, n0, m0),
                             (to_chunks(q), to_chunks(k), to_chunks(v), to_chunks(ig), to_chunks(lf)))
    return hs.swapaxes(0, 1).reshape(B, S, H, DH), C, n, m


def _mlstm_mixer(xn, conv_buf, C0, n0, m0, p, j, chunk):
    B, S, _ = xn.shape
    f32 = jnp.float32
    up = xn @ p['ml_w_up'][j]
    xm, z = up[..., :ML_INNER], up[..., ML_INNER:]
    xfull = jnp.concatenate([conv_buf.astype(xm.dtype), xm], axis=1)
    wc = p['ml_w_conv'][j]
    xc = p['ml_b_conv'][j]
    for w in range(ML_CONV_W):
        xc = xc + xfull[:, w:w + S] * wc[w]
    xc = jax.nn.silu(xc)
    new_buf = xfull[:, xfull.shape[1] - (ML_CONV_W - 1):]
    q = _blockdiag(xc, p['ml_w_q'][j])
    k = _blockdiag(xc, p['ml_w_k'][j])
    v = _blockdiag(xm, p['ml_w_v'][j])
    gate_in = jnp.concatenate([q, k, v], axis=-1)
    ig = (gate_in @ p['ml_w_ig'][j] + p['ml_b_ig'][j]).astype(f32)
    lf = jax.nn.log_sigmoid((gate_in @ p['ml_w_fg'][j] + p['ml_b_fg'][j]).astype(f32))

    def heads(a):
        return a.reshape(B, S, ML_HEADS, ML_HEAD_DIM).astype(f32)

    h, C, n, m = _mlstm_core(heads(q), heads(k) * (ML_HEAD_DIM ** -0.5), heads(v), ig, lf,
                             C0.astype(f32), n0.astype(f32), m0.astype(f32), chunk)
    mu = jnp.mean(h, axis=-1, keepdims=True)
    var = jnp.mean(jnp.square(h - mu), axis=-1, keepdims=True)
    hn = ((h - mu) * lax.rsqrt(var + EPS)).reshape(B, S, ML_INNER).astype(xn.dtype) * p['ml_hn_g'][j]
    out = (hn + p['ml_skip'][j] * xc) * jax.nn.silu(z)
    return out @ p['ml_w_down'][j], new_buf, C, n, m


def _chunk_mlp(xn, p, j):
    B, S, _ = xn.shape
    hp = jax.nn.gelu(xn @ p['cm_w_in'][j] + p['cm_b_in'][j], approximate=False)
    u, v = hp[..., :CM_WIDTH], hp[..., CM_WIDTH:]
    vf = v.astype(jnp.float32)
    mu = jnp.mean(vf, axis=-1, keepdims=True)
    var = jnp.mean(jnp.square(vf - mu), axis=-1, keepdims=True)
    vn = ((vf - mu) * lax.rsqrt(var + EPS)).astype(v.dtype) * p['cm_ln_g'][j]
    L = min(S, CM_CHUNK)
    nc = S // L
    causal = jnp.tril(jnp.ones((L, L), dtype=bool))
    ws = jnp.where(causal, p['cm_w_s'][j][:, :L, :L], 0)
    bs = p['cm_b_s'][j][:, :L]
    vc = vn.reshape(B, nc, L, CM_GROUPS, CM_GROUP_DIM)
    mix = jnp.einsum('gts,bcsgd->bctgd', ws, vc) + bs.T[None, None, :, :, None]
    y = u * mix.reshape(B, S, CM_WIDTH)
    return y @ p['cm_w_out'][j], vn


def _trunk(x, conv_bufs, Cs, ns, ms, p, ml_chunk):
    new_C, new_n, new_m, new_conv, new_v = [], [], [], [], []
    for i in range(DEPTH):
        x = x + 0.5 * _swiglu(_rmsnorm(x, p['norm_ff1'][i]),
                              p['ffn1_w_gate'][i], p['ffn1_w_up'][i], p['ffn1_w_down'][i])
        xn = _rmsnorm(x, p['norm_mix'][i])
        j = i // N_MIXERS
        if i % N_MIXERS == 0:
            out, buf, C, n, m = _mlstm_mixer(xn, conv_bufs[j], Cs[j], ns[j], ms[j], p, j, ml_chunk)
            new_C.append(C)
            new_n.append(n)
            new_m.append(m)
            new_conv.append(buf)
        else:
            out, vrows = _chunk_mlp(xn, p, j)
            new_v.append(vrows)
        x = x + out
        x = x + 0.5 * _swiglu(_rmsnorm(x, p['norm_ff2'][i]),
                              p['ffn2_w_gate'][i], p['ffn2_w_up'][i], p['ffn2_w_down'][i])
    y = _rmsnorm(x, p['norm_final'])
    return y, jnp.stack(new_C), jnp.stack(new_n), jnp.stack(new_m), jnp.stack(new_conv), new_v


def setup_inputs(seed: int = 0) -> dict:
    key = jax.random.key(seed)
    ks = jax.random.split(key, 40)
    f32 = jnp.float32

    def nrm(k, shape, scale):
        return jax.random.normal(k, shape, f32) * scale

    d = {}
    d['x_prompt'] = nrm(ks[0], (BATCH, SEQ, D_MODEL), 1.0)
    d['x_sample'] = nrm(ks[1], (DEC_BATCH, DEC_SEQ, D_MODEL), 1.0)
    d['state_C'] = nrm(ks[2], (N_ML_LAYERS, DEC_BATCH, ML_HEADS, ML_HEAD_DIM, ML_HEAD_DIM), 0.05)
    d['state_n'] = nrm(ks[3], (N_ML_LAYERS, DEC_BATCH, ML_HEADS, ML_HEAD_DIM), 0.1)
    d['state_m'] = nrm(ks[4], (N_ML_LAYERS, DEC_BATCH, ML_HEADS), 1.0)
    d['state_conv'] = nrm(ks[5], (N_ML_LAYERS, DEC_BATCH, ML_CONV_W - 1, ML_INNER), 1.0)
    d['norm_ff1'] = 1.0 + nrm(ks[6], (DEPTH, D_MODEL), 0.02)
    d['norm_mix'] = 1.0 + nrm(ks[7], (DEPTH, D_MODEL), 0.02)
    d['norm_ff2'] = 1.0 + nrm(ks[8], (DEPTH, D_MODEL), 0.02)
    d['norm_final'] = 1.0 + nrm(ks[9], (D_MODEL,), 0.02)
    d['ffn1_w_gate'] = nrm(ks[10], (DEPTH, D_MODEL, D_FF), D_MODEL ** -0.5)
    d['ffn1_w_up'] = nrm(ks[11], (DEPTH, D_MODEL, D_FF), D_MODEL ** -0.5)
    d['ffn1_w_down'] = nrm(ks[12], (DEPTH, D_FF, D_MODEL), D_FF ** -0.5)
    d['ffn2_w_gate'] = nrm(ks[13], (DEPTH, D_MODEL, D_FF), D_MODEL ** -0.5)
    d['ffn2_w_up'] = nrm(ks[14], (DEPTH, D_MODEL, D_FF), D_MODEL ** -0.5)
    d['ffn2_w_down'] = nrm(ks[15], (DEPTH, D_FF, D_MODEL), D_FF ** -0.5)
    d['ml_w_up'] = nrm(ks[16], (N_ML_LAYERS, D_MODEL, 2 * ML_INNER), D_MODEL ** -0.5)
    d['ml_w_conv'] = nrm(ks[17], (N_ML_LAYERS, ML_CONV_W, ML_INNER), ML_CONV_W ** -0.5)
    d['ml_b_conv'] = nrm(ks[18], (N_ML_LAYERS, ML_INNER), 0.02)
    d['ml_w_q'] = nrm(ks[19], (N_ML_LAYERS, ML_N_BLOCKS, ML_QKV_BLOCK, ML_QKV_BLOCK), ML_QKV_BLOCK ** -0.5)
    d['ml_w_k'] = nrm(ks[20], (N_ML_LAYERS, ML_N_BLOCKS, ML_QKV_BLOCK, ML_QKV_BLOCK), ML_QKV_BLOCK ** -0.5)
    d['ml_w_v'] = nrm(ks[21], (N_ML_LAYERS, ML_N_BLOCKS, ML_QKV_BLOCK, ML_QKV_BLOCK), ML_QKV_BLOCK ** -0.5)
    d['ml_w_ig'] = nrm(ks[22], (N_ML_LAYERS, 3 * ML_INNER, ML_HEADS), (3 * ML_INNER) ** -0.5)
    d['ml_b_ig'] = nrm(ks[23], (N_ML_LAYERS, ML_HEADS), 0.1)
    d['ml_w_fg'] = nrm(ks[24], (N_ML_LAYERS, 3 * ML_INNER, ML_HEADS), (3 * ML_INNER) ** -0.5)
    d['ml_b_fg'] = jnp.linspace(3.0, 6.0, ML_HEADS, dtype=f32)[None, :] + nrm(ks[25], (N_ML_LAYERS, ML_HEADS), 0.1)
    d['ml_hn_g'] = 1.0 + nrm(ks[26], (N_ML_LAYERS, ML_INNER), 0.02)
    d['ml_skip'] = 1.0 + nrm(ks[27], (N_ML_LAYERS, ML_INNER), 0.02)
    d['ml_w_down'] = nrm(ks[28], (N_ML_LAYERS, ML_INNER, D_MODEL), ML_INNER ** -0.5)
    d['cm_w_in'] = nrm(ks[29], (N_CM_LAYERS, D_MODEL, 2 * CM_WIDTH), D_MODEL ** -0.5)
    d['cm_b_in'] = nrm(ks[30], (N_CM_LAYERS, 2 * CM_WIDTH), 0.02)
    d['cm_ln_g'] = 1.0 + nrm(ks[31], (N_CM_LAYERS, CM_WIDTH), 0.02)
    d['cm_w_s'] = nrm(ks[32], (N_CM_LAYERS, CM_GROUPS, CM_CHUNK, CM_CHUNK), 0.5 * CM_CHUNK ** -0.5)
    d['cm_b_s'] = 1.0 + nrm(ks[33], (N_CM_LAYERS, CM_GROUPS, CM_CHUNK), 0.02)
    d['cm_w_out'] = nrm(ks[34], (N_CM_LAYERS, CM_WIDTH, D_MODEL), CM_WIDTH ** -0.5)
    return d


def reference(x_prompt, x_sample, state_C, state_n, state_m, state_conv,
              norm_ff1, norm_mix, norm_ff2, norm_final,
              ffn1_w_gate, ffn1_w_up, ffn1_w_down, ffn2_w_gate, ffn2_w_up, ffn2_w_down,
              ml_w_up, ml_w_conv, ml_b_conv, ml_w_q, ml_w_k, ml_w_v,
              ml_w_ig, ml_b_ig, ml_w_fg, ml_b_fg, ml_hn_g, ml_skip, ml_w_down,
              cm_w_in, cm_b_in, cm_ln_g, cm_w_s, cm_b_s, cm_w_out):
    p = {'norm_ff1': norm_ff1, 'norm_mix': norm_mix, 'norm_ff2': norm_ff2, 'norm_final': norm_final,
         'ffn1_w_gate': ffn1_w_gate, 'ffn1_w_up': ffn1_w_up, 'ffn1_w_down': ffn1_w_down,
         'ffn2_w_gate': ffn2_w_gate, 'ffn2_w_up': ffn2_w_up, 'ffn2_w_down': ffn2_w_down,
         'ml_w_up': ml_w_up, 'ml_w_conv': ml_w_conv, 'ml_b_conv': ml_b_conv,
         'ml_w_q': ml_w_q, 'ml_w_k': ml_w_k, 'ml_w_v': ml_w_v,
         'ml_w_ig': ml_w_ig, 'ml_b_ig': ml_b_ig, 'ml_w_fg': ml_w_fg, 'ml_b_fg': ml_b_fg,
         'ml_hn_g': ml_hn_g, 'ml_skip': ml_skip, 'ml_w_down': ml_w_down,
         'cm_w_in': cm_w_in, 'cm_b_in': cm_b_in, 'cm_ln_g': cm_ln_g,
         'cm_w_s': cm_w_s, 'cm_b_s': cm_b_s, 'cm_w_out': cm_w_out}
    f32 = jnp.float32
    bp = x_prompt.shape[0]
    zero_conv = jnp.zeros((N_ML_LAYERS, bp, ML_CONV_W - 1, ML_INNER), x_prompt.dtype)
    zero_C = jnp.zeros((N_ML_LAYERS, bp, ML_HEADS, ML_HEAD_DIM, ML_HEAD_DIM), f32)
    zero_n = jnp.zeros((N_ML_LAYERS, bp, ML_HEADS, ML_HEAD_DIM), f32)
    zero_m = jnp.zeros((N_ML_LAYERS, bp, ML_HEADS), f32)
    y_prompt, C_prompt, n_prompt, m_prompt, conv_prompt, _ = _trunk(
        x_prompt, zero_conv, zero_C, zero_n, zero_m, p, min(ML_CHUNK, x_prompt.shape[1]))
    y_sample, C_sample, n_sample, m_sample, conv_sample, v_rows = _trunk(
        x_sample, state_conv, state_C, state_n, state_m, p, x_sample.shape[1])
    v_sample = jnp.stack(v_rows)
    return (y_prompt, y_sample, C_prompt, n_prompt, m_prompt, conv_prompt,
            C_sample, n_sample, m_sample, conv_sample, v_sample)
```

```python
import functools

import jax
import jax.numpy as jnp
from jax import lax
from jax.experimental import pallas as pl
from jax.experimental.pallas import tpu as pltpu

F32 = jnp.float32
BF16 = jnp.bfloat16

EPS = 1e-6
N_HEADS = 4
HEAD_DIM = 512
QKV_BLOCK = 4
CONV_W = 4
CM_GROUPS = 4
NEG = -1e30

LANE = 128
SUBLANE = 8
MXU_DIM = 256
VMEM_LIMIT = 56 * 1024 * 1024


def _cparams(n_axes):
    return pltpu.CompilerParams(dimension_semantics=("arbitrary",) * n_axes,
                                vmem_limit_bytes=VMEM_LIMIT)


def _const_spec(shape):
    nd = len(shape)
    return pl.BlockSpec(shape, lambda *_: (0,) * nd, pipeline_mode=pl.Buffered(1))


def _layer_spec(shape, layer):
    nd = len(shape)
    return pl.BlockSpec((None,) + tuple(shape), lambda *_: (layer,) + (0,) * nd,
                        pipeline_mode=pl.Buffered(1))


def _rmsnorm(x, g):
    return x * lax.rsqrt(jnp.mean(x * x, axis=-1, keepdims=True) + EPS) * g


def _silu(x):
    return x * jax.nn.sigmoid(x)


def _dot(a, b):
    return jnp.dot(a, b, preferred_element_type=F32)


def _ffn_body(x_ref, g_ref, wg_ref, wu_ref, wd_ref, gf_ref, o_ref, *, ff_chunks, final_norm):
    x = x_ref[...]
    xn = _rmsnorm(x, g_ref[...]).astype(BF16)
    acc = None
    for lo, hi in ff_chunks:
        gate = _dot(xn, wg_ref[:, lo:hi])
        up = _dot(xn, wu_ref[:, lo:hi])
        h = (_silu(gate) * up).astype(BF16)
        part = _dot(h, wd_ref[lo:hi, :])
        acc = part if acc is None else acc + part
    y = x + 0.5 * acc
    if final_norm:
        y = _rmsnorm(y, gf_ref[...])
    o_ref[...] = y


def _ffn(x, norm_g, w_gate, w_up, w_down, layer, tm, final_g=None):
    T, D = x.shape
    d_ff = w_gate.shape[-1]
    step = 4 * MXU_DIM
    ff_chunks = tuple((lo, min(lo + step, d_ff)) for lo in range(0, d_ff, step))
    final_norm = final_g is not None
    gf = final_g if final_norm else norm_g[layer]
    body = functools.partial(_ffn_body, ff_chunks=ff_chunks, final_norm=final_norm)
    return pl.pallas_call(
        body,
        out_shape=jax.ShapeDtypeStruct((T, D), F32),
        grid=(T // tm,),
        in_specs=[
            pl.BlockSpec((tm, D), lambda i: (i, 0)),
            _layer_spec((1, D), layer),
            _layer_spec((D, d_ff), layer),
            _layer_spec((D, d_ff), layer),
            _layer_spec((d_ff, D), layer),
            _const_spec((1, D)),
        ],
        out_specs=pl.BlockSpec((tm, D), lambda i: (i, 0)),
        compiler_params=_cparams(1),
        name="ffn",
    )(x, norm_g.reshape(norm_g.shape[0], 1, D), w_gate, w_up, w_down, gf.reshape(1, D))


def _ml_pre_body(x_ref, g_ref, wup_ref, wc_ref, bc_ref, bq_ref, bk_ref, bv_ref, wgt_ref, bgt_ref,
                 hist_ref, q_ref, k_ref, v_ref, xc_ref, z_ref, gates_ref, tail_ref, xbuf,
                 *, tm, inner, header, stride, tiles_per_seq):
    i = pl.program_id(0)

    @pl.when(i % tiles_per_seq == 0)
    def _():
        xbuf[0:header, :] = hist_ref[...]

    xn = _rmsnorm(x_ref[...], g_ref[...]).astype(BF16)
    for g in range(inner // MXU_DIM):
        cols = slice(g * MXU_DIM, (g + 1) * MXU_DIM)
        xm = _dot(xn, wup_ref[:, cols])
        xbuf[header:header + tm, cols] = xm
        acc = bc_ref[:, cols] + wc_ref[CONV_W - 1:CONV_W, cols] * xm
        for d in range(1, CONV_W):
            lo = header - d * stride
            acc = acc + wc_ref[CONV_W - 1 - d:CONV_W - d, cols] * xbuf[lo:lo + tm, cols]
        xc = _silu(acc)
        xc_ref[:, cols] = xc
        xcb = xc.astype(BF16)
        q_ref[:, cols] = _dot(xcb, bq_ref[g]).astype(BF16)
        k_ref[:, cols] = _dot(xcb, bk_ref[g]).astype(BF16)
        v_ref[:, cols] = _dot(xm.astype(BF16), bv_ref[g]).astype(BF16)
    z_ref[...] = _dot(xn, wup_ref[:, inner:])
    gates_ref[...] = (_dot(q_ref[...], wgt_ref[0:inner, :])
                      + _dot(k_ref[...], wgt_ref[inner:2 * inner, :])
                      + _dot(v_ref[...], wgt_ref[2 * inner:, :]) + bgt_ref[...])
    last = xbuf[tm:tm + header, :]
    tail_ref[...] = last
    xbuf[0:header, :] = last


def _ml_pre(x, norm_g, layer, w_up, w_conv, b_conv, bq, bk, bv, w_gates, b_gates, hist, j,
            tm, stride, tiles_per_seq):
    T, D = x.shape
    inner = w_conv.shape[-1]
    header = hist.shape[1]
    n_tiles = T // tm
    body = functools.partial(_ml_pre_body, tm=tm, inner=inner, header=header, stride=stride,
                             tiles_per_seq=tiles_per_seq)
    tok = lambda w: pl.BlockSpec((tm, w), lambda i: (i, 0))
    return pl.pallas_call(
        body,
        out_shape=(
            jax.ShapeDtypeStruct((T, inner), BF16),
            jax.ShapeDtypeStruct((T, inner), BF16),
            jax.ShapeDtypeStruct((T, inner), BF16),
            jax.ShapeDtypeStruct((T, inner), F32),
            jax.ShapeDtypeStruct((T, inner), F32),
            jax.ShapeDtypeStruct((T, LANE), F32),
            jax.ShapeDtypeStruct((n_tiles, header, inner), F32),
        ),
        grid=(n_tiles,),
        in_specs=[
            tok(D),
            _layer_spec((1, D), layer),
            _layer_spec((D, 2 * inner), j),
            _layer_spec((CONV_W, inner), j),
            _layer_spec((1, inner), j),
            _const_spec(bq.shape), _const_spec(bk.shape), _const_spec(bv.shape),
            _const_spec(w_gates.shape), _const_spec(b_gates.shape),
            pl.BlockSpec((None, header, inner), lambda i: (i // tiles_per_seq, 0, 0)),
        ],
        out_specs=(tok(inner), tok(inner), tok(inner), tok(inner), tok(inner), tok(LANE),
                   pl.BlockSpec((None, header, inner), lambda i: (i, 0, 0))),
        scratch_shapes=[pltpu.VMEM((header + tm, inner), F32)],
        compiler_params=_cparams(1),
        name="ml_pre",
    )(x, norm_g.reshape(norm_g.shape[0], 1, D), w_up, w_conv,
      b_conv.reshape(b_conv.shape[0], 1, inner), bq, bk, bv, w_gates, b_gates, hist)


def _log_sigmoid(x):
    return jnp.minimum(x, 0.0) - jnp.log(1.0 + jnp.exp(-jnp.abs(x)))


def _ml_core_body(*refs, L, has_state):
    if has_state:
        (q_ref, k_ref, v_ref, gates_ref, c0_ref, n0_ref, m0_ref, _,
         h_ref, c_ref, n_ref, m_ref) = refs
    else:
        q_ref, k_ref, v_ref, gates_ref, _, h_ref, c_ref, n_ref, m_ref = refs
    c_idx = pl.program_id(1)

    @pl.when(c_idx == 0)
    def _():
        if has_state:
            c_ref[...] = c0_ref[...]
            n_ref[...] = n0_ref[...]
            m_ref[...] = m0_ref[...]
        else:
            c_ref[...] = jnp.zeros_like(c_ref)
            n_ref[...] = jnp.zeros_like(n_ref)
            m_ref[...] = jnp.zeros_like(m_ref)

    scale = HEAD_DIM ** -0.5
    row = lax.broadcasted_iota(jnp.int32, (L, L), 0)
    col = lax.broadcasted_iota(jnp.int32, (L, L), 1)
    diag = row == col
    causal = col <= row
    gates = gates_ref[...]
    for h in range(N_HEADS):
        cols = slice(h * HEAD_DIM, (h + 1) * HEAD_DIM)
        qh, kh, vh = q_ref[:, cols], k_ref[:, cols], v_ref[:, cols]
        ig_col = gates[:, h:h + 1]
        lf_col = _log_sigmoid(gates[:, N_HEADS + h:N_HEADS + h + 1])
        lf_row = jnp.sum(jnp.where(diag, lf_col, 0.0), axis=0, keepdims=True)
        ig_row = jnp.sum(jnp.where(diag, ig_col, 0.0), axis=0, keepdims=True)
        bt_row = jnp.sum(jnp.where(row <= col, lf_col, 0.0), axis=0, keepdims=True)
        bt_col = jnp.sum(jnp.where(causal, lf_row, 0.0), axis=1, keepdims=True)
        a_row = ig_row - bt_row
        m0 = m_ref[h:h + 1, 0:1]
        amax = jnp.max(jnp.where(causal, a_row, NEG), axis=1, keepdims=True)
        m_col = bt_col + jnp.maximum(amax, m0)
        w_intra = jnp.exp(jnp.where(causal, bt_col - m_col + a_row, NEG))
        w_inter = jnp.exp(m0 + bt_col - m_col)
        s = lax.dot_general(qh, kh, (((1,), (1,)), ((), ())), preferred_element_type=F32)
        sc = s * scale * w_intra
        c0 = c_ref[h]
        n0 = n_ref[h:h + 1, :]
        num = _dot(sc.astype(BF16), vh) + _dot(qh, c0.astype(BF16)) * w_inter
        nq = (jnp.sum(sc, axis=1, keepdims=True)
              + w_inter * jnp.sum(qh.astype(F32) * n0, axis=1, keepdims=True))
        den = jnp.maximum(jnp.abs(nq), jnp.exp(-m_col))
        hh = num / den
        mu = jnp.mean(hh, axis=-1, keepdims=True)
        hc = hh - mu
        var = jnp.mean(hc * hc, axis=-1, keepdims=True)
        h_ref[:, cols] = hc * lax.rsqrt(var + EPS)
        bl = bt_col[L - 1:L, :]
        ml = m_col[L - 1:L, :]
        g_inter = jnp.exp(m0 + bl - ml)
        g_s = jnp.exp(bl - bt_col + ig_col - ml) * scale
        kg = kh.astype(F32) * g_s
        c_ref[h] = g_inter * c0 + lax.dot_general(
            kg.astype(BF16), vh, (((0,), (0,)), ((), ())), preferred_element_type=F32)
        n_ref[h:h + 1, :] = g_inter * n0 + jnp.sum(kg, axis=0, keepdims=True)
        m_ref[h:h + 1, :] = jnp.broadcast_to(ml, (1, LANE))


def _ml_core(q, k, v, gates, B, nc, L, j, c_all, state=None):
    T, inner = q.shape
    has_state = state is not None
    n_ml = 2
    body = functools.partial(_ml_core_body, L=L, has_state=has_state)
    tok = lambda w: pl.BlockSpec((L, w), lambda b, c: (b * nc + c, 0))
    c_block = pl.BlockSpec((None, None, N_HEADS, HEAD_DIM, HEAD_DIM), lambda b, c: (j, b, 0, 0, 0))
    in_specs = [tok(inner), tok(inner), tok(inner), tok(LANE)]
    args = [q, k, v, gates]
    if has_state:
        in_specs += [c_block,
                     pl.BlockSpec((None, None, N_HEADS, HEAD_DIM), lambda b, c: (j, b, 0, 0)),
                     pl.BlockSpec((None, SUBLANE, LANE), lambda b, c: (b, 0, 0))]
        args += list(state)
    in_specs.append(pl.BlockSpec(memory_space=pl.ANY))
    if c_all is not None:
        args.append(c_all)
        aliases = {len(args) - 1: 1}
    else:
        args.append(jnp.zeros((SUBLANE, LANE), F32))
        aliases = {}
    return pl.pallas_call(
        body,
        out_shape=(
            jax.ShapeDtypeStruct((T, inner), F32),
            jax.ShapeDtypeStruct((n_ml, B, N_HEADS, HEAD_DIM, HEAD_DIM), F32),
            jax.ShapeDtypeStruct((B, N_HEADS, HEAD_DIM), F32),
            jax.ShapeDtypeStruct((B, SUBLANE, LANE), F32),
        ),
        grid=(B, nc),
        in_specs=in_specs,
        out_specs=(tok(inner), c_block,
                   pl.BlockSpec((None, N_HEADS, HEAD_DIM), lambda b, c: (b, 0, 0)),
                   pl.BlockSpec((None, SUBLANE, LANE), lambda b, c: (b, 0, 0))),
        input_output_aliases=aliases,
        compiler_params=_cparams(2),
        name="ml_core",
    )(*args)


def _ml_post_body(x_ref, hn_ref, xc_ref, z_ref, hg_ref, sk_ref, wd_ref, o_ref):
    out = (hn_ref[...] * hg_ref[...] + sk_ref[...] * xc_ref[...]) * _silu(z_ref[...])
    o_ref[...] = x_ref[...] + _dot(out.astype(BF16), wd_ref[...])


def _ml_post(x, hn, xc, z, hn_g, skip, w_down, j, tm):
    T, D = x.shape
    inner = hn.shape[-1]
    tok = lambda w: pl.BlockSpec((tm, w), lambda i: (i, 0))
    return pl.pallas_call(
        _ml_post_body,
        out_shape=jax.ShapeDtypeStruct((T, D), F32),
        grid=(T // tm,),
        in_specs=[tok(D), tok(inner), tok(inner), tok(inner),
                  _layer_spec((1, inner), j), _layer_spec((1, inner), j),
                  _layer_spec((inner, D), j)],
        out_specs=tok(D),
        compiler_params=_cparams(1),
        name="ml_post",
    )(x, hn, xc, z, hn_g.reshape(hn_g.shape[0], 1, inner), skip.reshape(skip.shape[0], 1, inner),
      w_down)


def _cm_body(x_ref, g_ref, win_ref, bin_ref, lg_ref, wmix_ref, bmix_ref, wout_ref,
             o_ref, vn_ref, *, width):
    x = x_ref[...]
    xn = _rmsnorm(x, g_ref[...]).astype(BF16)
    hp = _dot(xn, win_ref[...]) + bin_ref[...]
    hp = 0.5 * hp * (1.0 + lax.erf(hp * (2.0 ** -0.5)))
    u, v = hp[:, :width], hp[:, width:]
    mu = jnp.mean(v, axis=-1, keepdims=True)
    vc = v - mu
    var = jnp.mean(vc * vc, axis=-1, keepdims=True)
    vn = vc * lax.rsqrt(var + EPS) * lg_ref[...]
    vn_ref[...] = vn
    vnb = vn.astype(BF16)
    gd = width // CM_GROUPS
    bmix = bmix_ref[...]
    parts = []
    for g in range(CM_GROUPS):
        cols = slice(g * gd, (g + 1) * gd)
        mix = _dot(wmix_ref[g], vnb[:, cols]) + bmix[:, g:g + 1]
        parts.append((u[:, cols] * mix).astype(BF16))
    y = jnp.concatenate(parts, axis=-1)
    o_ref[...] = x + _dot(y, wout_ref[...])


def _chunk_mlp(x, norm_g, layer, w_in, b_in, ln_g, wmix, bmix, w_out, j, tm):
    T, D = x.shape
    width = ln_g.shape[-1]
    body = functools.partial(_cm_body, width=width)
    tok = lambda w: pl.BlockSpec((tm, w), lambda i: (i, 0))
    return pl.pallas_call(
        body,
        out_shape=(jax.ShapeDtypeStruct((T, D), F32), jax.ShapeDtypeStruct((T, width), F32)),
        grid=(T // tm,),
        in_specs=[tok(D), _layer_spec((1, D), layer),
                  _layer_spec((D, 2 * width), j), _layer_spec((1, 2 * width), j),
                  _layer_spec((1, width), j),
                  _const_spec(wmix.shape), _const_spec(bmix.shape),
                  _layer_spec((width, D), j)],
        out_specs=(tok(D), tok(width)),
        compiler_params=_cparams(1),
        name="chunk_mlp",
    )(x, norm_g.reshape(norm_g.shape[0], 1, D), w_in, b_in.reshape(b_in.shape[0], 1, 2 * width),
      ln_g.reshape(ln_g.shape[0], 1, width), wmix, bmix, w_out)


def _blockdiag_dense(w):
    per = MXU_DIM // QKV_BLOCK
    w4 = w.reshape(-1, per, QKV_BLOCK, QKV_BLOCK)
    eye = jnp.eye(per, dtype=w.dtype)
    dense = w4[:, :, :, None, :] * eye[None, :, None, :, None]
    return dense.reshape(-1, MXU_DIM, MXU_DIM).astype(BF16)


def _gate_weights(w_ig, b_ig, w_fg, b_fg):
    pad = LANE - 2 * N_HEADS
    w = jnp.concatenate([w_ig, w_fg, jnp.zeros((w_ig.shape[0], pad), w_ig.dtype)], axis=1)
    b = jnp.concatenate([b_ig, b_fg, jnp.zeros((pad,), b_ig.dtype)])
    return w.astype(BF16), b.reshape(1, LANE)


def _trunk(x, grp, p):
    depth = p['norm_ff1'].shape[0]
    tm = grp['tm']
    new_n, new_m, new_conv, new_v = [], [], [], []
    c_all = None
    for i in range(depth):
        j = i // 2
        x = _ffn(x, p['norm_ff1'], p['ffn1_w_gate'], p['ffn1_w_up'], p['ffn1_w_down'], i, tm)
        if i % 2 == 0:
            q, k, v, xc, z, gates, tail = _ml_pre(
                x, p['norm_mix'], i, p['ml_w_up'], p['ml_w_conv'], p['ml_b_conv'],
                p['bq'][j], p['bk'][j], p['bv'][j], p['w_gates'][j], p['b_gates'][j],
                grp['hist'][j], j, grp['pre_tm'], grp['stride'], grp['tiles_per_seq'])
            new_conv.append(grp['conv_out'](tail))
            q, k, v, gates = grp['to_core'](q, k, v, gates)
            state = grp['state'](j)
            hn, c_all, n, m = _ml_core(q, k, v, gates, grp['B'], grp['nc'], grp['L'], j, c_all, state)
            hn = grp['from_core'](hn)
            new_n.append(n)
            new_m.append(m[:, :N_HEADS, 0])
            x = _ml_post(x, hn, xc, z, p['ml_hn_g'], p['ml_skip'], p['ml_w_down'], j, grp['pre_tm'])
        else:
            x, vn = _chunk_mlp(x, p['norm_mix'], i, p['cm_w_in'], p['cm_b_in'], p['cm_ln_g'],
                               grp['wmix'][j], grp['bmix'][j], p['cm_w_out'], j, grp['cm_tm'])
            new_v.append(vn)
        final_g = p['norm_final'] if i == depth - 1 else None
        x = _ffn(x, p['norm_ff2'], p['ffn2_w_gate'], p['ffn2_w_up'], p['ffn2_w_down'], i, tm,
                 final_g=final_g)
    return x, c_all, jnp.stack(new_n), jnp.stack(new_m), jnp.stack(new_conv), new_v


def kernel(x_prompt, x_sample, state_C, state_n, state_m, state_conv, norm_ff1, norm_mix, norm_ff2, norm_final, ffn1_w_gate, ffn1_w_up, ffn1_w_down, ffn2_w_gate, ffn2_w_up, ffn2_w_down, ml_w_up, ml_w_conv, ml_b_conv, ml_w_q, ml_w_k, ml_w_v, ml_w_ig, ml_b_ig, ml_w_fg, ml_b_fg, ml_hn_g, ml_skip, ml_w_down, cm_w_in, cm_b_in, cm_ln_g, cm_w_s, cm_b_s, cm_w_out):
    bp, sp, d_model = x_prompt.shape
    bs, ss, _ = x_sample.shape
    n_ml, inner = ml_w_conv.shape[0], ml_w_conv.shape[-1]
    n_cm, width = cm_ln_g.shape
    hist_rows = CONV_W - 1
    sb = min(64, bs)

    gate_wb = [_gate_weights(ml_w_ig[j], ml_b_ig[j], ml_w_fg[j], ml_b_fg[j]) for j in range(n_ml)]
    p = {
        'norm_ff1': norm_ff1, 'norm_mix': norm_mix, 'norm_ff2': norm_ff2, 'norm_final': norm_final,
        'ffn1_w_gate': ffn1_w_gate.astype(BF16), 'ffn1_w_up': ffn1_w_up.astype(BF16),
        'ffn1_w_down': ffn1_w_down.astype(BF16),
        'ffn2_w_gate': ffn2_w_gate.astype(BF16), 'ffn2_w_up': ffn2_w_up.astype(BF16),
        'ffn2_w_down': ffn2_w_down.astype(BF16),
        'ml_w_up': ml_w_up.astype(BF16), 'ml_w_conv': ml_w_conv, 'ml_b_conv': ml_b_conv,
        'bq': [_blockdiag_dense(ml_w_q[j]) for j in range(n_ml)],
        'bk': [_blockdiag_dense(ml_w_k[j]) for j in range(n_ml)],
        'bv': [_blockdiag_dense(ml_w_v[j]) for j in range(n_ml)],
        'w_gates': [wb[0] for wb in gate_wb], 'b_gates': [wb[1] for wb in gate_wb],
        'ml_hn_g': ml_hn_g, 'ml_skip': ml_skip, 'ml_w_down': ml_w_down.astype(BF16),
        'cm_w_in': cm_w_in.astype(BF16), 'cm_b_in': cm_b_in, 'cm_ln_g': cm_ln_g,
        'cm_w_out': cm_w_out.astype(BF16),
    }

    def mix_weights(L, kron_left):
        causal = jnp.tril(jnp.ones((L, L), dtype=bool))
        ws = jnp.where(causal, cm_w_s[:, :, :L, :L], 0)
        bs_ = cm_b_s[:, :, :L]
        if kron_left:
            reps = kron_left
            eye = jnp.eye(reps, dtype=ws.dtype)
            wm = ws[:, :, None, :, None, :] * eye[None, None, :, None, :, None]
            wm = wm.reshape(n_cm, CM_GROUPS, reps * L, reps * L)
            bm = jnp.tile(bs_, (1, 1, reps))
        else:
            eye = jnp.eye(sb, dtype=ws.dtype)
            wm = ws[:, :, :, None, :, None] * eye[None, None, None, :, None, :]
            wm = wm.reshape(n_cm, CM_GROUPS, L * sb, L * sb)
            bm = jnp.repeat(bs_, sb, axis=2)
        bm = jnp.swapaxes(bm, 1, 2)
        bm = jnp.concatenate([bm, jnp.zeros(bm.shape[:2] + (LANE - CM_GROUPS,), bm.dtype)], axis=2)
        return wm.astype(BF16), bm

    tm_p = 512
    pre_tm_p = 256
    cm_tm_p = 256
    chunk_p = min(128, sp)
    cm_chunk_p = min(128, sp)
    wmix_p, bmix_p = mix_weights(cm_chunk_p, cm_tm_p // cm_chunk_p)
    grp_p = {
        'tm': tm_p, 'pre_tm': pre_tm_p, 'cm_tm': cm_tm_p, 'stride': 1,
        'tiles_per_seq': sp // pre_tm_p,
        'hist': [jnp.zeros((bp, SUBLANE, inner), F32)] * n_ml,
        'conv_out': lambda tail: tail.reshape(bp, sp // pre_tm_p, SUBLANE, inner)[:, -1, SUBLANE - hist_rows:, :],
        'to_core': lambda q, k, v, g: (q, k, v, g),
        'from_core': lambda hn: hn,
        'state': lambda j: None,
        'B': bp, 'nc': sp // chunk_p, 'L': chunk_p,
        'wmix': wmix_p, 'bmix': bmix_p,
    }
    yp, c_p, n_p, m_p, conv_p, _ = _trunk(x_prompt.reshape(bp * sp, d_model), grp_p, p)

    ts = bs * ss
    n_sg = bs // sb
    l_pad = SUBLANE

    def tile_major(a):
        steps, feat = a.shape[1], a.shape[2]
        return a.reshape(n_sg, sb, steps, feat).transpose(0, 2, 1, 3).reshape(n_sg * steps * sb, feat)

    def batch_major(a, steps):
        feat = a.shape[-1]
        return a.reshape(n_sg, steps, sb, feat).transpose(0, 2, 1, 3).reshape(bs, steps, feat)

    def to_core(q, k, v, g):
        def seq_major(a, fill):
            a = batch_major(a, ss)
            pad = jnp.broadcast_to(fill.astype(a.dtype), (bs, l_pad - ss, a.shape[-1]))
            return jnp.concatenate([a, pad], axis=1).reshape(bs * l_pad, a.shape[-1])
        zero = jnp.zeros((inner,), F32)
        gfill = jnp.concatenate([jnp.full((N_HEADS,), NEG, F32), jnp.full((N_HEADS,), -NEG, F32),
                                 jnp.zeros((LANE - 2 * N_HEADS,), F32)])
        return seq_major(q, zero), seq_major(k, zero), seq_major(v, zero), seq_major(g, gfill)

    def from_core(hn):
        return tile_major(hn.reshape(bs, l_pad, inner)[:, :ss])

    def sample_state(j):
        m0 = jnp.pad(state_m[j], ((0, 0), (0, SUBLANE - N_HEADS)))
        return state_C, state_n, jnp.broadcast_to(m0[:, :, None], (bs, SUBLANE, LANE))

    wmix_s, bmix_s = mix_weights(ss, 0)
    tile_s = ss * sb
    hist_s = [tile_major(state_conv[j]).reshape(n_sg, hist_rows * sb, inner) for j in range(n_ml)]
    grp_s = {
        'tm': tile_s, 'pre_tm': tile_s, 'cm_tm': tile_s, 'stride': sb, 'tiles_per_seq': 1,
        'hist': hist_s,
        'conv_out': lambda tail: batch_major(tail.reshape(n_sg * hist_rows * sb, inner), hist_rows),
        'to_core': to_core, 'from_core': from_core, 'state': sample_state,
        'B': bs, 'nc': 1, 'L': l_pad,
        'wmix': wmix_s, 'bmix': bmix_s,
    }
    ys, c_s, n_s, m_s, conv_s, v_rows = _trunk(tile_major(x_sample), grp_s, p)

    y_prompt = yp.reshape(bp, sp, d_model)
    y_sample = batch_major(ys, ss)
    v_sample = jnp.stack([batch_major(vr, ss) for vr in v_rows])
    return (y_prompt, y_sample, c_p, n_p, m_p, conv_p, c_s, n_s, m_s, conv_s, v_sample)
```

```python
import functools
import math

import jax
import jax.numpy as jnp
from jax import lax
from jax.experimental import pallas as pl
from jax.experimental.pallas import tpu as pltpu

F32 = jnp.float32
BF16 = jnp.bfloat16

EPS = 1e-6
N_HEADS = 4
HEAD_DIM = 512
QKV_BLOCK = 4
CONV_W = 4
CM_GROUPS = 4
NEG = -1e30

LANE = 128
SUBLANE = 8
MXU_DIM = 256
VMEM_LIMIT = 56 * 1024 * 1024


def _cparams(n_axes):
    return pltpu.CompilerParams(dimension_semantics=("arbitrary",) * n_axes,
                                vmem_limit_bytes=VMEM_LIMIT)


def _const_spec(shape):
    nd = len(shape)
    return pl.BlockSpec(shape, lambda *_: (0,) * nd, pipeline_mode=pl.Buffered(1))


def _layer_spec(shape, layer):
    nd = len(shape)
    return pl.BlockSpec((None,) + tuple(shape), lambda *_: (layer,) + (0,) * nd,
                        pipeline_mode=pl.Buffered(1))


def _rmsnorm(x, g):
    return x * lax.rsqrt(jnp.mean(x * x, axis=-1, keepdims=True) + EPS) * g


def _silu(x):
    return x * jax.nn.sigmoid(x)


def _dot(a, b):
    return jnp.dot(a, b, preferred_element_type=F32)


def _ffn_body(x_ref, g_ref, wg_ref, wu_ref, wd_ref, gf_ref, o_ref, *, ff_chunks, final_norm):
    x = x_ref[...]
    xn = _rmsnorm(x, g_ref[...]).astype(BF16)
    acc = None
    for lo, hi in ff_chunks:
        gate = _dot(xn, wg_ref[:, lo:hi])
        up = _dot(xn, wu_ref[:, lo:hi])
        h = (_silu(gate) * up).astype(BF16)
        part = _dot(h, wd_ref[lo:hi, :])
        acc = part if acc is None else acc + part
    y = x + 0.5 * acc
    if final_norm:
        y = _rmsnorm(y, gf_ref[...])
    o_ref[...] = y


def _ffn(x, norm_g, w_gate, w_up, w_down, layer, tm, final_g=None):
    T, D = x.shape
    d_ff = w_gate.shape[-1]
    step = 4 * MXU_DIM
    ff_chunks = tuple((lo, min(lo + step, d_ff)) for lo in range(0, d_ff, step))
    final_norm = final_g is not None
    gf = final_g if final_norm else norm_g[layer]
    body = functools.partial(_ffn_body, ff_chunks=ff_chunks, final_norm=final_norm)
    return pl.pallas_call(
        body,
        out_shape=jax.ShapeDtypeStruct((T, D), F32),
        grid=(T // tm,),
        in_specs=[
            pl.BlockSpec((tm, D), lambda i: (i, 0)),
            _layer_spec((1, D), layer),
            _layer_spec((D, d_ff), layer),
            _layer_spec((D, d_ff), layer),
            _layer_spec((d_ff, D), layer),
            _const_spec((1, D)),
        ],
        out_specs=pl.BlockSpec((tm, D), lambda i: (i, 0)),
        compiler_params=_cparams(1),
        name="ffn",
    )(x, norm_g.reshape(norm_g.shape[0], 1, D), w_gate, w_up, w_down, gf.reshape(1, D))


def _ml_pre_body(x_ref, g_ref, wup_ref, wc_ref, bc_ref, bq_ref, bk_ref, bv_ref, wgt_ref, bgt_ref,
                 hist_ref, q_ref, k_ref, v_ref, xc_ref, z_ref, gates_ref, tail_ref, xbuf,
                 *, tm, inner, header, stride, tiles_per_seq):
    i = pl.program_id(0)

    n_groups = inner // MXU_DIM

    @pl.when(i % tiles_per_seq == 0)
    def _():
        for g in range(n_groups):
            xbuf[g, 0:header, :] = hist_ref[:, g * MXU_DIM:(g + 1) * MXU_DIM]

    xn = _rmsnorm(x_ref[...], g_ref[...]).astype(BF16)

    def up_proj(g):
        cols = slice(g * MXU_DIM, (g + 1) * MXU_DIM)
        zcols = slice(inner + g * MXU_DIM, inner + (g + 1) * MXU_DIM)
        xm_g = _dot(xn, wup_ref[:, cols])
        z_ref[:, cols] = _dot(xn, wup_ref[:, zcols])
        return xm_g

    def gate_part(g, qb, kb, vb):
        rows = lambda base: slice(base + g * MXU_DIM, base + (g + 1) * MXU_DIM)
        return (_dot(qb, wgt_ref[rows(0), :]) + _dot(kb, wgt_ref[rows(inner), :])
                + _dot(vb, wgt_ref[rows(2 * inner), :]))

    gacc = bgt_ref[...]
    prev_qkv = None
    xm_next = up_proj(0)
    for g in range(n_groups):
        cols = slice(g * MXU_DIM, (g + 1) * MXU_DIM)
        xm = xm_next
        if g + 1 < n_groups:
            xm_next = up_proj(g + 1)
        xbuf[g, header:header + tm, :] = xm
        acc = bc_ref[:, cols] + wc_ref[CONV_W - 1:CONV_W, cols] * xm
        for d in range(1, CONV_W):
            lo = header - d * stride
            acc = acc + wc_ref[CONV_W - 1 - d:CONV_W - d, cols] * xbuf[g, lo:lo + tm, :]
        xc = _silu(acc)
        xc_ref[:, cols] = xc
        xcb = xc.astype(BF16)
        qb = _dot(xcb, bq_ref[g]).astype(BF16)
        kb = _dot(xcb, bk_ref[g]).astype(BF16)
        vb = _dot(xm.astype(BF16), bv_ref[g]).astype(BF16)
        q_ref[:, cols] = qb
        k_ref[:, cols] = kb
        v_ref[:, cols] = vb
        if g > 0:
            gacc = gacc + gate_part(g - 1, *prev_qkv)
        prev_qkv = (qb, kb, vb)
        last = xbuf[g, tm:tm + header, :]
        tail_ref[:, cols] = last
        xbuf[g, 0:header, :] = last
    gates_ref[...] = gacc + gate_part(n_groups - 1, *prev_qkv)


def _ml_pre(x, norm_g, layer, w_up, w_conv, b_conv, bq, bk, bv, w_gates, b_gates, hist, j,
            tm, stride, tiles_per_seq):
    T, D = x.shape
    inner = w_conv.shape[-1]
    header = hist.shape[1]
    n_tiles = T // tm
    body = functools.partial(_ml_pre_body, tm=tm, inner=inner, header=header, stride=stride,
                             tiles_per_seq=tiles_per_seq)
    tok = lambda w: pl.BlockSpec((tm, w), lambda i: (i, 0))
    return pl.pallas_call(
        body,
        out_shape=(
            jax.ShapeDtypeStruct((T, inner), BF16),
            jax.ShapeDtypeStruct((T, inner), BF16),
            jax.ShapeDtypeStruct((T, inner), BF16),
            jax.ShapeDtypeStruct((T, inner), F32),
            jax.ShapeDtypeStruct((T, inner), F32),
            jax.ShapeDtypeStruct((T, 2 * LANE), F32),
            jax.ShapeDtypeStruct((n_tiles, header, inner), F32),
        ),
        grid=(n_tiles,),
        in_specs=[
            tok(D),
            _layer_spec((1, D), layer),
            _layer_spec((D, 2 * inner), j),
            _layer_spec((CONV_W, inner), j),
            _layer_spec((1, inner), j),
            _const_spec(bq.shape), _const_spec(bk.shape), _const_spec(bv.shape),
            _const_spec(w_gates.shape), _const_spec(b_gates.shape),
            pl.BlockSpec((None, header, inner), lambda i: (i // tiles_per_seq, 0, 0)),
        ],
        out_specs=(tok(inner), tok(inner), tok(inner), tok(inner), tok(inner), tok(2 * LANE),
                   pl.BlockSpec((None, header, inner), lambda i: (i, 0, 0))),
        scratch_shapes=[pltpu.VMEM((inner // MXU_DIM, header + tm, MXU_DIM), F32)],
        compiler_params=_cparams(1),
        name="ml_pre",
    )(x, norm_g.reshape(norm_g.shape[0], 1, D), w_up, w_conv,
      b_conv.reshape(b_conv.shape[0], 1, inner), bq, bk, bv, w_gates, b_gates, hist)


def _log_sigmoid(x):
    return jnp.minimum(x, 0.0) - jnp.log(1.0 + jnp.exp(-jnp.abs(x)))


def _token_scan(x, op, fill):
    n = x.shape[0]
    tok = lax.broadcasted_iota(jnp.int32, x.shape, 0)
    d = 1
    while d < n:
        x = op(x, jnp.where(tok >= d, pltpu.roll(x, d, axis=0), fill))
        d *= 2
    return x


def _gate_vectors(gates, m_prev, L):
    ig = gates[:, :LANE]
    bt = _token_scan(_log_sigmoid(gates[:, LANE:]), jnp.add, 0.0)
    a = ig - bt
    m = bt + jnp.maximum(_token_scan(a, jnp.maximum, NEG), m_prev)
    dcol = bt - m
    bl, ml = bt[L - 1:L, :], m[L - 1:L, :]
    return dict(a=a, dcol=dcol, w_inter=jnp.exp(m_prev + dcol), exp_neg_m=jnp.exp(-m),
                gs=jnp.exp(bl - bt + ig - ml) * HEAD_DIM ** -0.5,
                g_inter=jnp.exp(m_prev + bl - ml), ml=ml)


def _ml_core_body(*refs, L, has_state, single_chunk):
    if has_state:
        (q_ref, k_ref, v_ref, gates_ref, c0_ref, n0_ref, m0_ref, _,
         h_ref, c_ref, n_ref, m_ref) = refs
    else:
        q_ref, k_ref, v_ref, gates_ref, _, h_ref, c_ref, n_ref, m_ref = refs
    if has_state and single_chunk:
        c_prev, n_prev_ref, m_prev_ref = c0_ref, n0_ref, m0_ref
    else:
        c_prev, n_prev_ref, m_prev_ref = c_ref, n_ref, m_ref

        @pl.when(pl.program_id(1) == 0)
        def _():
            if has_state:
                c_ref[...] = c0_ref[...]
                n_ref[...] = n0_ref[...]
                m_ref[...] = m0_ref[...]
            else:
                c_ref[...] = jnp.zeros_like(c_ref)
                n_ref[...] = jnp.zeros_like(n_ref)
                m_ref[...] = jnp.zeros_like(m_ref)

    log_scale = -0.5 * math.log(HEAD_DIM)
    row = lax.broadcasted_iota(jnp.int32, (L, L), 0)
    col = lax.broadcasted_iota(jnp.int32, (L, L), 1)
    causal = col <= row
    heads = range(N_HEADS)
    hcols = [slice(h * HEAD_DIM, (h + 1) * HEAD_DIM) for h in heads]
    n_prev = n_prev_ref[...]


    qc = [_dot(q_ref[:, hcols[h]], c_prev[h].astype(BF16)) for h in heads]

    gv = _gate_vectors(gates_ref[...], m_prev_ref[0:1, :], L)
    m_ref[...] = jnp.broadcast_to(gv['ml'], m_ref.shape)
    hcol = lambda name, h: gv[name][:, h:h + 1]
    a_shift = gv['a'] + log_scale
    if L % LANE == 0:
        a_rows = a_shift.T
        a_row = [a_rows[h:h + 1, :] for h in heads]
    else:
        a_row = [jnp.sum(jnp.where(row == col, a_shift[:, h:h + 1], 0.0), axis=0, keepdims=True)
                 for h in heads]
    ones_rows = jnp.ones((SUBLANE, L), BF16)

    for h in heads:
        qh, kh, vh = q_ref[:, hcols[h]], k_ref[:, hcols[h]], v_ref[:, hcols[h]]

        g_inter = hcol('g_inter', h)
        kg = kh * hcol('gs', h).astype(BF16)
        upd = lax.dot_general(kg, vh, (((0,), (0,)), ((), ())), preferred_element_type=F32)
        c_ref[h] = g_inter * c_prev[h] + upd
        n_ref[h:h + 1, :] = g_inter * n_prev[h:h + 1, :] + _dot(ones_rows, kg)[0:1, :]

        w_inter = hcol('w_inter', h)
        w_intra = jnp.exp(jnp.where(causal, hcol('dcol', h) + a_row[h], NEG))
        s = lax.dot_general(qh, kh, (((1,), (1,)), ((), ())), preferred_element_type=F32)
        sc = s * w_intra
        num = _dot(sc.astype(BF16), vh) + qc[h] * w_inter
        qn = jnp.sum((qh * n_prev[h:h + 1, :].astype(BF16)).astype(F32), axis=1, keepdims=True)
        nq = jnp.sum(sc, axis=1, keepdims=True) + w_inter * qn
        den = jnp.maximum(jnp.abs(nq), hcol('exp_neg_m', h))
        mu = jnp.mean(num, axis=-1, keepdims=True)
        hc = num - mu
        var = jnp.mean(hc * hc, axis=-1, keepdims=True)
        h_ref[:, hcols[h]] = hc * lax.rsqrt(var + EPS * den * den)


def _ml_core(q, k, v, gates, B, nc, L, j, c_all, state=None):
    T, inner = q.shape
    has_state = state is not None
    n_ml = 2
    body = functools.partial(_ml_core_body, L=L, has_state=has_state, single_chunk=nc == 1)
    tok = lambda w: pl.BlockSpec((L, w), lambda b, c: (b * nc + c, 0))
    c_block = pl.BlockSpec((None, None, N_HEADS, HEAD_DIM, HEAD_DIM), lambda b, c: (j, b, 0, 0, 0))
    in_specs = [tok(inner), tok(inner), tok(inner), tok(2 * LANE)]
    args = [q, k, v, gates]
    if has_state:
        in_specs += [c_block,
                     pl.BlockSpec((None, None, N_HEADS, HEAD_DIM), lambda b, c: (j, b, 0, 0)),
                     pl.BlockSpec((None, SUBLANE, LANE), lambda b, c: (b, 0, 0))]
        args += list(state)
    in_specs.append(pl.BlockSpec(memory_space=pl.ANY))
    if c_all is not None:
        args.append(c_all)
        aliases = {len(args) - 1: 1}
    else:
        args.append(jnp.zeros((SUBLANE, LANE), F32))
        aliases = {}
    return pl.pallas_call(
        body,
        out_shape=(
            jax.ShapeDtypeStruct((T, inner), F32),
            jax.ShapeDtypeStruct((n_ml, B, N_HEADS, HEAD_DIM, HEAD_DIM), F32),
            jax.ShapeDtypeStruct((B, N_HEADS, HEAD_DIM), F32),
            jax.ShapeDtypeStruct((B, SUBLANE, LANE), F32),
        ),
        grid=(B, nc),
        in_specs=in_specs,
        out_specs=(tok(inner), c_block,
                   pl.BlockSpec((None, N_HEADS, HEAD_DIM), lambda b, c: (b, 0, 0)),
                   pl.BlockSpec((None, SUBLANE, LANE), lambda b, c: (b, 0, 0))),
        input_output_aliases=aliases,
        compiler_params=_cparams(2),
        name="ml_core",
    )(*args)


def _ml_post_body(x_ref, hn_ref, xc_ref, z_ref, hg_ref, sk_ref, wd_ref, o_ref):
    out = (hn_ref[...] * hg_ref[...] + sk_ref[...] * xc_ref[...]) * _silu(z_ref[...])
    o_ref[...] = x_ref[...] + _dot(out.astype(BF16), wd_ref[...])


def _ml_post(x, hn, xc, z, hn_g, skip, w_down, j, tm):
    T, D = x.shape
    inner = hn.shape[-1]
    tok = lambda w: pl.BlockSpec((tm, w), lambda i: (i, 0))
    return pl.pallas_call(
        _ml_post_body,
        out_shape=jax.ShapeDtypeStruct((T, D), F32),
        grid=(T // tm,),
        in_specs=[tok(D), tok(inner), tok(inner), tok(inner),
                  _layer_spec((1, inner), j), _layer_spec((1, inner), j),
                  _layer_spec((inner, D), j)],
        out_specs=tok(D),
        compiler_params=_cparams(1),
        name="ml_post",
    )(x, hn, xc, z, hn_g.reshape(hn_g.shape[0], 1, inner), skip.reshape(skip.shape[0], 1, inner),
      w_down)


def _cm_body(x_ref, g_ref, win_ref, bin_ref, lg_ref, wmix_ref, bmix_ref, wout_ref,
             o_ref, vn_ref, *, width):
    x = x_ref[...]
    xn = _rmsnorm(x, g_ref[...]).astype(BF16)
    hp = _dot(xn, win_ref[...]) + bin_ref[...]
    hp = 0.5 * hp * (1.0 + lax.erf(hp * (2.0 ** -0.5)))
    u, v = hp[:, :width], hp[:, width:]
    mu = jnp.mean(v, axis=-1, keepdims=True)
    vc = v - mu
    var = jnp.mean(vc * vc, axis=-1, keepdims=True)
    vn = vc * lax.rsqrt(var + EPS) * lg_ref[...]
    vn_ref[...] = vn
    vnb = vn.astype(BF16)
    gd = width // CM_GROUPS
    bmix = bmix_ref[...]
    parts = []
    for g in range(CM_GROUPS):
        cols = slice(g * gd, (g + 1) * gd)
        mix = _dot(wmix_ref[g], vnb[:, cols]) + bmix[:, g:g + 1]
        parts.append((u[:, cols] * mix).astype(BF16))
    y = jnp.concatenate(parts, axis=-1)
    o_ref[...] = x + _dot(y, wout_ref[...])


def _chunk_mlp(x, norm_g, layer, w_in, b_in, ln_g, wmix, bmix, w_out, j, tm):
    T, D = x.shape
    width = ln_g.shape[-1]
    body = functools.partial(_cm_body, width=width)
    tok = lambda w: pl.BlockSpec((tm, w), lambda i: (i, 0))
    return pl.pallas_call(
        body,
        out_shape=(jax.ShapeDtypeStruct((T, D), F32), jax.ShapeDtypeStruct((T, width), F32)),
        grid=(T // tm,),
        in_specs=[tok(D), _layer_spec((1, D), layer),
                  _layer_spec((D, 2 * width), j), _layer_spec((1, 2 * width), j),
                  _layer_spec((1, width), j),
                  _const_spec(wmix.shape), _const_spec(bmix.shape),
                  _layer_spec((width, D), j)],
        out_specs=(tok(D), tok(width)),
        compiler_params=_cparams(1),
        name="chunk_mlp",
    )(x, norm_g.reshape(norm_g.shape[0], 1, D), w_in, b_in.reshape(b_in.shape[0], 1, 2 * width),
      ln_g.reshape(ln_g.shape[0], 1, width), wmix, bmix, w_out)


def _blockdiag_dense(w):
    per = MXU_DIM // QKV_BLOCK
    w4 = w.reshape(-1, per, QKV_BLOCK, QKV_BLOCK)
    eye = jnp.eye(per, dtype=w.dtype)
    dense = w4[:, :, :, None, :] * eye[None, :, None, :, None]
    return dense.reshape(-1, MXU_DIM, MXU_DIM).astype(BF16)


def _gate_weights(w_ig, b_ig, w_fg, b_fg):
    wpad = jnp.zeros((w_ig.shape[0], LANE - N_HEADS), w_ig.dtype)
    bpad = jnp.zeros((LANE - N_HEADS,), b_ig.dtype)
    w = jnp.concatenate([w_ig, wpad, w_fg, wpad], axis=1)
    b = jnp.concatenate([b_ig, bpad, b_fg, bpad])
    return w.astype(BF16), b.reshape(1, 2 * LANE)


def _trunk(x, grp, p):
    depth = p['norm_ff1'].shape[0]
    tm = grp['tm']
    new_n, new_m, new_conv, new_v = [], [], [], []
    c_all = None
    for i in range(depth):
        j = i // 2
        x = _ffn(x, p['norm_ff1'], p['ffn1_w_gate'], p['ffn1_w_up'], p['ffn1_w_down'], i, tm)
        if i % 2 == 0:
            q, k, v, xc, z, gates, tail = _ml_pre(
                x, p['norm_mix'], i, p['ml_w_up'], p['ml_w_conv'], p['ml_b_conv'],
                p['bq'][j], p['bk'][j], p['bv'][j], p['w_gates'][j], p['b_gates'][j],
                grp['hist'][j], j, grp['pre_tm'], grp['stride'], grp['tiles_per_seq'])
            new_conv.append(grp['conv_out'](tail))
            q, k, v, gates = grp['to_core'](q, k, v, gates)
            state = grp['state'](j)
            hn, c_all, n, m = _ml_core(q, k, v, gates, grp['B'], grp['nc'], grp['L'], j, c_all, state)
            hn = grp['from_core'](hn)
            new_n.append(n)
            new_m.append(m[:, 0, :N_HEADS])
            x = _ml_post(x, hn, xc, z, p['ml_hn_g'], p['ml_skip'], p['ml_w_down'], j, grp['pre_tm'])
        else:
            x, vn = _chunk_mlp(x, p['norm_mix'], i, p['cm_w_in'], p['cm_b_in'], p['cm_ln_g'],
                               grp['wmix'][j], grp['bmix'][j], p['cm_w_out'], j, grp['cm_tm'])
            new_v.append(vn)
        final_g = p['norm_final'] if i == depth - 1 else None
        x = _ffn(x, p['norm_ff2'], p['ffn2_w_gate'], p['ffn2_w_up'], p['ffn2_w_down'], i, tm,
                 final_g=final_g)
    return x, c_all, jnp.stack(new_n), jnp.stack(new_m), jnp.stack(new_conv), new_v


def kernel(x_prompt, x_sample, state_C, state_n, state_m, state_conv, norm_ff1, norm_mix, norm_ff2, norm_final, ffn1_w_gate, ffn1_w_up, ffn1_w_down, ffn2_w_gate, ffn2_w_up, ffn2_w_down, ml_w_up, ml_w_conv, ml_b_conv, ml_w_q, ml_w_k, ml_w_v, ml_w_ig, ml_b_ig, ml_w_fg, ml_b_fg, ml_hn_g, ml_skip, ml_w_down, cm_w_in, cm_b_in, cm_ln_g, cm_w_s, cm_b_s, cm_w_out):
    bp, sp, d_model = x_prompt.shape
    bs, ss, _ = x_sample.shape
    n_ml, inner = ml_w_conv.shape[0], ml_w_conv.shape[-1]
    n_cm, width = cm_ln_g.shape
    hist_rows = CONV_W - 1
    sb = min(64, bs)

    gate_wb = [_gate_weights(ml_w_ig[j], ml_b_ig[j], ml_w_fg[j], ml_b_fg[j]) for j in range(n_ml)]
    p = {
        'norm_ff1': norm_ff1, 'norm_mix': norm_mix, 'norm_ff2': norm_ff2, 'norm_final': norm_final,
        'ffn1_w_gate': ffn1_w_gate.astype(BF16), 'ffn1_w_up': ffn1_w_up.astype(BF16),
        'ffn1_w_down': ffn1_w_down.astype(BF16),
        'ffn2_w_gate': ffn2_w_gate.astype(BF16), 'ffn2_w_up': ffn2_w_up.astype(BF16),
        'ffn2_w_down': ffn2_w_down.astype(BF16),
        'ml_w_up': ml_w_up.astype(BF16), 'ml_w_conv': ml_w_conv, 'ml_b_conv': ml_b_conv,
        'bq': [_blockdiag_dense(ml_w_q[j]) for j in range(n_ml)],
        'bk': [_blockdiag_dense(ml_w_k[j]) for j in range(n_ml)],
        'bv': [_blockdiag_dense(ml_w_v[j]) for j in range(n_ml)],
        'w_gates': [wb[0] for wb in gate_wb], 'b_gates': [wb[1] for wb in gate_wb],
        'ml_hn_g': ml_hn_g, 'ml_skip': ml_skip, 'ml_w_down': ml_w_down.astype(BF16),
        'cm_w_in': cm_w_in.astype(BF16), 'cm_b_in': cm_b_in, 'cm_ln_g': cm_ln_g,
        'cm_w_out': cm_w_out.astype(BF16),
    }

    def mix_weights(L, kron_left):
        causal = jnp.tril(jnp.ones((L, L), dtype=bool))
        ws = jnp.where(causal, cm_w_s[:, :, :L, :L], 0)
        bs_ = cm_b_s[:, :, :L]
        if kron_left:
            reps = kron_left
            eye = jnp.eye(reps, dtype=ws.dtype)
            wm = ws[:, :, None, :, None, :] * eye[None, None, :, None, :, None]
            wm = wm.reshape(n_cm, CM_GROUPS, reps * L, reps * L)
            bm = jnp.tile(bs_, (1, 1, reps))
        else:
            eye = jnp.eye(sb, dtype=ws.dtype)
            wm = ws[:, :, :, None, :, None] * eye[None, None, None, :, None, :]
            wm = wm.reshape(n_cm, CM_GROUPS, L * sb, L * sb)
            bm = jnp.repeat(bs_, sb, axis=2)
        bm = jnp.swapaxes(bm, 1, 2)
        bm = jnp.concatenate([bm, jnp.zeros(bm.shape[:2] + (LANE - CM_GROUPS,), bm.dtype)], axis=2)
        return wm.astype(BF16), bm

    tm_p = 512
    pre_tm_p = 256
    cm_tm_p = 256
    chunk_p = min(256, sp)
    cm_chunk_p = min(128, sp)
    wmix_p, bmix_p = mix_weights(cm_chunk_p, cm_tm_p // cm_chunk_p)
    grp_p = {
        'tm': tm_p, 'pre_tm': pre_tm_p, 'cm_tm': cm_tm_p, 'stride': 1,
        'tiles_per_seq': sp // pre_tm_p,
        'hist': [jnp.zeros((bp, SUBLANE, inner), F32)] * n_ml,
        'conv_out': lambda tail: tail.reshape(bp, sp // pre_tm_p, SUBLANE, inner)[:, -1, SUBLANE - hist_rows:, :],
        'to_core': lambda q, k, v, g: (q, k, v, g),
        'from_core': lambda hn: hn,
        'state': lambda j: None,
        'B': bp, 'nc': sp // chunk_p, 'L': chunk_p,
        'wmix': wmix_p, 'bmix': bmix_p,
    }
    yp, c_p, n_p, m_p, conv_p, _ = _trunk(x_prompt.reshape(bp * sp, d_model), grp_p, p)

    ts = bs * ss
    n_sg = bs // sb
    l_pad = SUBLANE

    def tile_major(a):
        steps, feat = a.shape[1], a.shape[2]
        return a.reshape(n_sg, sb, steps, feat).transpose(0, 2, 1, 3).reshape(n_sg * steps * sb, feat)

    def batch_major(a, steps):
        feat = a.shape[-1]
        return a.reshape(n_sg, steps, sb, feat).transpose(0, 2, 1, 3).reshape(bs, steps, feat)

    def to_core(q, k, v, g):
        def seq_major(a, fill):
            a = batch_major(a, ss)
            pad = jnp.broadcast_to(fill.astype(a.dtype), (bs, l_pad - ss, a.shape[-1]))
            return jnp.concatenate([a, pad], axis=1).reshape(bs * l_pad, a.shape[-1])
        zero = jnp.zeros((inner,), F32)
        lane_pad = jnp.zeros((LANE - N_HEADS,), F32)
        gfill = jnp.concatenate([jnp.full((N_HEADS,), NEG, F32), lane_pad,
                                 jnp.full((N_HEADS,), -NEG, F32), lane_pad])
        return seq_major(q, zero), seq_major(k, zero), seq_major(v, zero), seq_major(g, gfill)

    def from_core(hn):
        return tile_major(hn.reshape(bs, l_pad, inner)[:, :ss])

    def sample_state(j):
        m0 = jnp.pad(state_m[j], ((0, 0), (0, LANE - N_HEADS)))
        return state_C, state_n, jnp.broadcast_to(m0[:, None, :], (bs, SUBLANE, LANE))

    wmix_s, bmix_s = mix_weights(ss, 0)
    tile_s = ss * sb
    hist_s = [tile_major(state_conv[j]).reshape(n_sg, hist_rows * sb, inner) for j in range(n_ml)]
    grp_s = {
        'tm': tile_s, 'pre_tm': tile_s, 'cm_tm': tile_s, 'stride': sb, 'tiles_per_seq': 1,
        'hist': hist_s,
        'conv_out': lambda tail: batch_major(tail.reshape(n_sg * hist_rows * sb, inner), hist_rows),
        'to_core': to_core, 'from_core': from_core, 'state': sample_state,
        'B': bs, 'nc': 1, 'L': l_pad,
        'wmix': wmix_s, 'bmix': bmix_s,
    }
    ys, c_s, n_s, m_s, conv_s, v_rows = _trunk(tile_major(x_sample), grp_s, p)

    y_prompt = yp.reshape(bp, sp, d_model)
    y_sample = batch_major(ys, ss)
    v_sample = jnp.stack([batch_major(vr, ss) for vr in v_rows])
    return (y_prompt, y_sample, c_p, n_p, m_p, conv_p, c_s, n_s, m_s, conv_s, v_sample)
```

```python
import functools
import math

import jax
import jax.numpy as jnp
from jax import lax
from jax.experimental import pallas as pl
from jax.experimental.pallas import tpu as pltpu

F32 = jnp.float32
BF16 = jnp.bfloat16

EPS = 1e-6
N_HEADS = 4
HEAD_DIM = 512
QKV_BLOCK = 4
CONV_W = 4
CM_GROUPS = 4
NEG = -1e30

LANE = 128
SUBLANE = 8
MXU_DIM = 256
VMEM_LIMIT = 56 * 1024 * 1024


def _cparams(n_axes):
    return pltpu.CompilerParams(dimension_semantics=("arbitrary",) * n_axes,
                                vmem_limit_bytes=VMEM_LIMIT)


def _const_spec(shape):
    nd = len(shape)
    return pl.BlockSpec(shape, lambda *_: (0,) * nd, pipeline_mode=pl.Buffered(1))


def _layer_spec(shape, layer):
    nd = len(shape)
    return pl.BlockSpec((None,) + tuple(shape), lambda *_: (layer,) + (0,) * nd,
                        pipeline_mode=pl.Buffered(1))


def _rmsnorm(x, g):
    return x * lax.rsqrt(jnp.mean(x * x, axis=-1, keepdims=True) + EPS) * g


def _silu(x):
    return x * jax.nn.sigmoid(x)


def _dot(a, b):
    return jnp.dot(a, b, preferred_element_type=F32)


def _ffn_body(*refs, ff_chunks, final_norm, mixer_out):
    if mixer_out:
        (x_ref, g_ref, wg_ref, wu_ref, wd_ref, gf_ref,
         hn_ref, xc_ref, z_ref, hg_ref, sk_ref, wdm_ref, o_ref) = refs
    else:
        x_ref, g_ref, wg_ref, wu_ref, wd_ref, gf_ref, o_ref = refs
    x = x_ref[...]
    if mixer_out:
        for lo in range(0, hn_ref.shape[1], HEAD_DIM):
            cols = slice(lo, lo + HEAD_DIM)
            out = ((hn_ref[:, cols].astype(F32) * hg_ref[:, cols]
                    + sk_ref[:, cols] * xc_ref[:, cols].astype(F32))
                   * _silu(z_ref[:, cols].astype(F32)))
            x = x + _dot(out.astype(BF16), wdm_ref[cols, :])
    xn = _rmsnorm(x, g_ref[...]).astype(BF16)
    acc = None
    for lo, hi in ff_chunks:
        gate = _dot(xn, wg_ref[:, lo:hi])
        up = _dot(xn, wu_ref[:, lo:hi])
        h = (_silu(gate) * up).astype(BF16)
        part = _dot(h, wd_ref[lo:hi, :])
        acc = part if acc is None else acc + part
    y = x + 0.5 * acc
    if final_norm:
        y = _rmsnorm(y, gf_ref[...])
    o_ref[...] = y


def _ffn(x, norm_g, w_gate, w_up, w_down, layer, tm, final_g=None, mixer=None):
    T, D = x.shape
    d_ff = w_gate.shape[-1]
    step = 4 * MXU_DIM
    ff_chunks = tuple((lo, min(lo + step, d_ff)) for lo in range(0, d_ff, step))
    final_norm = final_g is not None
    gf = final_g if final_norm else norm_g[layer]
    body = functools.partial(_ffn_body, ff_chunks=ff_chunks, final_norm=final_norm,
                             mixer_out=mixer is not None)
    tok = lambda w: pl.BlockSpec((tm, w), lambda i: (i, 0))
    in_specs = [tok(D), _layer_spec((1, D), layer), _layer_spec((D, d_ff), layer),
                _layer_spec((D, d_ff), layer), _layer_spec((d_ff, D), layer), _const_spec((1, D))]
    args = [x, norm_g.reshape(norm_g.shape[0], 1, D), w_gate, w_up, w_down, gf.reshape(1, D)]
    if mixer is not None:
        hn, xc, z, gain, skip, w_down_m, j = mixer
        inner = hn.shape[-1]
        in_specs += [tok(inner), tok(inner), tok(inner), _layer_spec((1, inner), j),
                     _layer_spec((1, inner), j), _layer_spec((inner, D), j)]
        args += [hn, xc, z, gain.reshape(gain.shape[0], 1, inner),
                 skip.reshape(skip.shape[0], 1, inner), w_down_m]
    return pl.pallas_call(
        body,
        out_shape=jax.ShapeDtypeStruct((T, D), F32),
        grid=(T // tm,),
        in_specs=in_specs,
        out_specs=tok(D),
        compiler_params=_cparams(1),
        name="ffn",
    )(*args)


def _ml_pre_body(x_ref, g_ref, wup_ref, wc_ref, bc_ref, bq_ref, bk_ref, bv_ref, wgt_ref, bgt_ref,
                 hist_ref, q_ref, k_ref, v_ref, xc_ref, z_ref, gates_ref, tail_ref, xbuf,
                 *, tm, inner, header, stride, tiles_per_seq):
    i = pl.program_id(0)

    n_groups = inner // MXU_DIM

    @pl.when(i % tiles_per_seq == 0)
    def _():
        for g in range(n_groups):
            xbuf[g, 0:header, :] = hist_ref[:, g * MXU_DIM:(g + 1) * MXU_DIM]

    xn = _rmsnorm(x_ref[...], g_ref[...]).astype(BF16)

    def up_proj(g):
        cols = slice(g * MXU_DIM, (g + 1) * MXU_DIM)
        zcols = slice(inner + g * MXU_DIM, inner + (g + 1) * MXU_DIM)
        xm_g = _dot(xn, wup_ref[:, cols])
        z_ref[:, cols] = _dot(xn, wup_ref[:, zcols]).astype(BF16)
        return xm_g

    def gate_part(g, qb, kb, vb):
        rows = lambda base: slice(base + g * MXU_DIM, base + (g + 1) * MXU_DIM)
        return (_dot(qb, wgt_ref[rows(0), :]) + _dot(kb, wgt_ref[rows(inner), :])
                + _dot(vb, wgt_ref[rows(2 * inner), :]))

    gacc = bgt_ref[...]
    prev_qkv = None
    xm_next = up_proj(0)
    for g in range(n_groups):
        cols = slice(g * MXU_DIM, (g + 1) * MXU_DIM)
        xm = xm_next
        if g + 1 < n_groups:
            xm_next = up_proj(g + 1)
        xbuf[g, header:header + tm, :] = xm
        acc = bc_ref[:, cols] + wc_ref[CONV_W - 1:CONV_W, cols] * xm
        for d in range(1, CONV_W):
            lo = header - d * stride
            acc = acc + wc_ref[CONV_W - 1 - d:CONV_W - d, cols] * xbuf[g, lo:lo + tm, :]
        xcb = _silu(acc).astype(BF16)
        xc_ref[:, cols] = xcb
        qb = _dot(xcb, bq_ref[g]).astype(BF16)
        kb = _dot(xcb, bk_ref[g]).astype(BF16)
        vb = _dot(xm.astype(BF16), bv_ref[g]).astype(BF16)
        q_ref[:, cols] = qb
        k_ref[:, cols] = kb
        v_ref[:, cols] = vb
        if g > 0:
            gacc = gacc + gate_part(g - 1, *prev_qkv)
        prev_qkv = (qb, kb, vb)
        last = xbuf[g, tm:tm + header, :]
        tail_ref[:, cols] = last
        xbuf[g, 0:header, :] = last
    gates_ref[...] = gacc + gate_part(n_groups - 1, *prev_qkv)


def _ml_pre(x, norm_g, layer, w_up, w_conv, b_conv, bq, bk, bv, w_gates, b_gates, hist, j,
            tm, stride, tiles_per_seq):
    T, D = x.shape
    inner = w_conv.shape[-1]
    header = hist.shape[1]
    n_tiles = T // tm
    body = functools.partial(_ml_pre_body, tm=tm, inner=inner, header=header, stride=stride,
                             tiles_per_seq=tiles_per_seq)
    tok = lambda w: pl.BlockSpec((tm, w), lambda i: (i, 0))
    return pl.pallas_call(
        body,
        out_shape=(
            jax.ShapeDtypeStruct((T, inner), BF16),
            jax.ShapeDtypeStruct((T, inner), BF16),
            jax.ShapeDtypeStruct((T, inner), BF16),
            jax.ShapeDtypeStruct((T, inner), BF16),
            jax.ShapeDtypeStruct((T, inner), BF16),
            jax.ShapeDtypeStruct((T, 2 * LANE), F32),
            jax.ShapeDtypeStruct((n_tiles, header, inner), F32),
        ),
        grid=(n_tiles,),
        in_specs=[
            tok(D),
            _layer_spec((1, D), layer),
            _layer_spec((D, 2 * inner), j),
            _layer_spec((CONV_W, inner), j),
            _layer_spec((1, inner), j),
            _const_spec(bq.shape), _const_spec(bk.shape), _const_spec(bv.shape),
            _const_spec(w_gates.shape), _const_spec(b_gates.shape),
            pl.BlockSpec((None, header, inner), lambda i: (i // tiles_per_seq, 0, 0)),
        ],
        out_specs=(tok(inner), tok(inner), tok(inner), tok(inner), tok(inner), tok(2 * LANE),
                   pl.BlockSpec((None, header, inner), lambda i: (i, 0, 0))),
        scratch_shapes=[pltpu.VMEM((inner // MXU_DIM, header + tm, MXU_DIM), F32)],
        compiler_params=_cparams(1),
        name="ml_pre",
    )(x, norm_g.reshape(norm_g.shape[0], 1, D), w_up, w_conv,
      b_conv.reshape(b_conv.shape[0], 1, inner), bq, bk, bv, w_gates, b_gates, hist)


def _log_sigmoid(x):
    return jnp.minimum(x, 0.0) - jnp.log(1.0 + jnp.exp(-jnp.abs(x)))


def _token_scan(x, op, fill):
    n = x.shape[0]
    tok = lax.broadcasted_iota(jnp.int32, x.shape, 0)
    d = 1
    while d < n:
        x = op(x, jnp.where(tok >= d, pltpu.roll(x, d, axis=0), fill))
        d *= 2
    return x


def _gate_vectors(gates, m_prev, L):
    ig = gates[:, :LANE]
    bt = _token_scan(_log_sigmoid(gates[:, LANE:]), jnp.add, 0.0)
    a = ig - bt
    m = bt + jnp.maximum(_token_scan(a, jnp.maximum, NEG), m_prev)
    dcol = bt - m
    bl, ml = bt[L - 1:L, :], m[L - 1:L, :]
    return dict(a=a, dcol=dcol, w_inter=jnp.exp(m_prev + dcol), exp_neg_m=jnp.exp(-m),
                gs=jnp.exp(bl - bt + ig - ml) * HEAD_DIM ** -0.5,
                g_inter=jnp.exp(m_prev + bl - ml), ml=ml)


def _ml_core_body(*refs, L, has_state, single_chunk):
    if has_state:
        (q_ref, k_ref, v_ref, gates_ref, c0_ref, n0_ref, m0_ref, _,
         h_ref, c_ref, n_ref, m_ref) = refs
    else:
        q_ref, k_ref, v_ref, gates_ref, _, h_ref, c_ref, n_ref, m_ref = refs
    if has_state and single_chunk:
        c_prev, n_prev_ref, m_prev_ref = c0_ref, n0_ref, m0_ref
    else:
        c_prev, n_prev_ref, m_prev_ref = c_ref, n_ref, m_ref

        @pl.when(pl.program_id(1) == 0)
        def _():
            if has_state:
                c_ref[...] = c0_ref[...]
                n_ref[...] = n0_ref[...]
                m_ref[...] = m0_ref[...]
            else:
                c_ref[...] = jnp.zeros_like(c_ref)
                n_ref[...] = jnp.zeros_like(n_ref)
                m_ref[...] = jnp.zeros_like(m_ref)

    log_scale = -0.5 * math.log(HEAD_DIM)
    row = lax.broadcasted_iota(jnp.int32, (L, L), 0)
    col = lax.broadcasted_iota(jnp.int32, (L, L), 1)
    causal = col <= row
    heads = range(N_HEADS)
    hcols = [slice(h * HEAD_DIM, (h + 1) * HEAD_DIM) for h in heads]
    n_prev = n_prev_ref[...]


    qc = [_dot(q_ref[:, hcols[h]], c_prev[h].astype(BF16)) for h in heads]

    gv = _gate_vectors(gates_ref[...], m_prev_ref[0:1, :], L)
    m_ref[...] = jnp.broadcast_to(gv['ml'], m_ref.shape)
    hcol = lambda name, h: gv[name][:, h:h + 1]
    a_shift = gv['a'] + log_scale
    if L % LANE == 0:
        a_rows = a_shift.T
        a_row = [a_rows[h:h + 1, :] for h in heads]
    else:
        a_row = [jnp.sum(jnp.where(row == col, a_shift[:, h:h + 1], 0.0), axis=0, keepdims=True)
                 for h in heads]
    ones_rows = jnp.ones((SUBLANE, L), BF16)

    for h in heads:
        qh, kh, vh = q_ref[:, hcols[h]], k_ref[:, hcols[h]], v_ref[:, hcols[h]]

        g_inter = hcol('g_inter', h)
        kg = kh * hcol('gs', h).astype(BF16)
        upd = lax.dot_general(kg, vh, (((0,), (0,)), ((), ())), preferred_element_type=F32)
        c_ref[h] = g_inter * c_prev[h] + upd
        n_ref[h:h + 1, :] = g_inter * n_prev[h:h + 1, :] + _dot(ones_rows, kg)[0:1, :]

        w_inter = hcol('w_inter', h)
        w_intra = jnp.exp(jnp.where(causal, hcol('dcol', h) + a_row[h], NEG))
        s = lax.dot_general(qh, kh, (((1,), (1,)), ((), ())), preferred_element_type=F32)
        sc = s * w_intra
        num = _dot(sc.astype(BF16), vh) + qc[h] * w_inter
        qn = jnp.sum((qh * n_prev[h:h + 1, :].astype(BF16)).astype(F32), axis=1, keepdims=True)
        nq = jnp.sum(sc, axis=1, keepdims=True) + w_inter * qn
        den = jnp.maximum(jnp.abs(nq), hcol('exp_neg_m', h))
        mu = jnp.mean(num, axis=-1, keepdims=True)
        hc = num - mu
        var = jnp.mean(hc * hc, axis=-1, keepdims=True)
        h_ref[:, hcols[h]] = (hc * lax.rsqrt(var + EPS * den * den)).astype(BF16)


def _ml_core(q, k, v, gates, B, nc, L, j, c_all, state=None):
    T, inner = q.shape
    has_state = state is not None
    n_ml = 2
    body = functools.partial(_ml_core_body, L=L, has_state=has_state, single_chunk=nc == 1)
    tok = lambda w: pl.BlockSpec((L, w), lambda b, c: (b * nc + c, 0))
    c_block = pl.BlockSpec((None, None, N_HEADS, HEAD_DIM, HEAD_DIM), lambda b, c: (j, b, 0, 0, 0))
    in_specs = [tok(inner), tok(inner), tok(inner), tok(2 * LANE)]
    args = [q, k, v, gates]
    if has_state:
        in_specs += [c_block,
                     pl.BlockSpec((None, None, N_HEADS, HEAD_DIM), lambda b, c: (j, b, 0, 0)),
                     pl.BlockSpec((None, SUBLANE, LANE), lambda b, c: (b, 0, 0))]
        args += list(state)
    in_specs.append(pl.BlockSpec(memory_space=pl.ANY))
    if c_all is not None:
        args.append(c_all)
        aliases = {len(args) - 1: 1}
    else:
        args.append(jnp.zeros((SUBLANE, LANE), F32))
        aliases = {}
    return pl.pallas_call(
        body,
        out_shape=(
            jax.ShapeDtypeStruct((T, inner), BF16),
            jax.ShapeDtypeStruct((n_ml, B, N_HEADS, HEAD_DIM, HEAD_DIM), F32),
            jax.ShapeDtypeStruct((B, N_HEADS, HEAD_DIM), F32),
            jax.ShapeDtypeStruct((B, SUBLANE, LANE), F32),
        ),
        grid=(B, nc),
        in_specs=in_specs,
        out_specs=(tok(inner), c_block,
                   pl.BlockSpec((None, N_HEADS, HEAD_DIM), lambda b, c: (b, 0, 0)),
                   pl.BlockSpec((None, SUBLANE, LANE), lambda b, c: (b, 0, 0))),
        input_output_aliases=aliases,
        compiler_params=_cparams(2),
        name="ml_core",
    )(*args)


def _gelu(x):
    return 0.5 * x * (1.0 + lax.erf(x * (2.0 ** -0.5)))


def _cm_body(*refs, width, sub, emit_vn):
    if emit_vn:
        x_ref, g_ref, win_ref, bin_ref, lg_ref, wmix_ref, bmix_ref, wout_ref, o_ref, vn_ref = refs
    else:
        x_ref, g_ref, win_ref, bin_ref, lg_ref, wmix_ref, bmix_ref, wout_ref, o_ref = refs
    gd = width // CM_GROUPS
    n_sub = x_ref.shape[0] // sub

    def in_proj(s):
        x = x_ref[s * sub:(s + 1) * sub, :]
        xn = _rmsnorm(x, g_ref[...]).astype(BF16)
        v = _gelu(_dot(xn, win_ref[:, width:]) + bin_ref[:, width:])
        u = _gelu(_dot(xn, win_ref[:, :width]) + bin_ref[:, :width])
        return x, u, v

    def gate_and_out(s, x, u, v):
        rows = slice(s * sub, (s + 1) * sub)
        mu = jnp.mean(v, axis=-1, keepdims=True)
        vc = v - mu
        var = jnp.mean(vc * vc, axis=-1, keepdims=True)
        vn = vc * lax.rsqrt(var + EPS) * lg_ref[...]
        if emit_vn:
            vn_ref[rows, :] = vn
        vnb = vn.astype(BF16)
        bmix = bmix_ref[...]
        parts = []
        for g in range(CM_GROUPS):
            cols = slice(g * gd, (g + 1) * gd)
            mix = _dot(wmix_ref[g], vnb[:, cols]) + bmix[:, g:g + 1]
            parts.append((u[:, cols] * mix).astype(BF16))
        o_ref[rows, :] = x + _dot(jnp.concatenate(parts, axis=-1), wout_ref[...])

    nxt = in_proj(0)
    for s in range(n_sub):
        cur = nxt
        if s + 1 < n_sub:
            nxt = in_proj(s + 1)
        gate_and_out(s, *cur)


def _chunk_mlp(x, norm_g, layer, w_in, b_in, ln_g, wmix, bmix, w_out, j, tm, emit_vn):
    T, D = x.shape
    width = ln_g.shape[-1]
    body = functools.partial(_cm_body, width=width, sub=wmix.shape[1], emit_vn=emit_vn)
    tok = lambda w: pl.BlockSpec((tm, w), lambda i: (i, 0))
    out_shape = [jax.ShapeDtypeStruct((T, D), F32)]
    out_specs = [tok(D)]
    if emit_vn:
        out_shape.append(jax.ShapeDtypeStruct((T, width), F32))
        out_specs.append(tok(width))
    return pl.pallas_call(
        body,
        out_shape=tuple(out_shape),
        grid=(T // tm,),
        in_specs=[tok(D), _layer_spec((1, D), layer),
                  _layer_spec((D, 2 * width), j), _layer_spec((1, 2 * width), j),
                  _layer_spec((1, width), j),
                  _const_spec(wmix.shape), _const_spec(bmix.shape),
                  _layer_spec((width, D), j)],
        out_specs=tuple(out_specs),
        compiler_params=_cparams(1),
        name="chunk_mlp",
    )(x, norm_g.reshape(norm_g.shape[0], 1, D), w_in, b_in.reshape(b_in.shape[0], 1, 2 * width),
      ln_g.reshape(ln_g.shape[0], 1, width), wmix, bmix, w_out)


def _blockdiag_dense(w):
    rows = w.reshape(-1, MXU_DIM, QKV_BLOCK)
    r = lax.broadcasted_iota(jnp.int32, (MXU_DIM, MXU_DIM), 0)
    c = lax.broadcasted_iota(jnp.int32, (MXU_DIM, MXU_DIM), 1)
    spread = (lax.broadcasted_iota(jnp.int32, (QKV_BLOCK, MXU_DIM), 1) % QKV_BLOCK
              == lax.broadcasted_iota(jnp.int32, (QKV_BLOCK, MXU_DIM), 0)).astype(w.dtype)
    dense = jnp.einsum('grd,dc->grc', rows, spread, precision=lax.Precision.HIGHEST)
    return jnp.where(r // QKV_BLOCK == c // QKV_BLOCK, dense, 0).astype(BF16)


def _gate_weights(w_ig, b_ig, w_fg, b_fg):
    wpad = jnp.zeros((w_ig.shape[0], LANE - N_HEADS), w_ig.dtype)
    bpad = jnp.zeros((LANE - N_HEADS,), b_ig.dtype)
    w = jnp.concatenate([w_ig, wpad, w_fg, wpad], axis=1)
    b = jnp.concatenate([b_ig, bpad, b_fg, bpad])
    return w.astype(BF16), b.reshape(1, 2 * LANE)


def _trunk(x, grp, p):
    depth = p['norm_ff1'].shape[0]
    tm = grp['tm']
    new_n, new_m, new_conv, new_v = [], [], [], []
    c_all = None
    for i in range(depth):
        j = i // 2
        x = _ffn(x, p['norm_ff1'], p['ffn1_w_gate'], p['ffn1_w_up'], p['ffn1_w_down'], i, tm)
        if i % 2 == 0:
            q, k, v, xc, z, gates, tail = _ml_pre(
                x, p['norm_mix'], i, p['ml_w_up'], p['ml_w_conv'], p['ml_b_conv'],
                p['bq'][j], p['bk'][j], p['bv'][j], p['w_gates'][j], p['b_gates'][j],
                grp['hist'][j], j, grp['pre_tm'], grp['stride'], grp['tiles_per_seq'])
            new_conv.append(grp['conv_out'](tail))
            q, k, v, gates = grp['to_core'](q, k, v, gates)
            state = grp['state'](j)
            hn, c_all, n, m = _ml_core(q, k, v, gates, grp['B'], grp['nc'], grp['L'], j, c_all, state)
            hn = grp['from_core'](hn)
            new_n.append(n)
            new_m.append(m[:, 0, :N_HEADS])
            mixer = (hn, xc, z, p['ml_hn_g'], p['ml_skip'], p['ml_w_down'], j)
            tm2 = grp['tm_mixer']
        else:
            outs = _chunk_mlp(x, p['norm_mix'], i, p['cm_w_in'], p['cm_b_in'], p['cm_ln_g'],
                              grp['wmix'][j], grp['bmix'][j], p['cm_w_out'], j, grp['cm_tm'],
                              emit_vn=grp['emit_vn'])
            x = outs[0]
            new_v.append(outs[1] if grp['emit_vn'] else None)
            mixer = None
            tm2 = tm
        final_g = p['norm_final'] if i == depth - 1 else None
        x = _ffn(x, p['norm_ff2'], p['ffn2_w_gate'], p['ffn2_w_up'], p['ffn2_w_down'], i, tm2,
                 final_g=final_g, mixer=mixer)
    return x, c_all, jnp.stack(new_n), jnp.stack(new_m), jnp.stack(new_conv), new_v


def kernel(x_prompt, x_sample, state_C, state_n, state_m, state_conv, norm_ff1, norm_mix, norm_ff2, norm_final, ffn1_w_gate, ffn1_w_up, ffn1_w_down, ffn2_w_gate, ffn2_w_up, ffn2_w_down, ml_w_up, ml_w_conv, ml_b_conv, ml_w_q, ml_w_k, ml_w_v, ml_w_ig, ml_b_ig, ml_w_fg, ml_b_fg, ml_hn_g, ml_skip, ml_w_down, cm_w_in, cm_b_in, cm_ln_g, cm_w_s, cm_b_s, cm_w_out):
    bp, sp, d_model = x_prompt.shape
    bs, ss, _ = x_sample.shape
    n_ml, inner = ml_w_conv.shape[0], ml_w_conv.shape[-1]
    n_cm, width = cm_ln_g.shape
    hist_rows = CONV_W - 1
    sb = min(64, bs)

    gate_wb = [_gate_weights(ml_w_ig[j], ml_b_ig[j], ml_w_fg[j], ml_b_fg[j]) for j in range(n_ml)]
    p = {
        'norm_ff1': norm_ff1, 'norm_mix': norm_mix, 'norm_ff2': norm_ff2, 'norm_final': norm_final,
        'ffn1_w_gate': ffn1_w_gate.astype(BF16), 'ffn1_w_up': ffn1_w_up.astype(BF16),
        'ffn1_w_down': ffn1_w_down.astype(BF16),
        'ffn2_w_gate': ffn2_w_gate.astype(BF16), 'ffn2_w_up': ffn2_w_up.astype(BF16),
        'ffn2_w_down': ffn2_w_down.astype(BF16),
        'ml_w_up': ml_w_up.astype(BF16), 'ml_w_conv': ml_w_conv, 'ml_b_conv': ml_b_conv,
        'bq': [_blockdiag_dense(ml_w_q[j]) for j in range(n_ml)],
        'bk': [_blockdiag_dense(ml_w_k[j]) for j in range(n_ml)],
        'bv': [_blockdiag_dense(ml_w_v[j]) for j in range(n_ml)],
        'w_gates': [wb[0] for wb in gate_wb], 'b_gates': [wb[1] for wb in gate_wb],
        'ml_hn_g': ml_hn_g, 'ml_skip': ml_skip, 'ml_w_down': ml_w_down.astype(BF16),
        'cm_w_in': cm_w_in.astype(BF16), 'cm_b_in': cm_b_in, 'cm_ln_g': cm_ln_g,
        'cm_w_out': cm_w_out.astype(BF16),
    }

    def mix_weights(L, kron_left):
        causal = jnp.tril(jnp.ones((L, L), dtype=bool))
        ws = jnp.where(causal, cm_w_s[:, :, :L, :L], 0)
        bs_ = cm_b_s[:, :, :L]
        if kron_left:
            reps = kron_left
            eye = jnp.eye(reps, dtype=ws.dtype)
            wm = ws[:, :, None, :, None, :] * eye[None, None, :, None, :, None]
            wm = wm.reshape(n_cm, CM_GROUPS, reps * L, reps * L)
            bm = jnp.tile(bs_, (1, 1, reps))
        else:
            eye = jnp.eye(sb, dtype=ws.dtype)
            wm = ws[:, :, :, None, :, None] * eye[None, None, None, :, None, :]
            wm = wm.reshape(n_cm, CM_GROUPS, L * sb, L * sb)
            bm = jnp.repeat(bs_, sb, axis=2)
        bm = jnp.swapaxes(bm, 1, 2)
        bm = jnp.concatenate([bm, jnp.zeros(bm.shape[:2] + (LANE - CM_GROUPS,), bm.dtype)], axis=2)
        return wm.astype(BF16), bm

    tm_p = 1024
    pre_tm_p = 256
    cm_tm_p = 1024
    chunk_p = min(256, sp)
    cm_chunk_p = min(128, sp)
    wmix_p, bmix_p = mix_weights(cm_chunk_p, MXU_DIM // cm_chunk_p)
    grp_p = {
        'tm': tm_p, 'tm_mixer': 512, 'pre_tm': pre_tm_p, 'cm_tm': cm_tm_p, 'emit_vn': False,
        'stride': 1,
        'tiles_per_seq': sp // pre_tm_p,
        'hist': [jnp.zeros((bp, SUBLANE, inner), F32)] * n_ml,
        'conv_out': lambda tail: tail.reshape(bp, sp // pre_tm_p, SUBLANE, inner)[:, -1, SUBLANE - hist_rows:, :],
        'to_core': lambda q, k, v, g: (q, k, v, g),
        'from_core': lambda hn: hn,
        'state': lambda j: None,
        'B': bp, 'nc': sp // chunk_p, 'L': chunk_p,
        'wmix': wmix_p, 'bmix': bmix_p,
    }
    yp, c_p, n_p, m_p, conv_p, _ = _trunk(x_prompt.reshape(bp * sp, d_model), grp_p, p)

    ts = bs * ss
    n_sg = bs // sb
    l_pad = SUBLANE

    def tile_major(a):
        steps, feat = a.shape[1], a.shape[2]
        return a.reshape(n_sg, sb, steps, feat).transpose(0, 2, 1, 3).reshape(n_sg * steps * sb, feat)

    def batch_major(a, steps):
        feat = a.shape[-1]
        return a.reshape(n_sg, steps, sb, feat).transpose(0, 2, 1, 3).reshape(bs, steps, feat)

    def to_core(q, k, v, g):
        def seq_major(a, fill):
            a = batch_major(a, ss)
            pad = jnp.broadcast_to(fill.astype(a.dtype), (bs, l_pad - ss, a.shape[-1]))
            return jnp.concatenate([a, pad], axis=1).reshape(bs * l_pad, a.shape[-1])
        zero = jnp.zeros((inner,), F32)
        lane_pad = jnp.zeros((LANE - N_HEADS,), F32)
        gfill = jnp.concatenate([jnp.full((N_HEADS,), NEG, F32), lane_pad,
                                 jnp.full((N_HEADS,), -NEG, F32), lane_pad])
        return seq_major(q, zero), seq_major(k, zero), seq_major(v, zero), seq_major(g, gfill)

    def from_core(hn):
        return tile_major(hn.reshape(bs, l_pad, inner)[:, :ss])

    def sample_state(j):
        m0 = jnp.pad(state_m[j], ((0, 0), (0, LANE - N_HEADS)))
        return state_C, state_n, jnp.broadcast_to(m0[:, None, :], (bs, SUBLANE, LANE))

    wmix_s, bmix_s = mix_weights(ss, 0)
    tile_s = ss * sb
    hist_s = [tile_major(state_conv[j]).reshape(n_sg, hist_rows * sb, inner) for j in range(n_ml)]
    grp_s = {
        'tm': tile_s, 'tm_mixer': tile_s, 'pre_tm': tile_s, 'cm_tm': tile_s, 'emit_vn': True,
        'stride': sb,
        'tiles_per_seq': 1,
        'hist': hist_s,
        'conv_out': lambda tail: batch_major(tail.reshape(n_sg * hist_rows * sb, inner), hist_rows),
        'to_core': to_core, 'from_core': from_core, 'state': sample_state,
        'B': bs, 'nc': 1, 'L': l_pad,
        'wmix': wmix_s, 'bmix': bmix_s,
    }
    ys, c_s, n_s, m_s, conv_s, v_rows = _trunk(tile_major(x_sample), grp_s, p)

    y_prompt = yp.reshape(bp, sp, d_model)
    y_sample = batch_major(ys, ss)
    v_sample = jnp.stack([batch_major(vr, ss) for vr in v_rows])
    return (y_prompt, y_sample, c_p, n_p, m_p, conv_p, c_s, n_s, m_s, conv_s, v_sample)
```

```python
import functools
import math

import jax
import jax.numpy as jnp
from jax import lax
from jax.experimental import pallas as pl
from jax.experimental.pallas import tpu as pltpu

F32 = jnp.float32
BF16 = jnp.bfloat16

EPS = 1e-6
N_HEADS = 4
HEAD_DIM = 512
QKV_BLOCK = 4
CONV_W = 4
CM_GROUPS = 4
NEG = -1e30

LANE = 128
SUBLANE = 8
MXU_DIM = 256
VMEM_LIMIT = 56 * 1024 * 1024


def _cparams(n_axes):
    return pltpu.CompilerParams(dimension_semantics=("arbitrary",) * n_axes,
                                vmem_limit_bytes=VMEM_LIMIT)


def _const_spec(shape):
    nd = len(shape)
    return pl.BlockSpec(shape, lambda *_: (0,) * nd, pipeline_mode=pl.Buffered(1))


def _layer_spec(shape, layer):
    nd = len(shape)
    return pl.BlockSpec((None,) + tuple(shape), lambda *_: (layer,) + (0,) * nd,
                        pipeline_mode=pl.Buffered(1))


def _rmsnorm(x, g):
    return x * lax.rsqrt(jnp.mean(x * x, axis=-1, keepdims=True) + EPS) * g


def _silu(x):
    return x * jax.nn.sigmoid(x)


def _dot(a, b):
    return jnp.dot(a, b, preferred_element_type=F32)


def _ffn_body(*refs, ff_chunks, final_norm, mixer_out):
    if mixer_out:
        (x_ref, g_ref, wg_ref, wu_ref, wd_ref, gf_ref,
         hn_ref, xc_ref, z_ref, hg_ref, sk_ref, wdm_ref, o_ref) = refs
    else:
        x_ref, g_ref, wg_ref, wu_ref, wd_ref, gf_ref, o_ref = refs
    x = x_ref[...]
    if mixer_out:
        for lo in range(0, hn_ref.shape[1], HEAD_DIM):
            cols = slice(lo, lo + HEAD_DIM)
            out = ((hn_ref[:, cols].astype(F32) * hg_ref[:, cols]
                    + sk_ref[:, cols] * xc_ref[:, cols].astype(F32))
                   * _silu(z_ref[:, cols].astype(F32)))
            x = x + _dot(out.astype(BF16), wdm_ref[cols, :])
    xn = _rmsnorm(x, g_ref[...]).astype(wg_ref.dtype)
    acc = None
    for lo, hi in ff_chunks:
        gate = _dot(xn, wg_ref[:, lo:hi])
        up = _dot(xn, wu_ref[:, lo:hi])
        h = (_silu(gate) * up).astype(wd_ref.dtype)
        part = _dot(h, wd_ref[lo:hi, :])
        acc = part if acc is None else acc + part
    y = x + 0.5 * acc
    if final_norm:
        y = _rmsnorm(y, gf_ref[...])
    o_ref[...] = y


def _ffn(x, norm_g, w_gate, w_up, w_down, layer, tm, final_g=None, mixer=None, w_layer=None):
    T, D = x.shape
    d_ff = w_gate.shape[-1]
    step = 4 * MXU_DIM
    ff_chunks = tuple((lo, min(lo + step, d_ff)) for lo in range(0, d_ff, step))
    final_norm = final_g is not None
    gf = final_g if final_norm else norm_g[layer]
    body = functools.partial(_ffn_body, ff_chunks=ff_chunks, final_norm=final_norm,
                             mixer_out=mixer is not None)
    tok = lambda w: pl.BlockSpec((tm, w), lambda i: (i, 0))
    wl = layer if w_layer is None else w_layer
    in_specs = [tok(D), _layer_spec((1, D), layer), _layer_spec((D, d_ff), wl),
                _layer_spec((D, d_ff), wl), _layer_spec((d_ff, D), wl), _const_spec((1, D))]
    args = [x, norm_g.reshape(norm_g.shape[0], 1, D), w_gate, w_up, w_down, gf.reshape(1, D)]
    if mixer is not None:
        hn, xc, z, gain, skip, w_down_m, j = mixer
        inner = hn.shape[-1]
        in_specs += [tok(inner), tok(inner), tok(inner), _layer_spec((1, inner), j),
                     _layer_spec((1, inner), j), _layer_spec((inner, D), j)]
        args += [hn, xc, z, gain.reshape(gain.shape[0], 1, inner),
                 skip.reshape(skip.shape[0], 1, inner), w_down_m]
    return pl.pallas_call(
        body,
        out_shape=jax.ShapeDtypeStruct((T, D), F32),
        grid=(T // tm,),
        in_specs=in_specs,
        out_specs=tok(D),
        compiler_params=_cparams(1),
        name="ffn",
    )(*args)


def _ml_pre_body(x_ref, g_ref, wup_ref, wc_ref, bc_ref, bq_ref, bk_ref, bv_ref, wgt_ref, bgt_ref,
                 hist_ref, q_ref, k_ref, v_ref, xc_ref, z_ref, gates_ref, tail_ref, xbuf,
                 *, tm, inner, header, stride, tiles_per_seq):
    i = pl.program_id(0)

    n_groups = inner // MXU_DIM

    @pl.when(i % tiles_per_seq == 0)
    def _():
        for g in range(n_groups):
            xbuf[g, 0:header, :] = hist_ref[:, g * MXU_DIM:(g + 1) * MXU_DIM]

    xn = _rmsnorm(x_ref[...], g_ref[...]).astype(BF16)

    def up_proj(g):
        cols = slice(g * MXU_DIM, (g + 1) * MXU_DIM)
        zcols = slice(inner + g * MXU_DIM, inner + (g + 1) * MXU_DIM)
        xm_g = _dot(xn, wup_ref[:, cols])
        z_ref[:, cols] = _dot(xn, wup_ref[:, zcols]).astype(BF16)
        return xm_g

    def gate_part(g, qb, kb, vb):
        rows = lambda base: slice(base + g * MXU_DIM, base + (g + 1) * MXU_DIM)
        return (_dot(qb, wgt_ref[rows(0), :]) + _dot(kb, wgt_ref[rows(inner), :])
                + _dot(vb, wgt_ref[rows(2 * inner), :]))

    gacc = bgt_ref[...]
    prev_qkv = None
    xm_next = up_proj(0)
    for g in range(n_groups):
        cols = slice(g * MXU_DIM, (g + 1) * MXU_DIM)
        xm = xm_next
        if g + 1 < n_groups:
            xm_next = up_proj(g + 1)
        xbuf[g, header:header + tm, :] = xm
        acc = bc_ref[:, cols] + wc_ref[CONV_W - 1:CONV_W, cols] * xm
        for d in range(1, CONV_W):
            lo = header - d * stride
            acc = acc + wc_ref[CONV_W - 1 - d:CONV_W - d, cols] * xbuf[g, lo:lo + tm, :]
        xcb = _silu(acc).astype(BF16)
        xc_ref[:, cols] = xcb
        qb = _dot(xcb, bq_ref[g]).astype(BF16)
        kb = _dot(xcb, bk_ref[g]).astype(BF16)
        vb = _dot(xm.astype(BF16), bv_ref[g]).astype(BF16)
        q_ref[:, cols] = qb
        k_ref[:, cols] = kb
        v_ref[:, cols] = vb
        if g > 0:
            gacc = gacc + gate_part(g - 1, *prev_qkv)
        prev_qkv = (qb, kb, vb)
        last = xbuf[g, tm:tm + header, :]
        tail_ref[:, cols] = last
        xbuf[g, 0:header, :] = last
    gates_ref[...] = gacc + gate_part(n_groups - 1, *prev_qkv)


def _ml_pre(x, norm_g, layer, w_up, w_conv, b_conv, bq, bk, bv, w_gates, b_gates, hist, j,
            tm, stride, tiles_per_seq):
    T, D = x.shape
    inner = w_conv.shape[-1]
    header = hist.shape[1]
    n_tiles = T // tm
    body = functools.partial(_ml_pre_body, tm=tm, inner=inner, header=header, stride=stride,
                             tiles_per_seq=tiles_per_seq)
    tok = lambda w: pl.BlockSpec((tm, w), lambda i: (i, 0))
    return pl.pallas_call(
        body,
        out_shape=(
            jax.ShapeDtypeStruct((T, inner), BF16),
            jax.ShapeDtypeStruct((T, inner), BF16),
            jax.ShapeDtypeStruct((T, inner), BF16),
            jax.ShapeDtypeStruct((T, inner), BF16),
            jax.ShapeDtypeStruct((T, inner), BF16),
            jax.ShapeDtypeStruct((T, 2 * LANE), F32),
            jax.ShapeDtypeStruct((n_tiles, header, inner), F32),
        ),
        grid=(n_tiles,),
        in_specs=[
            tok(D),
            _layer_spec((1, D), layer),
            _layer_spec((D, 2 * inner), j),
            _layer_spec((CONV_W, inner), j),
            _layer_spec((1, inner), j),
            _const_spec(bq.shape), _const_spec(bk.shape), _const_spec(bv.shape),
            _const_spec(w_gates.shape), _const_spec(b_gates.shape),
            pl.BlockSpec((None, header, inner), lambda i: (i // tiles_per_seq, 0, 0)),
        ],
        out_specs=(tok(inner), tok(inner), tok(inner), tok(inner), tok(inner), tok(2 * LANE),
                   pl.BlockSpec((None, header, inner), lambda i: (i, 0, 0))),
        scratch_shapes=[pltpu.VMEM((inner // MXU_DIM, header + tm, MXU_DIM), F32)],
        compiler_params=_cparams(1),
        name="ml_pre",
    )(x, norm_g.reshape(norm_g.shape[0], 1, D), w_up, w_conv,
      b_conv.reshape(b_conv.shape[0], 1, inner), bq, bk, bv, w_gates, b_gates, hist)


def _log_sigmoid(x):
    return jnp.minimum(x, 0.0) - jnp.log(1.0 + jnp.exp(-jnp.abs(x)))


def _token_scan(x, op, fill):
    n = x.shape[0]
    tok = lax.broadcasted_iota(jnp.int32, x.shape, 0)
    d = 1
    while d < n:
        x = op(x, jnp.where(tok >= d, pltpu.roll(x, d, axis=0), fill))
        d *= 2
    return x


def _gate_vectors(gates, m_prev, L):
    ig = gates[:, :LANE]
    bt = _token_scan(_log_sigmoid(gates[:, LANE:]), jnp.add, 0.0)
    a = ig - bt
    m = bt + jnp.maximum(_token_scan(a, jnp.maximum, NEG), m_prev)
    dcol = bt - m
    bl, ml = bt[L - 1:L, :], m[L - 1:L, :]
    return dict(a=a, dcol=dcol, w_inter=jnp.exp(m_prev + dcol), exp_neg_m=jnp.exp(-m),
                gs=jnp.exp(bl - bt + ig - ml) * HEAD_DIM ** -0.5,
                g_inter=jnp.exp(m_prev + bl - ml), ml=ml)


def _ml_core_body(*refs, L, nb, has_state, single_chunk):
    if has_state:
        (q_ref, k_ref, v_ref, gates_ref, c0_ref, n0_ref, m0_ref, _,
         h_ref, c_ref, n_ref, m_ref) = refs
    else:
        q_ref, k_ref, v_ref, gates_ref, _, h_ref, c_ref, n_ref, m_ref = refs
    if has_state and single_chunk:
        c_prev, n_prev_ref, m_prev_ref = c0_ref, n0_ref, m0_ref
    else:
        c_prev, n_prev_ref, m_prev_ref = c_ref, n_ref, m_ref

        @pl.when(pl.program_id(1) == 0)
        def _():
            if has_state:
                c_ref[...] = c0_ref[...]
                n_ref[...] = n0_ref[...]
                m_ref[...] = m0_ref[...]
            else:
                c_ref[...] = jnp.zeros_like(c_ref)
                n_ref[...] = jnp.zeros_like(n_ref)
                m_ref[...] = jnp.zeros_like(m_ref)

    log_scale = -0.5 * math.log(HEAD_DIM)
    row = lax.broadcasted_iota(jnp.int32, (L, L), 0)
    col = lax.broadcasted_iota(jnp.int32, (L, L), 1)
    causal = col <= row
    heads = range(N_HEADS)
    seqs = range(nb)
    hcols = [slice(h * HEAD_DIM, (h + 1) * HEAD_DIM) for h in heads]
    ones_rows = jnp.ones((SUBLANE, L), BF16)
    n_prev = [n_prev_ref[s] for s in seqs]


    qc = [[_dot(q_ref[s, :, hcols[h]], c_prev[s, h].astype(BF16)) for h in heads] for s in seqs]

    gvs, a_rows = [], []
    for s in seqs:
        gv = _gate_vectors(gates_ref[s], m_prev_ref[s, 0:1, :], L)
        m_ref[s] = jnp.broadcast_to(gv['ml'], m_ref.shape[1:])
        a_shift = gv['a'] + log_scale
        if L % LANE == 0:
            a_t = a_shift.T
            a_rows.append([a_t[h:h + 1, :] for h in heads])
        else:
            a_rows.append([jnp.sum(jnp.where(row == col, a_shift[:, h:h + 1], 0.0), axis=0,
                                   keepdims=True) for h in heads])
        gvs.append(gv)

    for s in seqs:
        hcol = lambda name, h: gvs[s][name][:, h:h + 1]
        for h in heads:
            qh, kh, vh = q_ref[s, :, hcols[h]], k_ref[s, :, hcols[h]], v_ref[s, :, hcols[h]]

            g_inter = hcol('g_inter', h)
            kg = kh * hcol('gs', h).astype(BF16)
            upd = lax.dot_general(kg, vh, (((0,), (0,)), ((), ())), preferred_element_type=F32)
            c_ref[s, h] = g_inter * c_prev[s, h] + upd
            n_ref[s, h:h + 1, :] = (g_inter * n_prev[s][h:h + 1, :]
                                    + _dot(ones_rows, kg)[0:1, :])

            w_inter = hcol('w_inter', h)
            w_intra = jnp.exp(jnp.where(causal, hcol('dcol', h) + a_rows[s][h], NEG))
            sc = lax.dot_general(qh, kh, (((1,), (1,)), ((), ())),
                                 preferred_element_type=F32) * w_intra
            num = _dot(sc.astype(BF16), vh) + qc[s][h] * w_inter
            qn = jnp.sum((qh * n_prev[s][h:h + 1, :].astype(BF16)).astype(F32), axis=1,
                         keepdims=True)
            nq = jnp.sum(sc, axis=1, keepdims=True) + w_inter * qn
            den = jnp.maximum(jnp.abs(nq), hcol('exp_neg_m', h))
            mu = jnp.mean(num, axis=-1, keepdims=True)
            hc = num - mu
            var = jnp.mean(hc * hc, axis=-1, keepdims=True)
            h_ref[s, :, hcols[h]] = (hc * lax.rsqrt(var + EPS * den * den)).astype(BF16)


def _ml_core(q, k, v, gates, B, nc, L, nb, j, c_all, state=None):
    T, inner = q.shape
    has_state = state is not None
    n_ml = 2
    body = functools.partial(_ml_core_body, L=L, nb=nb, has_state=has_state, single_chunk=nc == 1)
    seq3 = lambda a: a.reshape(B, nc * L, a.shape[-1])
    tok = lambda w: pl.BlockSpec((nb, L, w), lambda b, c: (b, c, 0))
    c_block = pl.BlockSpec((None, nb, N_HEADS, HEAD_DIM, HEAD_DIM), lambda b, c: (j, b, 0, 0, 0))
    in_specs = [tok(inner), tok(inner), tok(inner), tok(2 * LANE)]
    args = [seq3(q), seq3(k), seq3(v), seq3(gates)]
    if has_state:
        in_specs += [c_block,
                     pl.BlockSpec((None, nb, N_HEADS, HEAD_DIM), lambda b, c: (j, b, 0, 0)),
                     pl.BlockSpec((nb, SUBLANE, LANE), lambda b, c: (b, 0, 0))]
        args += list(state)
    in_specs.append(pl.BlockSpec(memory_space=pl.ANY))
    if c_all is not None:
        args.append(c_all)
        aliases = {len(args) - 1: 1}
    else:
        args.append(jnp.zeros((SUBLANE, LANE), F32))
        aliases = {}
    hn, c_all, n, m = pl.pallas_call(
        body,
        out_shape=(
            jax.ShapeDtypeStruct((B, nc * L, inner), BF16),
            jax.ShapeDtypeStruct((n_ml, B, N_HEADS, HEAD_DIM, HEAD_DIM), F32),
            jax.ShapeDtypeStruct((B, N_HEADS, HEAD_DIM), F32),
            jax.ShapeDtypeStruct((B, SUBLANE, LANE), F32),
        ),
        grid=(B // nb, nc),
        in_specs=in_specs,
        out_specs=(tok(inner), c_block,
                   pl.BlockSpec((nb, N_HEADS, HEAD_DIM), lambda b, c: (b, 0, 0)),
                   pl.BlockSpec((nb, SUBLANE, LANE), lambda b, c: (b, 0, 0))),
        input_output_aliases=aliases,
        compiler_params=_cparams(2),
        name="ml_core",
    )(*args)
    return hn.reshape(T, inner), c_all, n, m


def _gelu(x):
    return 0.5 * x * (1.0 + lax.erf(x * (2.0 ** -0.5)))


def _cm_body(*refs, width, sub, emit_vn):
    if emit_vn:
        x_ref, g_ref, win_ref, bin_ref, lg_ref, wmix_ref, bmix_ref, wout_ref, o_ref, vn_ref = refs
    else:
        x_ref, g_ref, win_ref, bin_ref, lg_ref, wmix_ref, bmix_ref, wout_ref, o_ref = refs
    gd = width // CM_GROUPS
    n_sub = x_ref.shape[0] // sub

    def in_proj(s):
        x = x_ref[s * sub:(s + 1) * sub, :]
        xn = _rmsnorm(x, g_ref[...]).astype(BF16)
        v = _gelu(_dot(xn, win_ref[:, width:]) + bin_ref[:, width:])
        u = _gelu(_dot(xn, win_ref[:, :width]) + bin_ref[:, :width])
        return x, u, v

    def gate_and_out(s, x, u, v):
        rows = slice(s * sub, (s + 1) * sub)
        mu = jnp.mean(v, axis=-1, keepdims=True)
        vc = v - mu
        var = jnp.mean(vc * vc, axis=-1, keepdims=True)
        vn = vc * lax.rsqrt(var + EPS) * lg_ref[...]
        if emit_vn:
            vn_ref[rows, :] = vn
        vnb = vn.astype(BF16)
        bmix = bmix_ref[...]
        parts = []
        for g in range(CM_GROUPS):
            cols = slice(g * gd, (g + 1) * gd)
            mix = _dot(wmix_ref[g], vnb[:, cols]) + bmix[:, g:g + 1]
            parts.append((u[:, cols] * mix).astype(BF16))
        o_ref[rows, :] = x + _dot(jnp.concatenate(parts, axis=-1), wout_ref[...])

    nxt = in_proj(0)
    for s in range(n_sub):
        cur = nxt
        if s + 1 < n_sub:
            nxt = in_proj(s + 1)
        gate_and_out(s, *cur)


def _chunk_mlp(x, norm_g, layer, w_in, b_in, ln_g, wmix, bmix, w_out, j, tm, emit_vn):
    T, D = x.shape
    width = ln_g.shape[-1]
    body = functools.partial(_cm_body, width=width, sub=wmix.shape[1], emit_vn=emit_vn)
    tok = lambda w: pl.BlockSpec((tm, w), lambda i: (i, 0))
    out_shape = [jax.ShapeDtypeStruct((T, D), F32)]
    out_specs = [tok(D)]
    if emit_vn:
        out_shape.append(jax.ShapeDtypeStruct((T, width), F32))
        out_specs.append(tok(width))
    return pl.pallas_call(
        body,
        out_shape=tuple(out_shape),
        grid=(T // tm,),
        in_specs=[tok(D), _layer_spec((1, D), layer),
                  _layer_spec((D, 2 * width), j), _layer_spec((1, 2 * width), j),
                  _layer_spec((1, width), j),
                  _const_spec(wmix.shape), _const_spec(bmix.shape),
                  _layer_spec((width, D), j)],
        out_specs=tuple(out_specs),
        compiler_params=_cparams(1),
        name="chunk_mlp",
    )(x, norm_g.reshape(norm_g.shape[0], 1, D), w_in, b_in.reshape(b_in.shape[0], 1, 2 * width),
      ln_g.reshape(ln_g.shape[0], 1, width), wmix, bmix, w_out)


def _blockdiag_dense(w):
    rows = w.reshape(-1, MXU_DIM, QKV_BLOCK)
    r = lax.broadcasted_iota(jnp.int32, (MXU_DIM, MXU_DIM), 0)
    c = lax.broadcasted_iota(jnp.int32, (MXU_DIM, MXU_DIM), 1)
    spread = (lax.broadcasted_iota(jnp.int32, (QKV_BLOCK, MXU_DIM), 1) % QKV_BLOCK
              == lax.broadcasted_iota(jnp.int32, (QKV_BLOCK, MXU_DIM), 0)).astype(w.dtype)
    dense = jnp.einsum('grd,dc->grc', rows, spread, precision=lax.Precision.HIGHEST)
    return jnp.where(r // QKV_BLOCK == c // QKV_BLOCK, dense, 0).astype(BF16)


def _gate_weights(w_ig, b_ig, w_fg, b_fg):
    wpad = jnp.zeros((w_ig.shape[0], LANE - N_HEADS), w_ig.dtype)
    bpad = jnp.zeros((LANE - N_HEADS,), b_ig.dtype)
    w = jnp.concatenate([w_ig, wpad, w_fg, wpad], axis=1)
    b = jnp.concatenate([b_ig, bpad, b_fg, bpad])
    return w.astype(BF16), b.reshape(1, 2 * LANE)


def _trunk(x, grp, p):
    depth = p['norm_ff1'].shape[0]
    tm = grp['tm']
    new_n, new_m, new_conv, new_v = [], [], [], []
    c_all = None
    for i in range(depth):
        j = i // 2
        x = _ffn(x, p['norm_ff1'], p['ffn1_w_gate'], p['ffn1_w_up'], p['ffn1_w_down'], i, tm)
        if i % 2 == 0:
            q, k, v, xc, z, gates, tail = _ml_pre(
                x, p['norm_mix'], i, p['ml_w_up'], p['ml_w_conv'], p['ml_b_conv'],
                p['bq'][j], p['bk'][j], p['bv'][j], p['w_gates'][j], p['b_gates'][j],
                grp['hist'][j], j, grp['pre_tm'], grp['stride'], grp['tiles_per_seq'])
            new_conv.append(grp['conv_out'](tail))
            q, k, v, gates = grp['to_core'](q, k, v, gates)
            state = grp['state'](j)
            hn, c_all, n, m = _ml_core(q, k, v, gates, grp['B'], grp['nc'], grp['L'], grp['nb'], j,
                                       c_all, state)
            hn = grp['from_core'](hn)
            new_n.append(n)
            new_m.append(m[:, 0, :N_HEADS])
            mixer = (hn, xc, z, p['ml_hn_g'], p['ml_skip'], p['ml_w_down'], j)
        else:
            outs = _chunk_mlp(x, p['norm_mix'], i, p['cm_w_in'], p['cm_b_in'], p['cm_ln_g'],
                              grp['wmix'][j], grp['bmix'][j], p['cm_w_out'], j, grp['cm_tm'],
                              emit_vn=grp['emit_vn'])
            x = outs[0]
            new_v.append(outs[1] if grp['emit_vn'] else None)
            mixer = None
        final_g = p['norm_final'] if i == depth - 1 else None
        if mixer is None:
            x = _ffn(x, p['norm_ff2'], p['ffn2_w_gate'], p['ffn2_w_up'], p['ffn2_w_down'], i, tm,
                     final_g=final_g)
        else:
            x = _ffn(x, p['norm_ff2'], p['ffn2m_w_gate'], p['ffn2m_w_up'], p['ffn2m_w_down'], i,
                     grp['tm_mixer'], final_g=final_g, mixer=mixer, w_layer=j)
    return x, c_all, jnp.stack(new_n), jnp.stack(new_m), jnp.stack(new_conv), new_v


def kernel(x_prompt, x_sample, state_C, state_n, state_m, state_conv, norm_ff1, norm_mix, norm_ff2, norm_final, ffn1_w_gate, ffn1_w_up, ffn1_w_down, ffn2_w_gate, ffn2_w_up, ffn2_w_down, ml_w_up, ml_w_conv, ml_b_conv, ml_w_q, ml_w_k, ml_w_v, ml_w_ig, ml_b_ig, ml_w_fg, ml_b_fg, ml_hn_g, ml_skip, ml_w_down, cm_w_in, cm_b_in, cm_ln_g, cm_w_s, cm_b_s, cm_w_out):
    bp, sp, d_model = x_prompt.shape
    bs, ss, _ = x_sample.shape
    n_ml, inner = ml_w_conv.shape[0], ml_w_conv.shape[-1]
    n_cm, width = cm_ln_g.shape
    hist_rows = CONV_W - 1
    sb = min(64, bs)

    gate_wb = [_gate_weights(ml_w_ig[j], ml_b_ig[j], ml_w_fg[j], ml_b_fg[j]) for j in range(n_ml)]
    p = {
        'norm_ff1': norm_ff1, 'norm_mix': norm_mix, 'norm_ff2': norm_ff2, 'norm_final': norm_final,
        'ffn1_w_gate': ffn1_w_gate, 'ffn1_w_up': ffn1_w_up,
        'ffn1_w_down': ffn1_w_down,
        'ffn2_w_gate': ffn2_w_gate, 'ffn2_w_up': ffn2_w_up, 'ffn2_w_down': ffn2_w_down,
        'ffn2m_w_gate': ffn2_w_gate[0::2].astype(BF16), 'ffn2m_w_up': ffn2_w_up[0::2].astype(BF16),
        'ffn2m_w_down': ffn2_w_down[0::2].astype(BF16),
        'ml_w_up': ml_w_up.astype(BF16), 'ml_w_conv': ml_w_conv, 'ml_b_conv': ml_b_conv,
        'bq': [_blockdiag_dense(ml_w_q[j]) for j in range(n_ml)],
        'bk': [_blockdiag_dense(ml_w_k[j]) for j in range(n_ml)],
        'bv': [_blockdiag_dense(ml_w_v[j]) for j in range(n_ml)],
        'w_gates': [wb[0] for wb in gate_wb], 'b_gates': [wb[1] for wb in gate_wb],
        'ml_hn_g': ml_hn_g, 'ml_skip': ml_skip, 'ml_w_down': ml_w_down.astype(BF16),
        'cm_w_in': cm_w_in.astype(BF16), 'cm_b_in': cm_b_in, 'cm_ln_g': cm_ln_g,
        'cm_w_out': cm_w_out.astype(BF16),
    }

    def mix_weights(L, kron_left):
        causal = jnp.tril(jnp.ones((L, L), dtype=bool))
        ws = jnp.where(causal, cm_w_s[:, :, :L, :L], 0)
        bs_ = cm_b_s[:, :, :L]
        if kron_left:
            reps = kron_left
            eye = jnp.eye(reps, dtype=ws.dtype)
            wm = ws[:, :, None, :, None, :] * eye[None, None, :, None, :, None]
            wm = wm.reshape(n_cm, CM_GROUPS, reps * L, reps * L)
            bm = jnp.tile(bs_, (1, 1, reps))
        else:
            eye = jnp.eye(sb, dtype=ws.dtype)
            wm = ws[:, :, :, None, :, None] * eye[None, None, None, :, None, :]
            wm = wm.reshape(n_cm, CM_GROUPS, L * sb, L * sb)
            bm = jnp.repeat(bs_, sb, axis=2)
        bm = jnp.swapaxes(bm, 1, 2)
        bm = jnp.concatenate([bm, jnp.zeros(bm.shape[:2] + (LANE - CM_GROUPS,), bm.dtype)], axis=2)
        return wm.astype(BF16), bm

    tm_p = 512
    pre_tm_p = 512
    cm_tm_p = 1024
    chunk_p = min(256, sp)
    cm_chunk_p = min(128, sp)
    wmix_p, bmix_p = mix_weights(cm_chunk_p, MXU_DIM // cm_chunk_p)
    grp_p = {
        'tm': tm_p, 'tm_mixer': 512, 'pre_tm': pre_tm_p, 'cm_tm': cm_tm_p, 'emit_vn': False,
        'stride': 1,
        'tiles_per_seq': sp // pre_tm_p,
        'hist': [jnp.zeros((bp, SUBLANE, inner), F32)] * n_ml,
        'conv_out': lambda tail: tail.reshape(bp, sp // pre_tm_p, SUBLANE, inner)[:, -1, SUBLANE - hist_rows:, :],
        'to_core': lambda q, k, v, g: (q, k, v, g),
        'from_core': lambda hn: hn,
        'state': lambda j: None,
        'B': bp, 'nc': sp // chunk_p, 'L': chunk_p, 'nb': 2,
        'wmix': wmix_p, 'bmix': bmix_p,
    }
    yp, c_p, n_p, m_p, conv_p, _ = _trunk(x_prompt.reshape(bp * sp, d_model), grp_p, p)

    ts = bs * ss
    n_sg = bs // sb
    l_pad = SUBLANE

    def tile_major(a):
        steps, feat = a.shape[1], a.shape[2]
        return a.reshape(n_sg, sb, steps, feat).transpose(0, 2, 1, 3).reshape(n_sg * steps * sb, feat)

    def batch_major(a, steps):
        feat = a.shape[-1]
        return a.reshape(n_sg, steps, sb, feat).transpose(0, 2, 1, 3).reshape(bs, steps, feat)

    def to_core(q, k, v, g):
        def seq_major(a, fill):
            a = batch_major(a, ss)
            pad = jnp.broadcast_to(fill.astype(a.dtype), (bs, l_pad - ss, a.shape[-1]))
            return jnp.concatenate([a, pad], axis=1).reshape(bs * l_pad, a.shape[-1])
        zero = jnp.zeros((inner,), F32)
        lane_pad = jnp.zeros((LANE - N_HEADS,), F32)
        gfill = jnp.concatenate([jnp.full((N_HEADS,), NEG, F32), lane_pad,
                                 jnp.full((N_HEADS,), -NEG, F32), lane_pad])
        return seq_major(q, zero), seq_major(k, zero), seq_major(v, zero), seq_major(g, gfill)

    def from_core(hn):
        return tile_major(hn.reshape(bs, l_pad, inner)[:, :ss])

    def sample_state(j):
        m0 = jnp.pad(state_m[j], ((0, 0), (0, LANE - N_HEADS)))
        return state_C, state_n, jnp.broadcast_to(m0[:, None, :], (bs, SUBLANE, LANE))

    wmix_s, bmix_s = mix_weights(ss, 0)
    tile_s = ss * sb
    hist_s = [tile_major(state_conv[j]).reshape(n_sg, hist_rows * sb, inner) for j in range(n_ml)]
    grp_s = {
        'tm': tile_s, 'tm_mixer': tile_s, 'pre_tm': tile_s, 'cm_tm': tile_s, 'emit_vn': True,
        'stride': sb,
        'tiles_per_seq': 1,
        'hist': hist_s,
        'conv_out': lambda tail: batch_major(tail.reshape(n_sg * hist_rows * sb, inner), hist_rows),
        'to_core': to_core, 'from_core': from_core, 'state': sample_state,
        'B': bs, 'nc': 1, 'L': l_pad, 'nb': 2,
        'wmix': wmix_s, 'bmix': bmix_s,
    }
    ys, c_s, n_s, m_s, conv_s, v_rows = _trunk(tile_major(x_sample), grp_s, p)

    y_prompt = yp.reshape(bp, sp, d_model)
    y_sample = batch_major(ys, ss)
    v_sample = jnp.stack([batch_major(vr, ss) for vr in v_rows])
    return (y_prompt, y_sample, c_p, n_p, m_p, conv_p, c_s, n_s, m_s, conv_s, v_sample)
```

```python
import functools
import math

import jax
import jax.numpy as jnp
from jax import lax
from jax.experimental import pallas as pl
from jax.experimental.pallas import tpu as pltpu

F32 = jnp.float32
BF16 = jnp.bfloat16

EPS = 1e-6
N_HEADS = 4
HEAD_DIM = 512
QKV_BLOCK = 4
CONV_W = 4
CM_GROUPS = 4
NEG = -1e30

LANE = 128
SUBLANE = 8
MXU_DIM = 256
VMEM_LIMIT = 56 * 1024 * 1024
VMEM_LIMIT_CORE = 60 * 1024 * 1024


def _cparams(n_axes, vmem_limit=VMEM_LIMIT):
    return pltpu.CompilerParams(dimension_semantics=("arbitrary",) * n_axes,
                                vmem_limit_bytes=vmem_limit)


def _const_spec(shape):
    nd = len(shape)
    return pl.BlockSpec(shape, lambda *_: (0,) * nd, pipeline_mode=pl.Buffered(1))


def _layer_spec(shape, layer):
    nd = len(shape)
    return pl.BlockSpec((None,) + tuple(shape), lambda *_: (layer,) + (0,) * nd,
                        pipeline_mode=pl.Buffered(1))


def _rmsnorm(x, g):
    return x * lax.rsqrt(jnp.mean(x * x, axis=-1, keepdims=True) + EPS) * g


def _silu(x):
    return x * jax.nn.sigmoid(x)


def _dot(a, b):
    return jnp.dot(a, b, preferred_element_type=F32)


def _ffn_body(*refs, ff_chunks, final_norm, mixer_out):
    if mixer_out:
        (x_ref, g_ref, wg_ref, wu_ref, wd_ref, gf_ref,
         hn_ref, xc_ref, z_ref, hg_ref, sk_ref, wdm_ref, o_ref) = refs
    else:
        x_ref, g_ref, wg_ref, wu_ref, wd_ref, gf_ref, o_ref = refs
    x = x_ref[...]
    if mixer_out:
        for lo in range(0, hn_ref.shape[1], HEAD_DIM):
            cols = slice(lo, lo + HEAD_DIM)
            out = ((hn_ref[:, cols].astype(F32) * hg_ref[:, cols]
                    + sk_ref[:, cols] * xc_ref[:, cols].astype(F32))
                   * _silu(z_ref[:, cols].astype(F32)))
            x = x + _dot(out.astype(BF16), wdm_ref[cols, :])
    xn = _rmsnorm(x, g_ref[...]).astype(wg_ref.dtype)
    acc = None
    for lo, hi in ff_chunks:
        gate = _dot(xn, wg_ref[:, lo:hi])
        up = _dot(xn, wu_ref[:, lo:hi])
        h = (_silu(gate) * up).astype(wd_ref.dtype)
        part = _dot(h, wd_ref[lo:hi, :])
        acc = part if acc is None else acc + part
    y = x + 0.5 * acc
    if final_norm:
        y = _rmsnorm(y, gf_ref[...])
    o_ref[...] = y


def _ffn(x, norm_g, w_gate, w_up, w_down, layer, tm, final_g=None, mixer=None):
    T, D = x.shape
    d_ff = w_gate.shape[-1]
    step = 4 * MXU_DIM
    ff_chunks = tuple((lo, min(lo + step, d_ff)) for lo in range(0, d_ff, step))
    final_norm = final_g is not None
    gf = final_g if final_norm else norm_g[layer]
    body = functools.partial(_ffn_body, ff_chunks=ff_chunks, final_norm=final_norm,
                             mixer_out=mixer is not None)
    tok = lambda w: pl.BlockSpec((tm, w), lambda i: (i, 0))
    in_specs = [tok(D), _layer_spec((1, D), layer), _layer_spec((D, d_ff), layer),
                _layer_spec((D, d_ff), layer), _layer_spec((d_ff, D), layer), _const_spec((1, D))]
    args = [x, norm_g.reshape(norm_g.shape[0], 1, D), w_gate, w_up, w_down, gf.reshape(1, D)]
    if mixer is not None:
        hn, xc, z, gain, skip, w_down_m, j = mixer
        inner = hn.shape[-1]
        in_specs += [tok(inner), tok(inner), tok(inner), _layer_spec((1, inner), j),
                     _layer_spec((1, inner), j), _layer_spec((inner, D), j)]
        args += [hn, xc, z, gain.reshape(gain.shape[0], 1, inner),
                 skip.reshape(skip.shape[0], 1, inner), w_down_m]
    return pl.pallas_call(
        body,
        out_shape=jax.ShapeDtypeStruct((T, D), F32),
        grid=(T // tm,),
        in_specs=in_specs,
        out_specs=tok(D),
        compiler_params=_cparams(1),
        name="ffn",
    )(*args)


def _ml_pre_body(x_ref, g_ref, wup_ref, wc_ref, bc_ref, bq_ref, bk_ref, bv_ref, wgt_ref, bgt_ref,
                 hist_ref, q_ref, k_ref, v_ref, xc_ref, z_ref, gates_ref, tail_ref, xbuf,
                 *, tm, inner, header, stride, tiles_per_seq):
    i = pl.program_id(0)

    n_groups = inner // MXU_DIM

    @pl.when(i % tiles_per_seq == 0)
    def _():
        for g in range(n_groups):
            xbuf[g, 0:header, :] = hist_ref[:, g * MXU_DIM:(g + 1) * MXU_DIM]

    xn = _rmsnorm(x_ref[...], g_ref[...]).astype(BF16)

    def up_proj(g):
        cols = slice(g * MXU_DIM, (g + 1) * MXU_DIM)
        zcols = slice(inner + g * MXU_DIM, inner + (g + 1) * MXU_DIM)
        xm_g = _dot(xn, wup_ref[:, cols])
        z_ref[:, cols] = _dot(xn, wup_ref[:, zcols]).astype(BF16)
        return xm_g

    def gate_part(g, qb, kb, vb):
        rows = lambda base: slice(base + g * MXU_DIM, base + (g + 1) * MXU_DIM)
        return (_dot(qb, wgt_ref[rows(0), :]) + _dot(kb, wgt_ref[rows(inner), :])
                + _dot(vb, wgt_ref[rows(2 * inner), :]))

    gacc = bgt_ref[...]
    prev_qkv = None
    xm_next = up_proj(0)
    for g in range(n_groups):
        cols = slice(g * MXU_DIM, (g + 1) * MXU_DIM)
        xm = xm_next
        if g + 1 < n_groups:
            xm_next = up_proj(g + 1)
        xbuf[g, header:header + tm, :] = xm
        acc = bc_ref[:, cols] + wc_ref[CONV_W - 1:CONV_W, cols] * xm
        for d in range(1, CONV_W):
            lo = header - d * stride
            acc = acc + wc_ref[CONV_W - 1 - d:CONV_W - d, cols] * xbuf[g, lo:lo + tm, :]
        xcb = _silu(acc).astype(BF16)
        xc_ref[:, cols] = xcb
        qb = _dot(xcb, bq_ref[g]).astype(BF16)
        kb = _dot(xcb, bk_ref[g]).astype(BF16)
        vb = _dot(xm.astype(BF16), bv_ref[g]).astype(BF16)
        q_ref[:, cols] = qb
        k_ref[:, cols] = kb
        v_ref[:, cols] = vb
        if g > 0:
            gacc = gacc + gate_part(g - 1, *prev_qkv)
        prev_qkv = (qb, kb, vb)
        last = xbuf[g, tm:tm + header, :]
        tail_ref[:, cols] = last
        xbuf[g, 0:header, :] = last
    gates_ref[...] = gacc + gate_part(n_groups - 1, *prev_qkv)


def _ml_pre(x, norm_g, layer, w_up, w_conv, b_conv, bq, bk, bv, w_gates, b_gates, hist, j,
            tm, stride, tiles_per_seq):
    T, D = x.shape
    inner = w_conv.shape[-1]
    header = hist.shape[1]
    n_tiles = T // tm
    body = functools.partial(_ml_pre_body, tm=tm, inner=inner, header=header, stride=stride,
                             tiles_per_seq=tiles_per_seq)
    tok = lambda w: pl.BlockSpec((tm, w), lambda i: (i, 0))
    return pl.pallas_call(
        body,
        out_shape=(
            jax.ShapeDtypeStruct((T, inner), BF16),
            jax.ShapeDtypeStruct((T, inner), BF16),
            jax.ShapeDtypeStruct((T, inner), BF16),
            jax.ShapeDtypeStruct((T, inner), BF16),
            jax.ShapeDtypeStruct((T, inner), BF16),
            jax.ShapeDtypeStruct((T, 2 * LANE), F32),
            jax.ShapeDtypeStruct((n_tiles, header, inner), F32),
        ),
        grid=(n_tiles,),
        in_specs=[
            tok(D),
            _layer_spec((1, D), layer),
            _layer_spec((D, 2 * inner), j),
            _layer_spec((CONV_W, inner), j),
            _layer_spec((1, inner), j),
            _const_spec(bq.shape), _const_spec(bk.shape), _const_spec(bv.shape),
            _const_spec(w_gates.shape), _const_spec(b_gates.shape),
            pl.BlockSpec((None, header, inner), lambda i: (i // tiles_per_seq, 0, 0)),
        ],
        out_specs=(tok(inner), tok(inner), tok(inner), tok(inner), tok(inner), tok(2 * LANE),
                   pl.BlockSpec((None, header, inner), lambda i: (i, 0, 0))),
        scratch_shapes=[pltpu.VMEM((inner // MXU_DIM, header + tm, MXU_DIM), F32)],
        compiler_params=_cparams(1),
        name="ml_pre",
    )(x, norm_g.reshape(norm_g.shape[0], 1, D), w_up, w_conv,
      b_conv.reshape(b_conv.shape[0], 1, inner), bq, bk, bv, w_gates, b_gates, hist)


def _log_sigmoid(x):
    return jnp.minimum(x, 0.0) - jnp.log(1.0 + jnp.exp(-jnp.abs(x)))


def _token_scan(x, op, fill):
    n = x.shape[0]
    tok = lax.broadcasted_iota(jnp.int32, x.shape, 0)
    d = 1
    while d < n:
        x = op(x, jnp.where(tok >= d, pltpu.roll(x, d, axis=0), fill))
        d *= 2
    return x


def _gate_vectors(gates, m_prev, L):
    ig = gates[:, :LANE]
    bt = _token_scan(_log_sigmoid(gates[:, LANE:]), jnp.add, 0.0)
    a = ig - bt
    m = bt + jnp.maximum(_token_scan(a, jnp.maximum, NEG), m_prev)
    dcol = bt - m
    bl, ml = bt[L - 1:L, :], m[L - 1:L, :]
    return dict(a=a, dcol=dcol, w_inter=jnp.exp(m_prev + dcol), exp_neg_m=jnp.exp(-m),
                gs=jnp.exp(bl - bt + ig - ml) * HEAD_DIM ** -0.5,
                g_inter=jnp.exp(m_prev + bl - ml), ml=ml)


def _ml_core_body(*refs, cfgs):
    n_in = [8 if has_state else 5 for (_, _, _, has_state) in cfgs]
    pos_out = sum(n_in)
    pos_in = 0
    for (L, nb, nc, has_state), k_in in zip(cfgs, n_in):
        _ml_core_group(refs[pos_in:pos_in + k_in], refs[pos_out:pos_out + 4], L, nb, nc, has_state)
        pos_in += k_in
        pos_out += 4


def _ml_core_group(in_refs, out_refs, L, nb, nc, has_state):
    h_ref, c_ref, n_ref, m_ref = out_refs
    if has_state:
        q_ref, k_ref, v_ref, gates_ref, c0_ref, n0_ref, m0_ref, _ = in_refs
    else:
        q_ref, k_ref, v_ref, gates_ref, _ = in_refs
    if has_state and nc == 1:
        c_prev, n_prev_ref, m_prev_ref = c0_ref, n0_ref, m0_ref
    else:
        c_prev, n_prev_ref, m_prev_ref = c_ref, n_ref, m_ref

        @pl.when(pl.program_id(0) % nc == 0)
        def _():
            if has_state:
                c_ref[...] = c0_ref[...]
                n_ref[...] = n0_ref[...]
                m_ref[...] = m0_ref[...]
            else:
                c_ref[...] = jnp.zeros_like(c_ref)
                n_ref[...] = jnp.zeros_like(n_ref)
                m_ref[...] = jnp.zeros_like(m_ref)

    log_scale = -0.5 * math.log(HEAD_DIM)
    row = lax.broadcasted_iota(jnp.int32, (L, L), 0)
    col = lax.broadcasted_iota(jnp.int32, (L, L), 1)
    causal = col <= row
    heads = range(N_HEADS)
    seqs = range(nb)
    hcols = [slice(h * HEAD_DIM, (h + 1) * HEAD_DIM) for h in heads]
    ones_rows = jnp.ones((SUBLANE, L), BF16)
    n_prev = [n_prev_ref[s] for s in seqs]


    qc = [[_dot(q_ref[s, :, hcols[h]], c_prev[s, h].astype(BF16)) for h in heads] for s in seqs]

    gvs, a_rows = [], []
    for s in seqs:
        gv = _gate_vectors(gates_ref[s], m_prev_ref[s, 0:1, :], L)
        m_ref[s] = jnp.broadcast_to(gv['ml'], m_ref.shape[1:])
        a_shift = gv['a'] + log_scale
        if L % LANE == 0:
            a_t = a_shift.T
            a_rows.append([a_t[h:h + 1, :] for h in heads])
        else:
            a_rows.append([jnp.sum(jnp.where(row == col, a_shift[:, h:h + 1], 0.0), axis=0,
                                   keepdims=True) for h in heads])
        gvs.append(gv)

    for s in seqs:
        hcol = lambda name, h: gvs[s][name][:, h:h + 1]
        for h in heads:
            qh, kh, vh = q_ref[s, :, hcols[h]], k_ref[s, :, hcols[h]], v_ref[s, :, hcols[h]]

            g_inter = hcol('g_inter', h)
            kg = kh * hcol('gs', h).astype(BF16)
            upd = lax.dot_general(kg, vh, (((0,), (0,)), ((), ())), preferred_element_type=F32)
            c_ref[s, h] = g_inter * c_prev[s, h] + upd
            n_ref[s, h:h + 1, :] = (g_inter * n_prev[s][h:h + 1, :]
                                    + _dot(ones_rows, kg)[0:1, :])

            w_inter = hcol('w_inter', h)
            w_intra = jnp.exp(jnp.where(causal, hcol('dcol', h) + a_rows[s][h], NEG))
            sc = lax.dot_general(qh, kh, (((1,), (1,)), ((), ())),
                                 preferred_element_type=F32) * w_intra
            num = _dot(sc.astype(BF16), vh) + qc[s][h] * w_inter
            qn = jnp.sum((qh * n_prev[s][h:h + 1, :].astype(BF16)).astype(F32), axis=1,
                         keepdims=True)
            nq = jnp.sum(sc, axis=1, keepdims=True) + w_inter * qn
            den = jnp.maximum(jnp.abs(nq), hcol('exp_neg_m', h))
            mu = jnp.mean(num, axis=-1, keepdims=True)
            hc = num - mu
            var = jnp.mean(hc * hc, axis=-1, keepdims=True)
            h_ref[s, :, hcols[h]] = (hc * lax.rsqrt(var + EPS * den * den)).astype(BF16)


def _ml_core(groups, j):
    n_ml = 2
    steps = {g['B'] // g['nb'] * g['nc'] for g in groups}
    assert len(steps) == 1, "token groups must have the same number of grid steps"
    in_specs, args, out_specs, out_shape, aliases, cfgs = [], [], [], [], {}, []
    for gi, g in enumerate(groups):
        B, nc, L, nb = g['B'], g['nc'], g['L'], g['nb']
        inner = g['q'].shape[-1]
        has_state = g['state'] is not None
        cfgs.append((L, nb, nc, has_state))
        seq3 = lambda a, B=B, nc=nc, L=L: a.reshape(B, nc * L, a.shape[-1])
        tok = lambda w, nb=nb, L=L, nc=nc: pl.BlockSpec((nb, L, w), lambda t: (t // nc, t % nc, 0))
        per_seq = lambda shape, nb=nb, nc=nc: pl.BlockSpec(
            (nb,) + shape, lambda t: (t // nc,) + (0,) * len(shape))
        per_seq_layer = lambda shape, nb=nb, nc=nc: pl.BlockSpec(
            (None, nb) + shape, lambda t: (j, t // nc) + (0,) * len(shape))
        c_block = per_seq_layer((N_HEADS, HEAD_DIM, HEAD_DIM))
        in_specs += [tok(inner), tok(inner), tok(inner), tok(2 * LANE)]
        args += [seq3(g['q']), seq3(g['k']), seq3(g['v']), seq3(g['gates'])]
        if has_state:
            in_specs += [c_block, per_seq_layer((N_HEADS, HEAD_DIM)), per_seq((SUBLANE, LANE))]
            args += list(g['state'])
        in_specs.append(pl.BlockSpec(memory_space=pl.ANY))
        if g['c_all'] is not None:
            args.append(g['c_all'])
            aliases[len(args) - 1] = 4 * gi + 1
        else:
            args.append(jnp.zeros((SUBLANE, LANE), F32))
        out_shape += [jax.ShapeDtypeStruct((B, nc * L, inner), BF16),
                      jax.ShapeDtypeStruct((n_ml, B, N_HEADS, HEAD_DIM, HEAD_DIM), F32),
                      jax.ShapeDtypeStruct((B, N_HEADS, HEAD_DIM), F32),
                      jax.ShapeDtypeStruct((B, SUBLANE, LANE), F32)]
        out_specs += [tok(inner), c_block, per_seq((N_HEADS, HEAD_DIM)), per_seq((SUBLANE, LANE))]
    outs = pl.pallas_call(
        functools.partial(_ml_core_body, cfgs=tuple(cfgs)),
        out_shape=tuple(out_shape),
        grid=(steps.pop(),),
        in_specs=in_specs,
        out_specs=tuple(out_specs),
        input_output_aliases=aliases,
        compiler_params=_cparams(1, VMEM_LIMIT_CORE),
        name="ml_core",
    )(*args)
    results = []
    for gi, g in enumerate(groups):
        hn, c_all, n, m = outs[4 * gi:4 * gi + 4]
        results.append((hn.reshape(g['q'].shape), c_all, n, m))
    return results


def _gelu(x):
    return 0.5 * x * (1.0 + lax.erf(x * (2.0 ** -0.5)))


def _cm_body(*refs, width, sub, emit_vn):
    if emit_vn:
        x_ref, g_ref, win_ref, bin_ref, lg_ref, wmix_ref, bmix_ref, wout_ref, o_ref, vn_ref = refs
    else:
        x_ref, g_ref, win_ref, bin_ref, lg_ref, wmix_ref, bmix_ref, wout_ref, o_ref = refs
    gd = width // CM_GROUPS
    n_sub = x_ref.shape[0] // sub

    def in_proj(s):
        x = x_ref[s * sub:(s + 1) * sub, :]
        xn = _rmsnorm(x, g_ref[...]).astype(BF16)
        v = _gelu(_dot(xn, win_ref[:, width:]) + bin_ref[:, width:])
        u = _gelu(_dot(xn, win_ref[:, :width]) + bin_ref[:, :width])
        return x, u, v

    def gate_and_out(s, x, u, v):
        rows = slice(s * sub, (s + 1) * sub)
        mu = jnp.mean(v, axis=-1, keepdims=True)
        vc = v - mu
        var = jnp.mean(vc * vc, axis=-1, keepdims=True)
        vn = vc * lax.rsqrt(var + EPS) * lg_ref[...]
        if emit_vn:
            vn_ref[rows, :] = vn
        vnb = vn.astype(BF16)
        bmix = bmix_ref[...]
        parts = []
        for g in range(CM_GROUPS):
            cols = slice(g * gd, (g + 1) * gd)
            mix = _dot(wmix_ref[g], vnb[:, cols]) + bmix[:, g:g + 1]
            parts.append((u[:, cols] * mix).astype(BF16))
        o_ref[rows, :] = x + _dot(jnp.concatenate(parts, axis=-1), wout_ref[...])

    nxt = in_proj(0)
    for s in range(n_sub):
        cur = nxt
        if s + 1 < n_sub:
            nxt = in_proj(s + 1)
        gate_and_out(s, *cur)


def _chunk_mlp(x, norm_g, layer, w_in, b_in, ln_g, wmix, bmix, w_out, j, tm, emit_vn):
    T, D = x.shape
    width = ln_g.shape[-1]
    body = functools.partial(_cm_body, width=width, sub=wmix.shape[1], emit_vn=emit_vn)
    tok = lambda w: pl.BlockSpec((tm, w), lambda i: (i, 0))
    out_shape = [jax.ShapeDtypeStruct((T, D), F32)]
    out_specs = [tok(D)]
    if emit_vn:
        out_shape.append(jax.ShapeDtypeStruct((T, width), F32))
        out_specs.append(tok(width))
    return pl.pallas_call(
        body,
        out_shape=tuple(out_shape),
        grid=(T // tm,),
        in_specs=[tok(D), _layer_spec((1, D), layer),
                  _layer_spec((D, 2 * width), j), _layer_spec((1, 2 * width), j),
                  _layer_spec((1, width), j),
                  _const_spec(wmix.shape), _const_spec(bmix.shape),
                  _layer_spec((width, D), j)],
        out_specs=tuple(out_specs),
        compiler_params=_cparams(1),
        name="chunk_mlp",
    )(x, norm_g.reshape(norm_g.shape[0], 1, D), w_in, b_in.reshape(b_in.shape[0], 1, 2 * width),
      ln_g.reshape(ln_g.shape[0], 1, width), wmix, bmix, w_out)


def _blockdiag_dense(w):
    rows = w.reshape(-1, MXU_DIM, QKV_BLOCK)
    r = lax.broadcasted_iota(jnp.int32, (MXU_DIM, MXU_DIM), 0)
    c = lax.broadcasted_iota(jnp.int32, (MXU_DIM, MXU_DIM), 1)
    spread = (lax.broadcasted_iota(jnp.int32, (QKV_BLOCK, MXU_DIM), 1) % QKV_BLOCK
              == lax.broadcasted_iota(jnp.int32, (QKV_BLOCK, MXU_DIM), 0)).astype(w.dtype)
    dense = jnp.einsum('grd,dc->grc', rows, spread, precision=lax.Precision.HIGHEST)
    return jnp.where(r // QKV_BLOCK == c // QKV_BLOCK, dense, 0).astype(BF16)


def _gate_weights(w_ig, b_ig, w_fg, b_fg):
    wpad = jnp.zeros((w_ig.shape[0], LANE - N_HEADS), w_ig.dtype)
    bpad = jnp.zeros((LANE - N_HEADS,), b_ig.dtype)
    w = jnp.concatenate([w_ig, wpad, w_fg, wpad], axis=1)
    b = jnp.concatenate([b_ig, bpad, b_fg, bpad])
    return w.astype(BF16), b.reshape(1, 2 * LANE)


def _trunks(xs, grps, p):
    depth = p['norm_ff1'].shape[0]
    sts = [dict(x=x, c_all=None, n=[], m=[], conv=[], v=[]) for x in xs]
    for i in range(depth):
        j = i // 2
        for grp, st in zip(grps, sts):
            st['x'] = _ffn(st['x'], p['norm_ff1'], p['ffn1_w_gate'], p['ffn1_w_up'],
                           p['ffn1_w_down'], i, grp['tm'])
        if i % 2 == 0:
            core_in = []
            for grp, st in zip(grps, sts):
                q, k, v, xc, z, gates, tail = _ml_pre(
                    st['x'], p['norm_mix'], i, p['ml_w_up'], p['ml_w_conv'], p['ml_b_conv'],
                    p['bq'][j], p['bk'][j], p['bv'][j], p['w_gates'][j], p['b_gates'][j],
                    grp['hist'][j], j, grp['pre_tm'], grp['stride'], grp['tiles_per_seq'])
                st['conv'].append(grp['conv_out'](tail))
                st['xc'], st['z'] = xc, z
                q, k, v, gates = grp['to_core'](q, k, v, gates)
                core_in.append(dict(q=q, k=k, v=v, gates=gates, B=grp['B'], nc=grp['nc'], L=grp['L'],
                                    nb=grp['nb'], state=grp['state'](j), c_all=st['c_all']))
            for grp, st, (hn, c_all, n, m) in zip(grps, sts, _ml_core(core_in, j)):
                st['c_all'] = c_all
                st['n'].append(n)
                st['m'].append(m[:, 0, :N_HEADS])
                st['mixer'] = (grp['from_core'](hn), st['xc'], st['z'], p['ml_hn_g'], p['ml_skip'],
                               p['ml_w_down'], j)
        else:
            for grp, st in zip(grps, sts):
                outs = _chunk_mlp(st['x'], p['norm_mix'], i, p['cm_w_in'], p['cm_b_in'], p['cm_ln_g'],
                                  grp['wmix'][j], grp['bmix'][j], p['cm_w_out'], j, grp['cm_tm'],
                                  emit_vn=grp['emit_vn'])
                st['x'] = outs[0]
                st['v'].append(outs[1] if grp['emit_vn'] else None)
                st['mixer'] = None
        final_g = p['norm_final'] if i == depth - 1 else None
        for grp, st in zip(grps, sts):
            mixer = st['mixer']
            tm2 = grp['tm'] if mixer is None else grp['tm_mixer']
            st['x'] = _ffn(st['x'], p['norm_ff2'], p['ffn2_w_gate'], p['ffn2_w_up'],
                           p['ffn2_w_down'], i, tm2, final_g=final_g, mixer=mixer)
    return [(st['x'], st['c_all'], jnp.stack(st['n']), jnp.stack(st['m']), jnp.stack(st['conv']),
             st['v']) for st in sts]


def kernel(x_prompt, x_sample, state_C, state_n, state_m, state_conv, norm_ff1, norm_mix, norm_ff2, norm_final, ffn1_w_gate, ffn1_w_up, ffn1_w_down, ffn2_w_gate, ffn2_w_up, ffn2_w_down, ml_w_up, ml_w_conv, ml_b_conv, ml_w_q, ml_w_k, ml_w_v, ml_w_ig, ml_b_ig, ml_w_fg, ml_b_fg, ml_hn_g, ml_skip, ml_w_down, cm_w_in, cm_b_in, cm_ln_g, cm_w_s, cm_b_s, cm_w_out):
    bp, sp, d_model = x_prompt.shape
    bs, ss, _ = x_sample.shape
    n_ml, inner = ml_w_conv.shape[0], ml_w_conv.shape[-1]
    n_cm, width = cm_ln_g.shape
    hist_rows = CONV_W - 1
    sb = min(64, bs)

    gate_wb = [_gate_weights(ml_w_ig[j], ml_b_ig[j], ml_w_fg[j], ml_b_fg[j]) for j in range(n_ml)]
    p = {
        'norm_ff1': norm_ff1, 'norm_mix': norm_mix, 'norm_ff2': norm_ff2, 'norm_final': norm_final,
        'ffn1_w_gate': ffn1_w_gate.astype(BF16), 'ffn1_w_up': ffn1_w_up.astype(BF16),
        'ffn1_w_down': ffn1_w_down.astype(BF16),
        'ffn2_w_gate': ffn2_w_gate.astype(BF16), 'ffn2_w_up': ffn2_w_up.astype(BF16),
        'ffn2_w_down': ffn2_w_down.astype(BF16),
        'ml_w_up': ml_w_up.astype(BF16), 'ml_w_conv': ml_w_conv, 'ml_b_conv': ml_b_conv,
        'bq': [_blockdiag_dense(ml_w_q[j]) for j in range(n_ml)],
        'bk': [_blockdiag_dense(ml_w_k[j]) for j in range(n_ml)],
        'bv': [_blockdiag_dense(ml_w_v[j]) for j in range(n_ml)],
        'w_gates': [wb[0] for wb in gate_wb], 'b_gates': [wb[1] for wb in gate_wb],
        'ml_hn_g': ml_hn_g, 'ml_skip': ml_skip, 'ml_w_down': ml_w_down.astype(BF16),
        'cm_w_in': cm_w_in.astype(BF16), 'cm_b_in': cm_b_in, 'cm_ln_g': cm_ln_g,
        'cm_w_out': cm_w_out.astype(BF16),
    }

    def mix_weights(L, kron_left):
        causal = jnp.tril(jnp.ones((L, L), dtype=bool))
        ws = jnp.where(causal, cm_w_s[:, :, :L, :L], 0)
        bs_ = cm_b_s[:, :, :L]
        if kron_left:
            reps = kron_left
            eye = jnp.eye(reps, dtype=ws.dtype)
            wm = ws[:, :, None, :, None, :] * eye[None, None, :, None, :, None]
            wm = wm.reshape(n_cm, CM_GROUPS, reps * L, reps * L)
            bm = jnp.tile(bs_, (1, 1, reps))
        else:
            eye = jnp.eye(sb, dtype=ws.dtype)
            wm = ws[:, :, :, None, :, None] * eye[None, None, None, :, None, :]
            wm = wm.reshape(n_cm, CM_GROUPS, L * sb, L * sb)
            bm = jnp.repeat(bs_, sb, axis=2)
        bm = jnp.swapaxes(bm, 1, 2)
        bm = jnp.concatenate([bm, jnp.zeros(bm.shape[:2] + (LANE - CM_GROUPS,), bm.dtype)], axis=2)
        return wm.astype(BF16), bm

    tm_p = 1024
    pre_tm_p = 512
    cm_tm_p = 1024
    chunk_p = min(256, sp)
    cm_chunk_p = min(128, sp)
    wmix_p, bmix_p = mix_weights(cm_chunk_p, MXU_DIM // cm_chunk_p)
    grp_p = {
        'tm': tm_p, 'tm_mixer': 512, 'pre_tm': pre_tm_p, 'cm_tm': cm_tm_p, 'emit_vn': False,
        'stride': 1,
        'tiles_per_seq': sp // pre_tm_p,
        'hist': [jnp.zeros((bp, SUBLANE, inner), F32)] * n_ml,
        'conv_out': lambda tail: tail.reshape(bp, sp // pre_tm_p, SUBLANE, inner)[:, -1, SUBLANE - hist_rows:, :],
        'to_core': lambda q, k, v, g: (q, k, v, g),
        'from_core': lambda hn: hn,
        'state': lambda j: None,
        'B': bp, 'nc': sp // chunk_p, 'L': chunk_p, 'nb': 1,
        'wmix': wmix_p, 'bmix': bmix_p,
    }

    ts = bs * ss
    n_sg = bs // sb
    l_pad = SUBLANE

    def tile_major(a):
        steps, feat = a.shape[1], a.shape[2]
        return a.reshape(n_sg, sb, steps, feat).transpose(0, 2, 1, 3).reshape(n_sg * steps * sb, feat)

    def batch_major(a, steps):
        feat = a.shape[-1]
        return a.reshape(n_sg, steps, sb, feat).transpose(0, 2, 1, 3).reshape(bs, steps, feat)

    def to_core(q, k, v, g):
        def seq_major(a, fill):
            a = batch_major(a, ss)
            pad = jnp.broadcast_to(fill.astype(a.dtype), (bs, l_pad - ss, a.shape[-1]))
            return jnp.concatenate([a, pad], axis=1).reshape(bs * l_pad, a.shape[-1])
        zero = jnp.zeros((inner,), F32)
        lane_pad = jnp.zeros((LANE - N_HEADS,), F32)
        gfill = jnp.concatenate([jnp.full((N_HEADS,), NEG, F32), lane_pad,
                                 jnp.full((N_HEADS,), -NEG, F32), lane_pad])
        return seq_major(q, zero), seq_major(k, zero), seq_major(v, zero), seq_major(g, gfill)

    def from_core(hn):
        return tile_major(hn.reshape(bs, l_pad, inner)[:, :ss])

    def sample_state(j):
        m0 = jnp.pad(state_m[j], ((0, 0), (0, LANE - N_HEADS)))
        return state_C, state_n, jnp.broadcast_to(m0[:, None, :], (bs, SUBLANE, LANE))

    wmix_s, bmix_s = mix_weights(ss, 0)
    tile_s = ss * sb
    hist_s = [tile_major(state_conv[j]).reshape(n_sg, hist_rows * sb, inner) for j in range(n_ml)]
    grp_s = {
        'tm': tile_s, 'tm_mixer': tile_s, 'pre_tm': tile_s, 'cm_tm': tile_s, 'emit_vn': True,
        'stride': sb,
        'tiles_per_seq': 1,
        'hist': hist_s,
        'conv_out': lambda tail: batch_major(tail.reshape(n_sg * hist_rows * sb, inner), hist_rows),
        'to_core': to_core, 'from_core': from_core, 'state': sample_state,
        'B': bs, 'nc': 1, 'L': l_pad, 'nb': 2,
        'wmix': wmix_s, 'bmix': bmix_s,
    }
    ((yp, c_p, n_p, m_p, conv_p, _), (ys, c_s, n_s, m_s, conv_s, v_rows)) = _trunks(
        [x_prompt.reshape(bp * sp, d_model), tile_major(x_sample)], [grp_p, grp_s], p)

    y_prompt = yp.reshape(bp, sp, d_model)
    y_sample = batch_major(ys, ss)
    v_sample = jnp.stack([batch_major(vr, ss) for vr in v_rows])
    return (y_prompt, y_sample, c_p, n_p, m_p, conv_p, c_s, n_s, m_s, conv_s, v_sample)
```

```python
import functools
import math

import jax
import jax.numpy as jnp
from jax import lax
from jax.experimental import pallas as pl
from jax.experimental.pallas import tpu as pltpu

F32 = jnp.float32
BF16 = jnp.bfloat16

EPS = 1e-6
N_HEADS = 4
HEAD_DIM = 512
QKV_BLOCK = 4
CONV_W = 4
CM_GROUPS = 4
NEG = -1e30

LANE = 128
SUBLANE = 8
MXU_DIM = 256
VMEM_LIMIT = 56 * 1024 * 1024
VMEM_LIMIT_CORE = 60 * 1024 * 1024


def _cparams(n_axes, vmem_limit=VMEM_LIMIT):
    return pltpu.CompilerParams(dimension_semantics=("arbitrary",) * n_axes,
                                vmem_limit_bytes=vmem_limit)


def _const_spec(shape):
    nd = len(shape)
    return pl.BlockSpec(shape, lambda *_: (0,) * nd, pipeline_mode=pl.Buffered(1))


def _layer_spec(shape, layer):
    nd = len(shape)
    return pl.BlockSpec((None,) + tuple(shape), lambda *_: (layer,) + (0,) * nd,
                        pipeline_mode=pl.Buffered(1))


def _rmsnorm(x, g):
    return x * lax.rsqrt(jnp.mean(x * x, axis=-1, keepdims=True) + EPS) * g


def _silu(x):
    return x * jax.nn.sigmoid(x)


def _dot(a, b):
    return jnp.dot(a, b, preferred_element_type=F32)


def _ffn_body(*refs, ff_chunks, final_norm, mixer_out, n_cast):
    if n_cast:
        cast_in, cast_out = refs[-2 * n_cast - 1:-n_cast - 1], refs[-n_cast:]
        refs = refs[:-2 * n_cast - 1] + (refs[-n_cast - 1],)
        for src, dst in zip(cast_in, cast_out):
            dst[...] = src[...].astype(dst.dtype)
    if mixer_out:
        (x_ref, g_ref, wg_ref, wu_ref, wd_ref, gf_ref,
         hn_ref, xc_ref, z_ref, hg_ref, sk_ref, wdm_ref, o_ref) = refs
    else:
        x_ref, g_ref, wg_ref, wu_ref, wd_ref, gf_ref, o_ref = refs
    x = x_ref[...]
    if mixer_out:
        for lo in range(0, hn_ref.shape[1], HEAD_DIM):
            cols = slice(lo, lo + HEAD_DIM)
            out = ((hn_ref[:, cols].astype(F32) * hg_ref[:, cols]
                    + sk_ref[:, cols] * xc_ref[:, cols].astype(F32))
                   * _silu(z_ref[:, cols].astype(F32)))
            x = x + _dot(out.astype(BF16), wdm_ref[cols, :])
    xn = _rmsnorm(x, g_ref[...]).astype(wg_ref.dtype)
    acc = None
    for lo, hi in ff_chunks:
        gate = _dot(xn, wg_ref[:, lo:hi])
        up = _dot(xn, wu_ref[:, lo:hi])
        h = (_silu(gate) * up).astype(wd_ref.dtype)
        part = _dot(h, wd_ref[lo:hi, :])
        acc = part if acc is None else acc + part
    y = x + 0.5 * acc
    if final_norm:
        y = _rmsnorm(y, gf_ref[...])
    o_ref[...] = y


CAST_BLOCKS = 16


def _ffn(x, norm_g, weights, layer, tm, final_g=None, mixer=None, cast=None):
    T, D = x.shape
    w_gate, w_up, w_down = weights
    d_ff = w_gate.shape[-1]
    n_steps = T // tm
    step = 4 * MXU_DIM
    ff_chunks = tuple((lo, min(lo + step, d_ff)) for lo in range(0, d_ff, step))
    final_norm = final_g is not None
    gf = final_g if final_norm else norm_g[layer]
    n_cast = 0 if cast is None else len(cast[0])
    body = functools.partial(_ffn_body, ff_chunks=ff_chunks, final_norm=final_norm,
                             mixer_out=mixer is not None, n_cast=n_cast)
    tok = lambda w: pl.BlockSpec((tm, w), lambda i: (i, 0))
    in_specs = [tok(D), _layer_spec((1, D), layer), _const_spec((D, d_ff)),
                _const_spec((D, d_ff)), _const_spec((d_ff, D)), _const_spec((1, D))]
    args = [x, norm_g.reshape(norm_g.shape[0], 1, D), w_gate, w_up, w_down, gf.reshape(1, D)]
    if mixer is not None:
        hn, xc, z, gain, skip, w_down_m, j = mixer
        inner = hn.shape[-1]
        in_specs += [tok(inner), tok(inner), tok(inner), _layer_spec((1, inner), j),
                     _layer_spec((1, inner), j), _layer_spec((inner, D), j)]
        args += [hn, xc, z, gain.reshape(gain.shape[0], 1, inner),
                 skip.reshape(skip.shape[0], 1, inner), w_down_m]
    out_shape = [jax.ShapeDtypeStruct((T, D), F32)]
    out_specs = [tok(D)]
    if cast is not None:
        mats, cast_layer = cast
        assert n_steps % CAST_BLOCKS == 0
        per_block = n_steps // CAST_BLOCKS
        for w in mats:
            rows, cols = w.shape[1] // CAST_BLOCKS, w.shape[2]
            in_specs.append(pl.BlockSpec((None, rows, cols), lambda i: (cast_layer, i // per_block, 0)))
            args.append(w)
            out_shape.append(jax.ShapeDtypeStruct(w.shape[1:], BF16))
            out_specs.append(pl.BlockSpec((rows, cols), lambda i: (i // per_block, 0)))
    outs = pl.pallas_call(
        body,
        out_shape=tuple(out_shape),
        grid=(n_steps,),
        in_specs=in_specs,
        out_specs=tuple(out_specs),
        compiler_params=_cparams(1, VMEM_LIMIT if cast is None else VMEM_LIMIT_CORE),
        name="ffn",
    )(*args)
    return outs[0] if cast is None else (outs[0], tuple(outs[1:]))


def _ml_pre_body(x_ref, g_ref, wup_ref, wc_ref, bc_ref, bq_ref, bk_ref, bv_ref, wgt_ref, bgt_ref,
                 hist_ref, q_ref, k_ref, v_ref, xc_ref, z_ref, gates_ref, tail_ref, xbuf,
                 *, tm, inner, header, stride, tiles_per_seq):
    i = pl.program_id(0)

    n_groups = inner // MXU_DIM

    @pl.when(i % tiles_per_seq == 0)
    def _():
        for g in range(n_groups):
            xbuf[g, 0:header, :] = hist_ref[:, g * MXU_DIM:(g + 1) * MXU_DIM]

    xn = _rmsnorm(x_ref[...], g_ref[...]).astype(BF16)

    def up_proj(g):
        cols = slice(g * MXU_DIM, (g + 1) * MXU_DIM)
        zcols = slice(inner + g * MXU_DIM, inner + (g + 1) * MXU_DIM)
        xm_g = _dot(xn, wup_ref[:, cols])
        z_ref[:, cols] = _dot(xn, wup_ref[:, zcols]).astype(BF16)
        return xm_g

    def gate_part(g, qb, kb, vb):
        rows = lambda base: slice(base + g * MXU_DIM, base + (g + 1) * MXU_DIM)
        return (_dot(qb, wgt_ref[rows(0), :]) + _dot(kb, wgt_ref[rows(inner), :])
                + _dot(vb, wgt_ref[rows(2 * inner), :]))

    gacc = bgt_ref[...]
    prev_qkv = None
    xm_next = up_proj(0)
    for g in range(n_groups):
        cols = slice(g * MXU_DIM, (g + 1) * MXU_DIM)
        xm = xm_next
        if g + 1 < n_groups:
            xm_next = up_proj(g + 1)
        xbuf[g, header:header + tm, :] = xm
        acc = bc_ref[:, cols] + wc_ref[CONV_W - 1:CONV_W, cols] * xm
        for d in range(1, CONV_W):
            lo = header - d * stride
            acc = acc + wc_ref[CONV_W - 1 - d:CONV_W - d, cols] * xbuf[g, lo:lo + tm, :]
        xcb = _silu(acc).astype(BF16)
        xc_ref[:, cols] = xcb
        qb = _dot(xcb, bq_ref[g]).astype(BF16)
        kb = _dot(xcb, bk_ref[g]).astype(BF16)
        vb = _dot(xm.astype(BF16), bv_ref[g]).astype(BF16)
        q_ref[:, cols] = qb
        k_ref[:, cols] = kb
        v_ref[:, cols] = vb
        if g > 0:
            gacc = gacc + gate_part(g - 1, *prev_qkv)
        prev_qkv = (qb, kb, vb)
        last = xbuf[g, tm:tm + header, :]
        tail_ref[:, cols] = last
        xbuf[g, 0:header, :] = last
    gates_ref[...] = gacc + gate_part(n_groups - 1, *prev_qkv)


def _ml_pre(x, norm_g, layer, w_up, w_conv, b_conv, bq, bk, bv, w_gates, b_gates, hist, j,
            tm, stride, tiles_per_seq):
    T, D = x.shape
    inner = w_conv.shape[-1]
    header = hist.shape[1]
    n_tiles = T // tm
    tok = lambda w: pl.BlockSpec((tm, w), lambda i: (i, 0))
    in_specs = [
        tok(D),
        _layer_spec((1, D), layer),
        _layer_spec((D, 2 * inner), j),
        _layer_spec((CONV_W, inner), j),
        _layer_spec((1, inner), j),
        _const_spec(bq.shape), _const_spec(bk.shape), _const_spec(bv.shape),
        _const_spec(w_gates.shape), _const_spec(b_gates.shape),
        pl.BlockSpec((None, header, inner), lambda i: (i // tiles_per_seq, 0, 0)),
    ]
    args = [x, norm_g.reshape(norm_g.shape[0], 1, D), w_up, w_conv,
            b_conv.reshape(b_conv.shape[0], 1, inner), bq, bk, bv, w_gates, b_gates, hist]
    out_shape = [
        jax.ShapeDtypeStruct((T, inner), BF16),
        jax.ShapeDtypeStruct((T, inner), BF16),
        jax.ShapeDtypeStruct((T, inner), BF16),
        jax.ShapeDtypeStruct((T, inner), BF16),
        jax.ShapeDtypeStruct((T, inner), BF16),
        jax.ShapeDtypeStruct((T, 2 * LANE), F32),
        jax.ShapeDtypeStruct((n_tiles, header, inner), F32),
    ]
    out_specs = [tok(inner), tok(inner), tok(inner), tok(inner), tok(inner), tok(2 * LANE),
                 pl.BlockSpec((None, header, inner), lambda i: (i, 0, 0))]
    body = functools.partial(_ml_pre_body, tm=tm, inner=inner, header=header, stride=stride,
                             tiles_per_seq=tiles_per_seq)
    return pl.pallas_call(
        body,
        out_shape=tuple(out_shape),
        grid=(n_tiles,),
        in_specs=in_specs,
        out_specs=tuple(out_specs),
        scratch_shapes=[pltpu.VMEM((inner // MXU_DIM, header + tm, MXU_DIM), F32)],
        compiler_params=_cparams(1),
        name="ml_pre",
    )(*args)


def _log_sigmoid(x):
    return jnp.minimum(x, 0.0) - jnp.log(1.0 + jnp.exp(-jnp.abs(x)))


def _token_scan(x, op, fill):
    n = x.shape[0]
    tok = lax.broadcasted_iota(jnp.int32, x.shape, 0)
    d = 1
    while d < n:
        x = op(x, jnp.where(tok >= d, pltpu.roll(x, d, axis=0), fill))
        d *= 2
    return x


def _gate_vectors(gates, m_prev, L):
    ig = gates[:, :LANE]
    bt = _token_scan(_log_sigmoid(gates[:, LANE:]), jnp.add, 0.0)
    a = ig - bt
    m = bt + jnp.maximum(_token_scan(a, jnp.maximum, NEG), m_prev)
    dcol = bt - m
    bl, ml = bt[L - 1:L, :], m[L - 1:L, :]
    return dict(a=a, dcol=dcol, w_inter=jnp.exp(m_prev + dcol), exp_neg_m=jnp.exp(-m),
                gs=jnp.exp(bl - bt + ig - ml) * HEAD_DIM ** -0.5,
                g_inter=jnp.exp(m_prev + bl - ml), ml=ml)


def _ml_core_body(*refs, cfgs):
    n_in = [8 if has_state else 5 for (_, _, _, has_state) in cfgs]
    pos_out = sum(n_in)
    pos_in = 0
    for (L, nb, nc, has_state), k_in in zip(cfgs, n_in):
        _ml_core_group(refs[pos_in:pos_in + k_in], refs[pos_out:pos_out + 4], L, nb, nc, has_state)
        pos_in += k_in
        pos_out += 4


def _ml_core_group(in_refs, out_refs, L, nb, nc, has_state):
    h_ref, c_ref, n_ref, m_ref = out_refs
    if has_state:
        q_ref, k_ref, v_ref, gates_ref, c0_ref, n0_ref, m0_ref, _ = in_refs
    else:
        q_ref, k_ref, v_ref, gates_ref, _ = in_refs
    if has_state and nc == 1:
        c_prev, n_prev_ref, m_prev_ref = c0_ref, n0_ref, m0_ref
    else:
        c_prev, n_prev_ref, m_prev_ref = c_ref, n_ref, m_ref

        @pl.when(pl.program_id(0) % nc == 0)
        def _():
            if has_state:
                c_ref[...] = c0_ref[...]
                n_ref[...] = n0_ref[...]
                m_ref[...] = m0_ref[...]
            else:
                c_ref[...] = jnp.zeros_like(c_ref)
                n_ref[...] = jnp.zeros_like(n_ref)
                m_ref[...] = jnp.zeros_like(m_ref)

    log_scale = -0.5 * math.log(HEAD_DIM)
    row = lax.broadcasted_iota(jnp.int32, (L, L), 0)
    col = lax.broadcasted_iota(jnp.int32, (L, L), 1)
    causal = col <= row
    heads = range(N_HEADS)
    seqs = range(nb)
    hcols = [slice(h * HEAD_DIM, (h + 1) * HEAD_DIM) for h in heads]
    ones_rows = jnp.ones((SUBLANE, L), BF16)
    n_prev = [n_prev_ref[s] for s in seqs]


    qc = [[_dot(q_ref[s, :, hcols[h]], c_prev[s, h].astype(BF16)) for h in heads] for s in seqs]

    gvs, a_rows = [], []
    for s in seqs:
        gv = _gate_vectors(gates_ref[s], m_prev_ref[s, 0:1, :], L)
        m_ref[s] = jnp.broadcast_to(gv['ml'], m_ref.shape[1:])
        a_shift = gv['a'] + log_scale
        if L % LANE == 0:
            a_t = a_shift.T
            a_rows.append([a_t[h:h + 1, :] for h in heads])
        else:
            a_rows.append([jnp.sum(jnp.where(row == col, a_shift[:, h:h + 1], 0.0), axis=0,
                                   keepdims=True) for h in heads])
        gvs.append(gv)

    for s in seqs:
        hcol = lambda name, h: gvs[s][name][:, h:h + 1]
        for h in heads:
            qh, kh, vh = q_ref[s, :, hcols[h]], k_ref[s, :, hcols[h]], v_ref[s, :, hcols[h]]

            g_inter = hcol('g_inter', h)
            kg = kh * hcol('gs', h).astype(BF16)
            upd = lax.dot_general(kg, vh, (((0,), (0,)), ((), ())), preferred_element_type=F32)
            c_ref[s, h] = g_inter * c_prev[s, h] + upd
            n_ref[s, h:h + 1, :] = (g_inter * n_prev[s][h:h + 1, :]
                                    + _dot(ones_rows, kg)[0:1, :])

            w_inter = hcol('w_inter', h)
            w_intra = jnp.exp(jnp.where(causal, hcol('dcol', h) + a_rows[s][h], NEG))
            sc = lax.dot_general(qh, kh, (((1,), (1,)), ((), ())),
                                 preferred_element_type=F32) * w_intra
            num = _dot(sc.astype(BF16), vh) + qc[s][h] * w_inter
            qn = jnp.sum((qh * n_prev[s][h:h + 1, :].astype(BF16)).astype(F32), axis=1,
                         keepdims=True)
            nq = jnp.sum(sc, axis=1, keepdims=True) + w_inter * qn
            den = jnp.maximum(jnp.abs(nq), hcol('exp_neg_m', h))
            mu = jnp.mean(num, axis=-1, keepdims=True)
            hc = num - mu
            var = jnp.mean(hc * hc, axis=-1, keepdims=True)
            h_ref[s, :, hcols[h]] = (hc * lax.rsqrt(var + EPS * den * den)).astype(BF16)


def _ml_core(groups, j):
    ios = [_core_group_io(g, j) for g in groups]
    steps = {io['steps'] for io in ios}
    assert len(steps) == 1, "token groups must have the same number of grid steps"
    in_specs, args, out_specs, out_shape, aliases = [], [], [], [], {}
    for gi, io in enumerate(ios):
        if io['alias_in'] is not None:
            aliases[len(args) + io['alias_in']] = N_CORE_OUT * gi + 1
        in_specs += io['in_specs']
        args += io['args']
        out_specs += io['out_specs']
        out_shape += io['out_shape']
    outs = pl.pallas_call(
        functools.partial(_ml_core_body, cfgs=tuple(io['cfg'] for io in ios)),
        out_shape=tuple(out_shape),
        grid=(steps.pop(),),
        in_specs=in_specs,
        out_specs=tuple(out_specs),
        input_output_aliases=aliases,
        compiler_params=_cparams(1, VMEM_LIMIT_CORE),
        name="ml_core",
    )(*args)
    return [_core_group_result(g, outs[N_CORE_OUT * gi:N_CORE_OUT * (gi + 1)])
            for gi, g in enumerate(groups)]


N_CORE_OUT = 4


def _core_group_io(g, j):
    n_ml = 2
    B, nc, L, nb = g['B'], g['nc'], g['L'], g['nb']
    inner = g['q'].shape[-1]
    has_state = g['state'] is not None
    seq3 = lambda a: a.reshape(B, nc * L, a.shape[-1])
    tok = lambda w: pl.BlockSpec((nb, L, w), lambda t: (t // nc, t % nc, 0))
    per_seq = lambda shape: pl.BlockSpec((nb,) + shape, lambda t: (t // nc,) + (0,) * len(shape))
    per_seq_layer = lambda shape: pl.BlockSpec(
        (None, nb) + shape, lambda t: (j, t // nc) + (0,) * len(shape))
    c_block = per_seq_layer((N_HEADS, HEAD_DIM, HEAD_DIM))
    in_specs = [tok(inner), tok(inner), tok(inner), tok(2 * LANE)]
    args = [seq3(g['q']), seq3(g['k']), seq3(g['v']), seq3(g['gates'])]
    if has_state:
        in_specs += [c_block, per_seq_layer((N_HEADS, HEAD_DIM)), per_seq((SUBLANE, LANE))]
        args += list(g['state'])
    in_specs.append(pl.BlockSpec(memory_space=pl.ANY))
    alias_in = None
    if g['c_all'] is not None:
        args.append(g['c_all'])
        alias_in = len(args) - 1
    else:
        args.append(jnp.zeros((SUBLANE, LANE), F32))
    out_shape = [jax.ShapeDtypeStruct((B, nc * L, inner), BF16),
                 jax.ShapeDtypeStruct((n_ml, B, N_HEADS, HEAD_DIM, HEAD_DIM), F32),
                 jax.ShapeDtypeStruct((B, N_HEADS, HEAD_DIM), F32),
                 jax.ShapeDtypeStruct((B, SUBLANE, LANE), F32)]
    out_specs = [tok(inner), c_block, per_seq((N_HEADS, HEAD_DIM)), per_seq((SUBLANE, LANE))]
    return dict(in_specs=in_specs, args=args, out_specs=out_specs, out_shape=out_shape,
                cfg=(L, nb, nc, has_state), alias_in=alias_in, steps=B // nb * nc)


def _core_group_result(g, outs):
    hn, c_all, n, m = outs
    return hn.reshape(g['q'].shape), c_all, n, m


def _gelu(x):
    return 0.5 * x * (1.0 + lax.erf(x * (2.0 ** -0.5)))


def _cm_body(*refs, width, sub, emit_vn):
    if emit_vn:
        x_ref, g_ref, win_ref, bin_ref, lg_ref, wmix_ref, bmix_ref, wout_ref, o_ref, vn_ref = refs
    else:
        x_ref, g_ref, win_ref, bin_ref, lg_ref, wmix_ref, bmix_ref, wout_ref, o_ref = refs
    gd = width // CM_GROUPS
    n_sub = x_ref.shape[0] // sub

    def in_proj(s):
        x = x_ref[s * sub:(s + 1) * sub, :]
        xn = _rmsnorm(x, g_ref[...]).astype(BF16)
        v = _gelu(_dot(xn, win_ref[:, width:]) + bin_ref[:, width:])
        u = _gelu(_dot(xn, win_ref[:, :width]) + bin_ref[:, :width])
        return x, u, v

    def gate_and_out(s, x, u, v):
        rows = slice(s * sub, (s + 1) * sub)
        mu = jnp.mean(v, axis=-1, keepdims=True)
        vc = v - mu
        var = jnp.mean(vc * vc, axis=-1, keepdims=True)
        vn = vc * lax.rsqrt(var + EPS) * lg_ref[...]
        if emit_vn:
            vn_ref[rows, :] = vn
        vnb = vn.astype(BF16)
        bmix = bmix_ref[...]
        parts = []
        for g in range(CM_GROUPS):
            cols = slice(g * gd, (g + 1) * gd)
            mix = _dot(wmix_ref[g], vnb[:, cols]) + bmix[:, g:g + 1]
            parts.append((u[:, cols] * mix).astype(BF16))
        o_ref[rows, :] = x + _dot(jnp.concatenate(parts, axis=-1), wout_ref[...])

    nxt = in_proj(0)
    for s in range(n_sub):
        cur = nxt
        if s + 1 < n_sub:
            nxt = in_proj(s + 1)
        gate_and_out(s, *cur)


def _chunk_mlp(x, norm_g, layer, w_in, b_in, ln_g, wmix, bmix, w_out, j, tm, emit_vn):
    T, D = x.shape
    width = ln_g.shape[-1]
    body = functools.partial(_cm_body, width=width, sub=wmix.shape[1], emit_vn=emit_vn)
    tok = lambda w: pl.BlockSpec((tm, w), lambda i: (i, 0))
    out_shape = [jax.ShapeDtypeStruct((T, D), F32)]
    out_specs = [tok(D)]
    if emit_vn:
        out_shape.append(jax.ShapeDtypeStruct((T, width), F32))
        out_specs.append(tok(width))
    return pl.pallas_call(
        body,
        out_shape=tuple(out_shape),
        grid=(T // tm,),
        in_specs=[tok(D), _layer_spec((1, D), layer),
                  _layer_spec((D, 2 * width), j), _layer_spec((1, 2 * width), j),
                  _layer_spec((1, width), j),
                  _const_spec(wmix.shape), _const_spec(bmix.shape),
                  _layer_spec((width, D), j)],
        out_specs=tuple(out_specs),
        compiler_params=_cparams(1),
        name="chunk_mlp",
    )(x, norm_g.reshape(norm_g.shape[0], 1, D), w_in, b_in.reshape(b_in.shape[0], 1, 2 * width),
      ln_g.reshape(ln_g.shape[0], 1, width), wmix, bmix, w_out)


def _blockdiag_dense(w):
    rows = w.reshape(-1, MXU_DIM, QKV_BLOCK)
    r = lax.broadcasted_iota(jnp.int32, (MXU_DIM, MXU_DIM), 0)
    c = lax.broadcasted_iota(jnp.int32, (MXU_DIM, MXU_DIM), 1)
    spread = (lax.broadcasted_iota(jnp.int32, (QKV_BLOCK, MXU_DIM), 1) % QKV_BLOCK
              == lax.broadcasted_iota(jnp.int32, (QKV_BLOCK, MXU_DIM), 0)).astype(w.dtype)
    dense = jnp.einsum('grd,dc->grc', rows, spread, precision=lax.Precision.HIGHEST)
    return jnp.where(r // QKV_BLOCK == c // QKV_BLOCK, dense, 0).astype(BF16)


def _gate_weights(w_ig, b_ig, w_fg, b_fg):
    wpad = jnp.zeros((w_ig.shape[0], LANE - N_HEADS), w_ig.dtype)
    bpad = jnp.zeros((LANE - N_HEADS,), b_ig.dtype)
    w = jnp.concatenate([w_ig, wpad, w_fg, wpad], axis=1)
    b = jnp.concatenate([b_ig, bpad, b_fg, bpad])
    return w.astype(BF16), b.reshape(1, 2 * LANE)


def _trunks(xs, grps, p):
    depth = p['norm_ff1'].shape[0]
    sts = [dict(x=x, c_all=None, n=[], m=[], conv=[], v=[]) for x in xs]

    ffn_w = {'cur': tuple(w[0].astype(BF16) for w in p['ffn1_w'])}

    def ffn_round(norm_key, i, nxt, use_mixer=False, final_g=None):
        new_w = None
        for gi, (grp, st) in enumerate(zip(grps, sts)):
            mixer = st['mixer'] if use_mixer else None
            tm = grp['tm'] if mixer is None else grp['tm_mixer']
            cast = nxt if gi == 0 else None
            out = _ffn(st['x'], p[norm_key], ffn_w['cur'], i, tm, final_g=final_g, mixer=mixer,
                       cast=cast)
            if cast is not None:
                out, new_w = out
            st['x'] = out
        ffn_w['cur'] = new_w

    for i in range(depth):
        j = i // 2
        ffn_round('norm_ff1', i, (p['ffn2_w'], i))
        if i % 2 == 0:
            core_in = []
            for grp, st in zip(grps, sts):
                q, k, v, xc, z, gates, tail = _ml_pre(
                    st['x'], p['norm_mix'], i, p['ml_w_up'], p['ml_w_conv'], p['ml_b_conv'],
                    p['bq'][j], p['bk'][j], p['bv'][j], p['w_gates'][j], p['b_gates'][j],
                    grp['hist'][j], j, grp['pre_tm'], grp['stride'], grp['tiles_per_seq'])
                st['conv'].append(grp['conv_out'](tail))
                st['xc'], st['z'] = xc, z
                q, k, v, gates = grp['to_core'](q, k, v, gates)
                core_in.append(dict(q=q, k=k, v=v, gates=gates, B=grp['B'], nc=grp['nc'], L=grp['L'],
                                    nb=grp['nb'], state=grp['state'](j), c_all=st['c_all']))
            for grp, st, (hn, c_all, n, m) in zip(grps, sts, _ml_core(core_in, j)):
                st['c_all'] = c_all
                st['n'].append(n)
                st['m'].append(m[:, 0, :N_HEADS])
                st['mixer'] = (grp['from_core'](hn), st['xc'], st['z'], p['ml_hn_g'], p['ml_skip'],
                               p['ml_w_down'], j)
        else:
            for grp, st in zip(grps, sts):
                outs = _chunk_mlp(st['x'], p['norm_mix'], i, p['cm_w_in'], p['cm_b_in'], p['cm_ln_g'],
                                  grp['wmix'][j], grp['bmix'][j], p['cm_w_out'], j, grp['cm_tm'],
                                  emit_vn=grp['emit_vn'])
                st['x'] = outs[0]
                st['v'].append(outs[1] if grp['emit_vn'] else None)
                st['mixer'] = None
        last = i == depth - 1
        ffn_round('norm_ff2', i, None if last else (p['ffn1_w'], i + 1), use_mixer=True,
                  final_g=p['norm_final'] if last else None)
    return [(st['x'], st['c_all'], jnp.stack(st['n']), jnp.stack(st['m']), jnp.stack(st['conv']),
             st['v']) for st in sts]


def kernel(x_prompt, x_sample, state_C, state_n, state_m, state_conv, norm_ff1, norm_mix, norm_ff2, norm_final, ffn1_w_gate, ffn1_w_up, ffn1_w_down, ffn2_w_gate, ffn2_w_up, ffn2_w_down, ml_w_up, ml_w_conv, ml_b_conv, ml_w_q, ml_w_k, ml_w_v, ml_w_ig, ml_b_ig, ml_w_fg, ml_b_fg, ml_hn_g, ml_skip, ml_w_down, cm_w_in, cm_b_in, cm_ln_g, cm_w_s, cm_b_s, cm_w_out):
    bp, sp, d_model = x_prompt.shape
    bs, ss, _ = x_sample.shape
    n_ml, inner = ml_w_conv.shape[0], ml_w_conv.shape[-1]
    n_cm, width = cm_ln_g.shape
    hist_rows = CONV_W - 1
    sb = min(64, bs)

    gate_wb = [_gate_weights(ml_w_ig[j], ml_b_ig[j], ml_w_fg[j], ml_b_fg[j]) for j in range(n_ml)]
    p = {
        'norm_ff1': norm_ff1, 'norm_mix': norm_mix, 'norm_ff2': norm_ff2, 'norm_final': norm_final,
        'ffn1_w': (ffn1_w_gate, ffn1_w_up, ffn1_w_down),
        'ffn2_w': (ffn2_w_gate, ffn2_w_up, ffn2_w_down),
        'ml_w_up': ml_w_up.astype(BF16), 'ml_w_conv': ml_w_conv, 'ml_b_conv': ml_b_conv,
        'bq': [_blockdiag_dense(ml_w_q[j]) for j in range(n_ml)],
        'bk': [_blockdiag_dense(ml_w_k[j]) for j in range(n_ml)],
        'bv': [_blockdiag_dense(ml_w_v[j]) for j in range(n_ml)],
        'w_gates': [wb[0] for wb in gate_wb], 'b_gates': [wb[1] for wb in gate_wb],
        'ml_hn_g': ml_hn_g, 'ml_skip': ml_skip, 'ml_w_down': ml_w_down.astype(BF16),
        'cm_w_in': cm_w_in.astype(BF16), 'cm_b_in': cm_b_in, 'cm_ln_g': cm_ln_g,
        'cm_w_out': cm_w_out.astype(BF16),
    }

    def mix_weights(L, kron_left):
        causal = jnp.tril(jnp.ones((L, L), dtype=bool))
        ws = jnp.where(causal, cm_w_s[:, :, :L, :L], 0)
        bs_ = cm_b_s[:, :, :L]
        if kron_left:
            reps = kron_left
            eye = jnp.eye(reps, dtype=ws.dtype)
            wm = ws[:, :, None, :, None, :] * eye[None, None, :, None, :, None]
            wm = wm.reshape(n_cm, CM_GROUPS, reps * L, reps * L)
            bm = jnp.tile(bs_, (1, 1, reps))
        else:
            eye = jnp.eye(sb, dtype=ws.dtype)
            wm = ws[:, :, :, None, :, None] * eye[None, None, None, :, None, :]
            wm = wm.reshape(n_cm, CM_GROUPS, L * sb, L * sb)
            bm = jnp.repeat(bs_, sb, axis=2)
        bm = jnp.swapaxes(bm, 1, 2)
        bm = jnp.concatenate([bm, jnp.zeros(bm.shape[:2] + (LANE - CM_GROUPS,), bm.dtype)], axis=2)
        return wm.astype(BF16), bm

    tm_p = 1024
    pre_tm_p = 512
    cm_tm_p = 1024
    chunk_p = min(256, sp)
    cm_chunk_p = min(128, sp)
    wmix_p, bmix_p = mix_weights(cm_chunk_p, MXU_DIM // cm_chunk_p)
    grp_p = {
        'tm': tm_p, 'tm_mixer': 512, 'pre_tm': pre_tm_p, 'cm_tm': cm_tm_p, 'emit_vn': False,
        'stride': 1,
        'tiles_per_seq': sp // pre_tm_p,
        'hist': [jnp.zeros((bp, SUBLANE, inner), F32)] * n_ml,
        'conv_out': lambda tail: tail.reshape(bp, sp // pre_tm_p, SUBLANE, inner)[:, -1, SUBLANE - hist_rows:, :],
        'to_core': lambda q, k, v, g: (q, k, v, g),
        'from_core': lambda hn: hn,
        'state': lambda j: None,
        'B': bp, 'nc': sp // chunk_p, 'L': chunk_p, 'nb': 1,
        'wmix': wmix_p, 'bmix': bmix_p,
    }

    ts = bs * ss
    n_sg = bs // sb
    l_pad = SUBLANE

    def tile_major(a):
        steps, feat = a.shape[1], a.shape[2]
        return a.reshape(n_sg, sb, steps, feat).transpose(0, 2, 1, 3).reshape(n_sg * steps * sb, feat)

    def batch_major(a, steps):
        feat = a.shape[-1]
        return a.reshape(n_sg, steps, sb, feat).transpose(0, 2, 1, 3).reshape(bs, steps, feat)

    def to_core(q, k, v, g):
        def seq_major(a, fill):
            a = batch_major(a, ss)
            pad = jnp.broadcast_to(fill.astype(a.dtype), (bs, l_pad - ss, a.shape[-1]))
            return jnp.concatenate([a, pad], axis=1).reshape(bs * l_pad, a.shape[-1])
        zero = jnp.zeros((inner,), F32)
        lane_pad = jnp.zeros((LANE - N_HEADS,), F32)
        gfill = jnp.concatenate([jnp.full((N_HEADS,), NEG, F32), lane_pad,
                                 jnp.full((N_HEADS,), -NEG, F32), lane_pad])
        return seq_major(q, zero), seq_major(k, zero), seq_major(v, zero), seq_major(g, gfill)

    def from_core(hn):
        return tile_major(hn.reshape(bs, l_pad, inner)[:, :ss])

    def sample_state(j):
        m0 = jnp.pad(state_m[j], ((0, 0), (0, LANE - N_HEADS)))
        return state_C, state_n, jnp.broadcast_to(m0[:, None, :], (bs, SUBLANE, LANE))

    wmix_s, bmix_s = mix_weights(ss, 0)
    tile_s = ss * sb
    hist_s = [tile_major(state_conv[j]).reshape(n_sg, hist_rows * sb, inner) for j in range(n_ml)]
    grp_s = {
        'tm': tile_s, 'tm_mixer': tile_s, 'pre_tm': tile_s, 'cm_tm': tile_s, 'emit_vn': True,
        'stride': sb,
        'tiles_per_seq': 1,
        'hist': hist_s,
        'conv_out': lambda tail: batch_major(tail.reshape(n_sg * hist_rows * sb, inner), hist_rows),
        'to_core': to_core, 'from_core': from_core, 'state': sample_state,
        'B': bs, 'nc': 1, 'L': l_pad, 'nb': 2,
        'wmix': wmix_s, 'bmix': bmix_s,
    }
    ((yp, c_p, n_p, m_p, conv_p, _), (ys, c_s, n_s, m_s, conv_s, v_rows)) = _trunks(
        [x_prompt.reshape(bp * sp, d_model), tile_major(x_sample)], [grp_p, grp_s], p)

    y_prompt = yp.reshape(bp, sp, d_model)
    y_sample = batch_major(ys, ss)
    v_sample = jnp.stack([batch_major(vr, ss) for vr in v_rows])
    return (y_prompt, y_sample, c_p, n_p, m_p, conv_p, c_s, n_s, m_s, conv_s, v_sample)
```

```python
import functools
import math

import jax
import jax.numpy as jnp
from jax import lax
from jax.experimental import pallas as pl
from jax.experimental.pallas import tpu as pltpu

F32 = jnp.float32
BF16 = jnp.bfloat16

EPS = 1e-6
N_HEADS = 4
HEAD_DIM = 512
QKV_BLOCK = 4
CONV_W = 4
CM_GROUPS = 4
NEG = -1e30

LANE = 128
SUBLANE = 8
MXU_DIM = 256
VMEM_LIMIT = 56 * 1024 * 1024
VMEM_LIMIT_CORE = 60 * 1024 * 1024


def _cparams(n_axes, vmem_limit=VMEM_LIMIT):
    return pltpu.CompilerParams(dimension_semantics=("arbitrary",) * n_axes,
                                vmem_limit_bytes=vmem_limit)


def _const_spec(shape):
    nd = len(shape)
    return pl.BlockSpec(shape, lambda *_: (0,) * nd, pipeline_mode=pl.Buffered(1))


def _layer_spec(shape, layer):
    nd = len(shape)
    return pl.BlockSpec((None,) + tuple(shape), lambda *_: (layer,) + (0,) * nd,
                        pipeline_mode=pl.Buffered(1))


def _rmsnorm(x, g):
    return x * lax.rsqrt(jnp.mean(x * x, axis=-1, keepdims=True) + EPS) * g


def _silu(x):
    return x * jax.nn.sigmoid(x)


def _dot(a, b):
    return jnp.dot(a, b, preferred_element_type=F32)


def _ffn_body(*refs, ff_chunks, final_norm, mixer_out, n_cast):
    if n_cast:
        cast_in, cast_out = refs[-2 * n_cast - 1:-n_cast - 1], refs[-n_cast:]
        refs = refs[:-2 * n_cast - 1] + (refs[-n_cast - 1],)
        for src, dst in zip(cast_in, cast_out):
            dst[...] = src[...].astype(dst.dtype)
    if mixer_out:
        (x_ref, g_ref, wg_ref, wu_ref, wd_ref, gf_ref,
         hn_ref, xc_ref, z_ref, hg_ref, sk_ref, wdm_ref, o_ref) = refs
    else:
        x_ref, g_ref, wg_ref, wu_ref, wd_ref, gf_ref, o_ref = refs
    x = x_ref[...]
    if mixer_out:
        for lo in range(0, hn_ref.shape[1], HEAD_DIM):
            cols = slice(lo, lo + HEAD_DIM)
            out = ((hn_ref[:, cols].astype(F32) * hg_ref[:, cols]
                    + sk_ref[:, cols] * xc_ref[:, cols].astype(F32))
                   * _silu(z_ref[:, cols].astype(F32)))
            x = x + _dot(out.astype(BF16), wdm_ref[cols, :])
    xn = _rmsnorm(x, g_ref[...]).astype(wg_ref.dtype)
    acc = None
    for lo, hi in ff_chunks:
        gate = _dot(xn, wg_ref[:, lo:hi])
        up = _dot(xn, wu_ref[:, lo:hi])
        h = (_silu(gate) * up).astype(wd_ref.dtype)
        part = _dot(h, wd_ref[lo:hi, :])
        acc = part if acc is None else acc + part
    y = x + 0.5 * acc
    if final_norm:
        y = _rmsnorm(y, gf_ref[...])
    o_ref[...] = y


CAST_BLOCKS = 16


def _ffn(x, norm_g, weights, layer, tm, final_g=None, mixer=None, cast=None):
    T, D = x.shape
    w_gate, w_up, w_down = weights
    d_ff = w_gate.shape[-1]
    n_steps = T // tm
    step = 4 * MXU_DIM
    ff_chunks = tuple((lo, min(lo + step, d_ff)) for lo in range(0, d_ff, step))
    final_norm = final_g is not None
    gf = final_g if final_norm else norm_g[layer]
    n_cast = 0 if cast is None else len(cast[0])
    body = functools.partial(_ffn_body, ff_chunks=ff_chunks, final_norm=final_norm,
                             mixer_out=mixer is not None, n_cast=n_cast)
    tok = lambda w: pl.BlockSpec((tm, w), lambda i: (i, 0))
    in_specs = [tok(D), _layer_spec((1, D), layer), _const_spec((D, d_ff)),
                _const_spec((D, d_ff)), _const_spec((d_ff, D)), _const_spec((1, D))]
    args = [x, norm_g.reshape(norm_g.shape[0], 1, D), w_gate, w_up, w_down, gf.reshape(1, D)]
    if mixer is not None:
        hn, xc, z, gain, skip, w_down_m, j = mixer
        inner = hn.shape[-1]
        in_specs += [tok(inner), tok(inner), tok(inner), _layer_spec((1, inner), j),
                     _layer_spec((1, inner), j), _layer_spec((inner, D), j)]
        args += [hn, xc, z, gain.reshape(gain.shape[0], 1, inner),
                 skip.reshape(skip.shape[0], 1, inner), w_down_m]
    out_shape = [jax.ShapeDtypeStruct((T, D), F32)]
    out_specs = [tok(D)]
    if cast is not None:
        mats, cast_layer = cast
        assert n_steps % CAST_BLOCKS == 0
        per_block = n_steps // CAST_BLOCKS
        for w in mats:
            rows, cols = w.shape[1] // CAST_BLOCKS, w.shape[2]
            in_specs.append(pl.BlockSpec((None, rows, cols), lambda i: (cast_layer, i // per_block, 0)))
            args.append(w)
            out_shape.append(jax.ShapeDtypeStruct(w.shape[1:], BF16))
            out_specs.append(pl.BlockSpec((rows, cols), lambda i: (i // per_block, 0)))
    outs = pl.pallas_call(
        body,
        out_shape=tuple(out_shape),
        grid=(n_steps,),
        in_specs=in_specs,
        out_specs=tuple(out_specs),
        compiler_params=_cparams(1, VMEM_LIMIT if cast is None else VMEM_LIMIT_CORE),
        name="ffn",
    )(*args)
    return outs[0] if cast is None else (outs[0], tuple(outs[1:]))


N_PRE_IN, N_PRE_OUT = 11, 7


def _ml_pre_body(*refs, tm, inner, header, stride, tiles_per_seq, guest_cfg):
    (x_ref, g_ref, wup_ref, wc_ref, bc_ref, bq_ref, bk_ref, bv_ref, wgt_ref, bgt_ref,
     hist_ref) = refs[:N_PRE_IN]
    xbuf = refs[-1]
    if guest_cfg is None:
        q_ref, k_ref, v_ref, xc_ref, z_ref, gates_ref, tail_ref = refs[N_PRE_IN:-1]
        guest = iter(())
    else:
        pos_out = N_PRE_IN + _core_n_in(guest_cfg)
        q_ref, k_ref, v_ref, xc_ref, z_ref, gates_ref, tail_ref = refs[pos_out:pos_out + N_PRE_OUT]
        guest = _ml_core_group(refs[N_PRE_IN:pos_out], refs[pos_out + N_PRE_OUT:-1], *guest_cfg)
    i = pl.program_id(0)

    n_groups = inner // MXU_DIM

    @pl.when(i % tiles_per_seq == 0)
    def _():
        for g in range(n_groups):
            xbuf[g, 0:header, :] = hist_ref[:, g * MXU_DIM:(g + 1) * MXU_DIM]

    xn = _rmsnorm(x_ref[...], g_ref[...]).astype(BF16)

    def up_proj(g):
        cols = slice(g * MXU_DIM, (g + 1) * MXU_DIM)
        zcols = slice(inner + g * MXU_DIM, inner + (g + 1) * MXU_DIM)
        xm_g = _dot(xn, wup_ref[:, cols])
        z_ref[:, cols] = _dot(xn, wup_ref[:, zcols]).astype(BF16)
        return xm_g

    def gate_part(g, qb, kb, vb):
        rows = lambda base: slice(base + g * MXU_DIM, base + (g + 1) * MXU_DIM)
        return (_dot(qb, wgt_ref[rows(0), :]) + _dot(kb, wgt_ref[rows(inner), :])
                + _dot(vb, wgt_ref[rows(2 * inner), :]))

    gacc = bgt_ref[...]
    prev_qkv = None
    xm_next = up_proj(0)
    for g in range(n_groups):
        cols = slice(g * MXU_DIM, (g + 1) * MXU_DIM)
        xm = xm_next
        if g + 1 < n_groups:
            xm_next = up_proj(g + 1)
        xbuf[g, header:header + tm, :] = xm
        acc = bc_ref[:, cols] + wc_ref[CONV_W - 1:CONV_W, cols] * xm
        for d in range(1, CONV_W):
            lo = header - d * stride
            acc = acc + wc_ref[CONV_W - 1 - d:CONV_W - d, cols] * xbuf[g, lo:lo + tm, :]
        xcb = _silu(acc).astype(BF16)
        xc_ref[:, cols] = xcb
        qb = _dot(xcb, bq_ref[g]).astype(BF16)
        kb = _dot(xcb, bk_ref[g]).astype(BF16)
        vb = _dot(xm.astype(BF16), bv_ref[g]).astype(BF16)
        q_ref[:, cols] = qb
        k_ref[:, cols] = kb
        v_ref[:, cols] = vb
        if g > 0:
            gacc = gacc + gate_part(g - 1, *prev_qkv)
        prev_qkv = (qb, kb, vb)
        last = xbuf[g, tm:tm + header, :]
        tail_ref[:, cols] = last
        xbuf[g, 0:header, :] = last
        next(guest, None)
    gates_ref[...] = gacc + gate_part(n_groups - 1, *prev_qkv)
    for _ in guest:
        pass


def _ml_pre(x, norm_g, layer, w_up, w_conv, b_conv, bq, bk, bv, w_gates, b_gates, hist, j,
            tm, stride, tiles_per_seq, guest=None):
    T, D = x.shape
    inner = w_conv.shape[-1]
    header = hist.shape[1]
    n_tiles = T // tm
    tok = lambda w: pl.BlockSpec((tm, w), lambda i: (i, 0))
    in_specs = [
        tok(D),
        _layer_spec((1, D), layer),
        _layer_spec((D, 2 * inner), j),
        _layer_spec((CONV_W, inner), j),
        _layer_spec((1, inner), j),
        _const_spec(bq.shape), _const_spec(bk.shape), _const_spec(bv.shape),
        _const_spec(w_gates.shape), _const_spec(b_gates.shape),
        pl.BlockSpec((None, header, inner), lambda i: (i // tiles_per_seq, 0, 0)),
    ]
    args = [x, norm_g.reshape(norm_g.shape[0], 1, D), w_up, w_conv,
            b_conv.reshape(b_conv.shape[0], 1, inner), bq, bk, bv, w_gates, b_gates, hist]
    out_shape = [
        jax.ShapeDtypeStruct((T, inner), BF16),
        jax.ShapeDtypeStruct((T, inner), BF16),
        jax.ShapeDtypeStruct((T, inner), BF16),
        jax.ShapeDtypeStruct((T, inner), BF16),
        jax.ShapeDtypeStruct((T, inner), BF16),
        jax.ShapeDtypeStruct((T, 2 * LANE), F32),
        jax.ShapeDtypeStruct((n_tiles, header, inner), F32),
    ]
    out_specs = [tok(inner), tok(inner), tok(inner), tok(inner), tok(inner), tok(2 * LANE),
                 pl.BlockSpec((None, header, inner), lambda i: (i, 0, 0))]
    assert len(in_specs) == N_PRE_IN and len(out_specs) == N_PRE_OUT
    aliases, guest_cfg, vmem = {}, None, VMEM_LIMIT
    if guest is not None:
        io = _core_group_io(guest, j)
        assert io['steps'] == n_tiles, "guest group needs one grid step per token tile"
        if io['alias_in'] is not None:
            aliases[len(args) + io['alias_in']] = len(out_specs) + 1
        in_specs += io['in_specs']
        args += io['args']
        out_specs += io['out_specs']
        out_shape += io['out_shape']
        guest_cfg = io['cfg']
        vmem = VMEM_LIMIT_CORE
    body = functools.partial(_ml_pre_body, tm=tm, inner=inner, header=header, stride=stride,
                             tiles_per_seq=tiles_per_seq, guest_cfg=guest_cfg)
    outs = pl.pallas_call(
        body,
        out_shape=tuple(out_shape),
        grid=(n_tiles,),
        in_specs=in_specs,
        out_specs=tuple(out_specs),
        scratch_shapes=[pltpu.VMEM((inner // MXU_DIM, header + tm, MXU_DIM), F32)],
        input_output_aliases=aliases,
        compiler_params=_cparams(1, vmem),
        name="ml_pre",
    )(*args)
    if guest is None:
        return outs
    return tuple(outs[:N_PRE_OUT]) + (_core_group_result(guest, outs[N_PRE_OUT:]),)


def _log_sigmoid(x):
    return jnp.minimum(x, 0.0) - jnp.log(1.0 + jnp.exp(-jnp.abs(x)))


def _token_scan(x, op, fill):
    n = x.shape[0]
    tok = lax.broadcasted_iota(jnp.int32, x.shape, 0)
    d = 1
    while d < n:
        x = op(x, jnp.where(tok >= d, pltpu.roll(x, d, axis=0), fill))
        d *= 2
    return x


def _gate_vectors(gates, m_prev, L):
    ig = gates[:, :LANE]
    bt = _token_scan(_log_sigmoid(gates[:, LANE:]), jnp.add, 0.0)
    a = ig - bt
    m = bt + jnp.maximum(_token_scan(a, jnp.maximum, NEG), m_prev)
    dcol = bt - m
    bl, ml = bt[L - 1:L, :], m[L - 1:L, :]
    return dict(a=a, dcol=dcol, w_inter=jnp.exp(m_prev + dcol), exp_neg_m=jnp.exp(-m),
                gs=jnp.exp(bl - bt + ig - ml) * HEAD_DIM ** -0.5,
                g_inter=jnp.exp(m_prev + bl - ml), ml=ml)


def _ml_core_body(*refs, cfgs):
    n_in = [_core_n_in(cfg) for cfg in cfgs]
    pos_out = sum(n_in)
    pos_in = 0
    stages = []
    for cfg, k_in in zip(cfgs, n_in):
        stages.append(_ml_core_group(refs[pos_in:pos_in + k_in],
                                     refs[pos_out:pos_out + N_CORE_OUT], *cfg))
        pos_in += k_in
        pos_out += N_CORE_OUT
    while stages:
        stages = [st for st in stages if next(st, 'done') != 'done']


def _core_n_in(cfg):
    has_state = cfg[3]
    return 8 if has_state else 5


def _ml_core_group(in_refs, out_refs, L, nb, nc, has_state):
    h_ref, c_ref, n_ref, m_ref = out_refs
    if has_state:
        q_ref, k_ref, v_ref, gates_ref, c0_ref, n0_ref, m0_ref, _ = in_refs
    else:
        q_ref, k_ref, v_ref, gates_ref, _ = in_refs
    if has_state and nc == 1:
        c_prev, n_prev_ref, m_prev_ref = c0_ref, n0_ref, m0_ref
    else:
        c_prev, n_prev_ref, m_prev_ref = c_ref, n_ref, m_ref

        @pl.when(pl.program_id(0) % nc == 0)
        def _():
            if has_state:
                c_ref[...] = c0_ref[...]
                n_ref[...] = n0_ref[...]
                m_ref[...] = m0_ref[...]
            else:
                c_ref[...] = jnp.zeros_like(c_ref)
                n_ref[...] = jnp.zeros_like(n_ref)
                m_ref[...] = jnp.zeros_like(m_ref)

    log_scale = -0.5 * math.log(HEAD_DIM)
    row = lax.broadcasted_iota(jnp.int32, (L, L), 0)
    col = lax.broadcasted_iota(jnp.int32, (L, L), 1)
    causal = col <= row
    heads = range(N_HEADS)
    seqs = range(nb)
    hcols = [slice(h * HEAD_DIM, (h + 1) * HEAD_DIM) for h in heads]
    ones_rows = jnp.ones((SUBLANE, L), BF16)
    n_prev = [n_prev_ref[s] for s in seqs]


    qc = [[_dot(q_ref[s, :, hcols[h]], c_prev[s, h].astype(BF16)) for h in heads] for s in seqs]
    yield

    gvs, a_rows = [], []
    for s in seqs:
        gv = _gate_vectors(gates_ref[s], m_prev_ref[s, 0:1, :], L)
        m_ref[s] = jnp.broadcast_to(gv['ml'], m_ref.shape[1:])
        a_shift = gv['a'] + log_scale
        if L % LANE == 0:
            a_t = a_shift.T
            a_rows.append([a_t[h:h + 1, :] for h in heads])
        else:
            a_rows.append([jnp.sum(jnp.where(row == col, a_shift[:, h:h + 1], 0.0), axis=0,
                                   keepdims=True) for h in heads])
        gvs.append(gv)
    yield

    for s in seqs:
        hcol = lambda name, h: gvs[s][name][:, h:h + 1]
        for h in heads:
            qh, kh, vh = q_ref[s, :, hcols[h]], k_ref[s, :, hcols[h]], v_ref[s, :, hcols[h]]

            g_inter = hcol('g_inter', h)
            kg = kh * hcol('gs', h).astype(BF16)
            upd = lax.dot_general(kg, vh, (((0,), (0,)), ((), ())), preferred_element_type=F32)
            c_ref[s, h] = g_inter * c_prev[s, h] + upd
            n_ref[s, h:h + 1, :] = (g_inter * n_prev[s][h:h + 1, :]
                                    + _dot(ones_rows, kg)[0:1, :])
            yield

            w_inter = hcol('w_inter', h)
            w_intra = jnp.exp(jnp.where(causal, hcol('dcol', h) + a_rows[s][h], NEG))
            sc = lax.dot_general(qh, kh, (((1,), (1,)), ((), ())),
                                 preferred_element_type=F32) * w_intra
            num = _dot(sc.astype(BF16), vh) + qc[s][h] * w_inter
            qn = jnp.sum((qh * n_prev[s][h:h + 1, :].astype(BF16)).astype(F32), axis=1,
                         keepdims=True)
            nq = jnp.sum(sc, axis=1, keepdims=True) + w_inter * qn
            den = jnp.maximum(jnp.abs(nq), hcol('exp_neg_m', h))
            mu = jnp.mean(num, axis=-1, keepdims=True)
            hc = num - mu
            var = jnp.mean(hc * hc, axis=-1, keepdims=True)
            h_ref[s, :, hcols[h]] = (hc * lax.rsqrt(var + EPS * den * den)).astype(BF16)
            yield


def _ml_core(groups, j):
    ios = [_core_group_io(g, j) for g in groups]
    steps = {io['steps'] for io in ios}
    assert len(steps) == 1, "token groups must have the same number of grid steps"
    in_specs, args, out_specs, out_shape, aliases = [], [], [], [], {}
    for gi, io in enumerate(ios):
        if io['alias_in'] is not None:
            aliases[len(args) + io['alias_in']] = N_CORE_OUT * gi + 1
        in_specs += io['in_specs']
        args += io['args']
        out_specs += io['out_specs']
        out_shape += io['out_shape']
    outs = pl.pallas_call(
        functools.partial(_ml_core_body, cfgs=tuple(io['cfg'] for io in ios)),
        out_shape=tuple(out_shape),
        grid=(steps.pop(),),
        in_specs=in_specs,
        out_specs=tuple(out_specs),
        input_output_aliases=aliases,
        compiler_params=_cparams(1, VMEM_LIMIT_CORE),
        name="ml_core",
    )(*args)
    return [_core_group_result(g, outs[N_CORE_OUT * gi:N_CORE_OUT * (gi + 1)])
            for gi, g in enumerate(groups)]


N_CORE_OUT = 4


def _core_group_io(g, j):
    n_ml = 2
    B, nc, L, nb = g['B'], g['nc'], g['L'], g['nb']
    blk0 = g.get('seq0', 0) // nb
    nseq = g.get('nseq', B)
    inner = g['q'].shape[-1]
    has_state = g['state'] is not None
    seq3 = lambda a: a.reshape(B, nc * L, a.shape[-1])
    tok = lambda w: pl.BlockSpec((nb, L, w), lambda t: (blk0 + t // nc, t % nc, 0))
    per_seq = lambda shape: pl.BlockSpec((nb,) + shape, lambda t: (blk0 + t // nc,) + (0,) * len(shape))
    per_seq_layer = lambda shape: pl.BlockSpec(
        (None, nb) + shape, lambda t: (j, blk0 + t // nc) + (0,) * len(shape))
    c_block = per_seq_layer((N_HEADS, HEAD_DIM, HEAD_DIM))
    in_specs = [tok(inner), tok(inner), tok(inner), tok(2 * LANE)]
    args = [seq3(g['q']), seq3(g['k']), seq3(g['v']), seq3(g['gates'])]
    if has_state:
        in_specs += [c_block, per_seq_layer((N_HEADS, HEAD_DIM)), per_seq((SUBLANE, LANE))]
        args += list(g['state'])
    in_specs.append(pl.BlockSpec(memory_space=pl.ANY))
    alias_in = None
    if g['c_all'] is not None:
        args.append(g['c_all'])
        alias_in = len(args) - 1
    else:
        args.append(jnp.zeros((SUBLANE, LANE), F32))
    out_shape = [jax.ShapeDtypeStruct((B, nc * L, inner), BF16),
                 jax.ShapeDtypeStruct((n_ml, B, N_HEADS, HEAD_DIM, HEAD_DIM), F32),
                 jax.ShapeDtypeStruct((B, N_HEADS, HEAD_DIM), F32),
                 jax.ShapeDtypeStruct((B, SUBLANE, LANE), F32)]
    out_specs = [tok(inner), c_block, per_seq((N_HEADS, HEAD_DIM)), per_seq((SUBLANE, LANE))]
    return dict(in_specs=in_specs, args=args, out_specs=out_specs, out_shape=out_shape,
                cfg=(L, nb, nc, has_state), alias_in=alias_in, steps=nseq // nb * nc)


def _core_group_result(g, outs):
    hn, c_all, n, m = outs
    return hn.reshape(g['q'].shape), c_all, n, m


def _gelu(x):
    return 0.5 * x * (1.0 + lax.erf(x * (2.0 ** -0.5)))


def _cm_body(*refs, width, sub, emit_vn):
    if emit_vn:
        x_ref, g_ref, win_ref, bin_ref, lg_ref, wmix_ref, bmix_ref, wout_ref, o_ref, vn_ref = refs
    else:
        x_ref, g_ref, win_ref, bin_ref, lg_ref, wmix_ref, bmix_ref, wout_ref, o_ref = refs
    gd = width // CM_GROUPS
    n_sub = x_ref.shape[0] // sub

    def in_proj(s):
        x = x_ref[s * sub:(s + 1) * sub, :]
        xn = _rmsnorm(x, g_ref[...]).astype(BF16)
        v = _gelu(_dot(xn, win_ref[:, width:]) + bin_ref[:, width:])
        u = _gelu(_dot(xn, win_ref[:, :width]) + bin_ref[:, :width])
        return x, u, v

    def gate_and_out(s, x, u, v):
        rows = slice(s * sub, (s + 1) * sub)
        mu = jnp.mean(v, axis=-1, keepdims=True)
        vc = v - mu
        var = jnp.mean(vc * vc, axis=-1, keepdims=True)
        vn = vc * lax.rsqrt(var + EPS) * lg_ref[...]
        if emit_vn:
            vn_ref[rows, :] = vn
        vnb = vn.astype(BF16)
        bmix = bmix_ref[...]
        parts = []
        for g in range(CM_GROUPS):
            cols = slice(g * gd, (g + 1) * gd)
            mix = _dot(wmix_ref[g], vnb[:, cols]) + bmix[:, g:g + 1]
            parts.append((u[:, cols] * mix).astype(BF16))
        o_ref[rows, :] = x + _dot(jnp.concatenate(parts, axis=-1), wout_ref[...])

    nxt = in_proj(0)
    for s in range(n_sub):
        cur = nxt
        if s + 1 < n_sub:
            nxt = in_proj(s + 1)
        gate_and_out(s, *cur)


def _chunk_mlp(x, norm_g, layer, w_in, b_in, ln_g, wmix, bmix, w_out, j, tm, emit_vn):
    T, D = x.shape
    width = ln_g.shape[-1]
    body = functools.partial(_cm_body, width=width, sub=wmix.shape[1], emit_vn=emit_vn)
    tok = lambda w: pl.BlockSpec((tm, w), lambda i: (i, 0))
    out_shape = [jax.ShapeDtypeStruct((T, D), F32)]
    out_specs = [tok(D)]
    if emit_vn:
        out_shape.append(jax.ShapeDtypeStruct((T, width), F32))
        out_specs.append(tok(width))
    return pl.pallas_call(
        body,
        out_shape=tuple(out_shape),
        grid=(T // tm,),
        in_specs=[tok(D), _layer_spec((1, D), layer),
                  _layer_spec((D, 2 * width), j), _layer_spec((1, 2 * width), j),
                  _layer_spec((1, width), j),
                  _const_spec(wmix.shape), _const_spec(bmix.shape),
                  _layer_spec((width, D), j)],
        out_specs=tuple(out_specs),
        compiler_params=_cparams(1),
        name="chunk_mlp",
    )(x, norm_g.reshape(norm_g.shape[0], 1, D), w_in, b_in.reshape(b_in.shape[0], 1, 2 * width),
      ln_g.reshape(ln_g.shape[0], 1, width), wmix, bmix, w_out)


def _blockdiag_dense(w):
    rows = w.reshape(-1, MXU_DIM, QKV_BLOCK)
    r = lax.broadcasted_iota(jnp.int32, (MXU_DIM, MXU_DIM), 0)
    c = lax.broadcasted_iota(jnp.int32, (MXU_DIM, MXU_DIM), 1)
    spread = (lax.broadcasted_iota(jnp.int32, (QKV_BLOCK, MXU_DIM), 1) % QKV_BLOCK
              == lax.broadcasted_iota(jnp.int32, (QKV_BLOCK, MXU_DIM), 0)).astype(w.dtype)
    dense = jnp.einsum('grd,dc->grc', rows, spread, precision=lax.Precision.HIGHEST)
    return jnp.where(r // QKV_BLOCK == c // QKV_BLOCK, dense, 0).astype(BF16)


def _gate_weights(w_ig, b_ig, w_fg, b_fg):
    wpad = jnp.zeros((w_ig.shape[0], LANE - N_HEADS), w_ig.dtype)
    bpad = jnp.zeros((LANE - N_HEADS,), b_ig.dtype)
    w = jnp.concatenate([w_ig, wpad, w_fg, wpad], axis=1)
    b = jnp.concatenate([b_ig, bpad, b_fg, bpad])
    return w.astype(BF16), b.reshape(1, 2 * LANE)


def _trunks(xs, grps, p):
    depth = p['norm_ff1'].shape[0]
    sts = [dict(x=x, c_all=None, n=[], m=[], conv=[], v=[]) for x in xs]

    ffn_w = {'cur': tuple(w[0].astype(BF16) for w in p['ffn1_w'])}

    def ffn_round(norm_key, i, nxt, use_mixer=False, final_g=None):
        new_w = None
        for gi, (grp, st) in enumerate(zip(grps, sts)):
            mixer = st['mixer'] if use_mixer else None
            tm = grp['tm'] if mixer is None else grp['tm_mixer']
            cast = nxt if gi == 0 else None
            out = _ffn(st['x'], p[norm_key], ffn_w['cur'], i, tm, final_g=final_g, mixer=mixer,
                       cast=cast)
            if cast is not None:
                out, new_w = out
            st['x'] = out
        ffn_w['cur'] = new_w

    for i in range(depth):
        j = i // 2
        ffn_round('norm_ff1', i, (p['ffn2_w'], i))
        if i % 2 == 0:
            core_in = [None] * len(grps)
            early = [None] * len(grps)
            for gi in sorted(range(len(grps)), key=lambda gi: 'guest' in grps[gi]):
                grp, st = grps[gi], sts[gi]
                guest = None
                if 'guest' in grp:
                    guest = dict(core_in[grp['guest']], seq0=0, nseq=grp['guest_seqs'])
                outs = _ml_pre(
                    st['x'], p['norm_mix'], i, p['ml_w_up'], p['ml_w_conv'], p['ml_b_conv'],
                    p['bq'][j], p['bk'][j], p['bv'][j], p['w_gates'][j], p['b_gates'][j],
                    grp['hist'][j], j, grp['pre_tm'], grp['stride'], grp['tiles_per_seq'],
                    guest=guest)
                q, k, v, xc, z, gates, tail = outs[:N_PRE_OUT]
                if guest is not None:
                    hosted = core_in[grp['guest']]
                    early[grp['guest']] = (grp['guest_seqs'], outs[N_PRE_OUT])
                    hosted.update(seq0=grp['guest_seqs'], nseq=hosted['B'] - grp['guest_seqs'],
                                  c_all=outs[N_PRE_OUT][1])
                st['conv'].append(grp['conv_out'](tail))
                st['xc'], st['z'] = xc, z
                q, k, v, gates = grp['to_core'](q, k, v, gates)
                core_in[gi] = dict(q=q, k=k, v=v, gates=gates, B=grp['B'], nc=grp['nc'],
                                   L=grp['L'], nb=grp['nb'], state=grp['state'](j),
                                   c_all=st['c_all'])
            for gi, (hn, c_all, n, m) in enumerate(_ml_core(core_in, j)):
                grp, st = grps[gi], sts[gi]
                if early[gi] is not None:
                    nseq, (hn0, _, n0, m0) = early[gi]
                    rows = nseq * grp['nc'] * grp['L']
                    hn = jnp.concatenate([hn0[:rows], hn[rows:]], axis=0)
                    n = jnp.concatenate([n0[:nseq], n[nseq:]], axis=0)
                    m = jnp.concatenate([m0[:nseq], m[nseq:]], axis=0)
                st['c_all'] = c_all
                st['n'].append(n)
                st['m'].append(m[:, 0, :N_HEADS])
                st['mixer'] = (grp['from_core'](hn), st['xc'], st['z'], p['ml_hn_g'], p['ml_skip'],
                               p['ml_w_down'], j)
        else:
            for grp, st in zip(grps, sts):
                outs = _chunk_mlp(st['x'], p['norm_mix'], i, p['cm_w_in'], p['cm_b_in'], p['cm_ln_g'],
                                  grp['wmix'][j], grp['bmix'][j], p['cm_w_out'], j, grp['cm_tm'],
                                  emit_vn=grp['emit_vn'])
                st['x'] = outs[0]
                st['v'].append(outs[1] if grp['emit_vn'] else None)
                st['mixer'] = None
        last = i == depth - 1
        ffn_round('norm_ff2', i, None if last else (p['ffn1_w'], i + 1), use_mixer=True,
                  final_g=p['norm_final'] if last else None)
    return [(st['x'], st['c_all'], jnp.stack(st['n']), jnp.stack(st['m']), jnp.stack(st['conv']),
             st['v']) for st in sts]


def kernel(x_prompt, x_sample, state_C, state_n, state_m, state_conv, norm_ff1, norm_mix, norm_ff2, norm_final, ffn1_w_gate, ffn1_w_up, ffn1_w_down, ffn2_w_gate, ffn2_w_up, ffn2_w_down, ml_w_up, ml_w_conv, ml_b_conv, ml_w_q, ml_w_k, ml_w_v, ml_w_ig, ml_b_ig, ml_w_fg, ml_b_fg, ml_hn_g, ml_skip, ml_w_down, cm_w_in, cm_b_in, cm_ln_g, cm_w_s, cm_b_s, cm_w_out):
    bp, sp, d_model = x_prompt.shape
    bs, ss, _ = x_sample.shape
    n_ml, inner = ml_w_conv.shape[0], ml_w_conv.shape[-1]
    n_cm, width = cm_ln_g.shape
    hist_rows = CONV_W - 1
    sb = min(64, bs)

    gate_wb = [_gate_weights(ml_w_ig[j], ml_b_ig[j], ml_w_fg[j], ml_b_fg[j]) for j in range(n_ml)]
    p = {
        'norm_ff1': norm_ff1, 'norm_mix': norm_mix, 'norm_ff2': norm_ff2, 'norm_final': norm_final,
        'ffn1_w': (ffn1_w_gate, ffn1_w_up, ffn1_w_down),
        'ffn2_w': (ffn2_w_gate, ffn2_w_up, ffn2_w_down),
        'ml_w_up': ml_w_up.astype(BF16), 'ml_w_conv': ml_w_conv, 'ml_b_conv': ml_b_conv,
        'bq': [_blockdiag_dense(ml_w_q[j]) for j in range(n_ml)],
        'bk': [_blockdiag_dense(ml_w_k[j]) for j in range(n_ml)],
        'bv': [_blockdiag_dense(ml_w_v[j]) for j in range(n_ml)],
        'w_gates': [wb[0] for wb in gate_wb], 'b_gates': [wb[1] for wb in gate_wb],
        'ml_hn_g': ml_hn_g, 'ml_skip': ml_skip, 'ml_w_down': ml_w_down.astype(BF16),
        'cm_w_in': cm_w_in.astype(BF16), 'cm_b_in': cm_b_in, 'cm_ln_g': cm_ln_g,
        'cm_w_out': cm_w_out.astype(BF16),
    }

    def mix_weights(L, kron_left):
        causal = jnp.tril(jnp.ones((L, L), dtype=bool))
        ws = jnp.where(causal, cm_w_s[:, :, :L, :L], 0)
        bs_ = cm_b_s[:, :, :L]
        if kron_left:
            reps = kron_left
            eye = jnp.eye(reps, dtype=ws.dtype)
            wm = ws[:, :, None, :, None, :] * eye[None, None, :, None, :, None]
            wm = wm.reshape(n_cm, CM_GROUPS, reps * L, reps * L)
            bm = jnp.tile(bs_, (1, 1, reps))
        else:
            eye = jnp.eye(sb, dtype=ws.dtype)
            wm = ws[:, :, :, None, :, None] * eye[None, None, None, :, None, :]
            wm = wm.reshape(n_cm, CM_GROUPS, L * sb, L * sb)
            bm = jnp.repeat(bs_, sb, axis=2)
        bm = jnp.swapaxes(bm, 1, 2)
        bm = jnp.concatenate([bm, jnp.zeros(bm.shape[:2] + (LANE - CM_GROUPS,), bm.dtype)], axis=2)
        return wm.astype(BF16), bm

    tm_p = 1024
    pre_tm_p = 256
    cm_tm_p = 1024
    chunk_p = min(256, sp)
    cm_chunk_p = min(128, sp)
    wmix_p, bmix_p = mix_weights(cm_chunk_p, MXU_DIM // cm_chunk_p)
    grp_p = {
        'tm': tm_p, 'tm_mixer': 512, 'pre_tm': pre_tm_p, 'cm_tm': cm_tm_p, 'emit_vn': False,
        'stride': 1,
        'tiles_per_seq': sp // pre_tm_p,
        'hist': [jnp.zeros((bp, SUBLANE, inner), F32)] * n_ml,
        'conv_out': lambda tail: tail.reshape(bp, sp // pre_tm_p, SUBLANE, inner)[:, -1, SUBLANE - hist_rows:, :],
        'to_core': lambda q, k, v, g: (q, k, v, g),
        'from_core': lambda hn: hn,
        'state': lambda j: None,
        'B': bp, 'nc': sp // chunk_p, 'L': chunk_p, 'nb': 1,
        'guest': 1, 'guest_seqs': bs // 2,
        'wmix': wmix_p, 'bmix': bmix_p,
    }

    ts = bs * ss
    n_sg = bs // sb
    l_pad = SUBLANE

    def tile_major(a):
        steps, feat = a.shape[1], a.shape[2]
        return a.reshape(n_sg, sb, steps, feat).transpose(0, 2, 1, 3).reshape(n_sg * steps * sb, feat)

    def batch_major(a, steps):
        feat = a.shape[-1]
        return a.reshape(n_sg, steps, sb, feat).transpose(0, 2, 1, 3).reshape(bs, steps, feat)

    def to_core(q, k, v, g):
        def seq_major(a, fill):
            a = batch_major(a, ss)
            pad = jnp.broadcast_to(fill.astype(a.dtype), (bs, l_pad - ss, a.shape[-1]))
            return jnp.concatenate([a, pad], axis=1).reshape(bs * l_pad, a.shape[-1])
        zero = jnp.zeros((inner,), F32)
        lane_pad = jnp.zeros((LANE - N_HEADS,), F32)
        gfill = jnp.concatenate([jnp.full((N_HEADS,), NEG, F32), lane_pad,
                                 jnp.full((N_HEADS,), -NEG, F32), lane_pad])
        return seq_major(q, zero), seq_major(k, zero), seq_major(v, zero), seq_major(g, gfill)

    def from_core(hn):
        return tile_major(hn.reshape(bs, l_pad, inner)[:, :ss])

    def sample_state(j):
        m0 = jnp.pad(state_m[j], ((0, 0), (0, LANE - N_HEADS)))
        return state_C, state_n, jnp.broadcast_to(m0[:, None, :], (bs, SUBLANE, LANE))

    wmix_s, bmix_s = mix_weights(ss, 0)
    tile_s = ss * sb
    hist_s = [tile_major(state_conv[j]).reshape(n_sg, hist_rows * sb, inner) for j in range(n_ml)]
    grp_s = {
        'tm': tile_s, 'tm_mixer': tile_s, 'pre_tm': tile_s, 'cm_tm': tile_s, 'emit_vn': True,
        'stride': sb,
        'tiles_per_seq': 1,
        'hist': hist_s,
        'conv_out': lambda tail: batch_major(tail.reshape(n_sg * hist_rows * sb, inner), hist_rows),
        'to_core': to_core, 'from_core': from_core, 'state': sample_state,
        'B': bs, 'nc': 1, 'L': l_pad, 'nb': 1,
        'wmix': wmix_s, 'bmix': bmix_s,
    }
    ((yp, c_p, n_p, m_p, conv_p, _), (ys, c_s, n_s, m_s, conv_s, v_rows)) = _trunks(
        [x_prompt.reshape(bp * sp, d_model), tile_major(x_sample)], [grp_p, grp_s], p)

    y_prompt = yp.reshape(bp, sp, d_model)
    y_sample = batch_major(ys, ss)
    v_sample = jnp.stack([batch_major(vr, ss) for vr in v_rows])
    return (y_prompt, y_sample, c_p, n_p, m_p, conv_p, c_s, n_s, m_s, conv_s, v_sample)
```

```python
import functools
import math

import jax
import jax.numpy as jnp
from jax import lax
from jax.experimental import pallas as pl
from jax.experimental.pallas import tpu as pltpu

F32 = jnp.float32
BF16 = jnp.bfloat16

EPS = 1e-6
N_HEADS = 4
HEAD_DIM = 512
QKV_BLOCK = 4
CONV_W = 4
CM_GROUPS = 4
NEG = -1e30

LANE = 128
SUBLANE = 8
MXU_DIM = 256
VMEM_LIMIT = 56 * 1024 * 1024
VMEM_LIMIT_CORE = 60 * 1024 * 1024


def _cparams(n_axes, vmem_limit=VMEM_LIMIT):
    return pltpu.CompilerParams(dimension_semantics=("arbitrary",) * n_axes,
                                vmem_limit_bytes=vmem_limit)


def _const_spec(shape):
    nd = len(shape)
    return pl.BlockSpec(shape, lambda *_: (0,) * nd, pipeline_mode=pl.Buffered(1))


def _layer_spec(shape, layer):
    nd = len(shape)
    return pl.BlockSpec((None,) + tuple(shape), lambda *_: (layer,) + (0,) * nd,
                        pipeline_mode=pl.Buffered(1))


def _rmsnorm(x, g):
    return x * lax.rsqrt(jnp.mean(x * x, axis=-1, keepdims=True) + EPS) * g


def _silu(x):
    return x * jax.nn.sigmoid(x)


def _dot(a, b):
    return jnp.dot(a, b, preferred_element_type=F32)


def _ffn_body(*refs, ff_chunks, final_norm, mixer_out, n_cast):
    if n_cast:
        cast_in, cast_out = refs[-2 * n_cast - 1:-n_cast - 1], refs[-n_cast:]
        refs = refs[:-2 * n_cast - 1] + (refs[-n_cast - 1],)
        for src, dst in zip(cast_in, cast_out):
            dst[...] = src[...].astype(dst.dtype)
    if mixer_out:
        (x_ref, g_ref, wg_ref, wu_ref, wd_ref, gf_ref,
         hn_ref, xc_ref, z_ref, hg_ref, sk_ref, wdm_ref, o_ref) = refs
    else:
        x_ref, g_ref, wg_ref, wu_ref, wd_ref, gf_ref, o_ref = refs
    x = x_ref[...]
    if mixer_out:
        for lo in range(0, hn_ref.shape[1], HEAD_DIM):
            cols = slice(lo, lo + HEAD_DIM)
            out = ((hn_ref[:, cols].astype(F32) * hg_ref[:, cols]
                    + sk_ref[:, cols] * xc_ref[:, cols].astype(F32))
                   * _silu(z_ref[:, cols].astype(F32)))
            x = x + _dot(out.astype(BF16), wdm_ref[cols, :])
    xn = _rmsnorm(x, g_ref[...]).astype(wg_ref.dtype)
    acc = None
    for lo, hi in ff_chunks:
        gate = _dot(xn, wg_ref[:, lo:hi])
        up = _dot(xn, wu_ref[:, lo:hi])
        h = (_silu(gate) * up).astype(wd_ref.dtype)
        part = _dot(h, wd_ref[lo:hi, :])
        acc = part if acc is None else acc + part
    y = x + 0.5 * acc
    if final_norm:
        y = _rmsnorm(y, gf_ref[...])
    o_ref[...] = y


CAST_BLOCKS = 16


def _ffn(x, norm_g, weights, layer, tm, final_g=None, mixer=None, cast=None):
    T, D = x.shape
    w_gate, w_up, w_down = weights
    d_ff = w_gate.shape[-1]
    n_steps = T // tm
    step = 4 * MXU_DIM
    ff_chunks = tuple((lo, min(lo + step, d_ff)) for lo in range(0, d_ff, step))
    final_norm = final_g is not None
    gf = final_g if final_norm else norm_g[layer]
    n_cast = 0 if cast is None else len(cast[0])
    body = functools.partial(_ffn_body, ff_chunks=ff_chunks, final_norm=final_norm,
                             mixer_out=mixer is not None, n_cast=n_cast)
    tok = lambda w: pl.BlockSpec((tm, w), lambda i: (i, 0))
    in_specs = [tok(D), _layer_spec((1, D), layer), _const_spec((D, d_ff)),
                _const_spec((D, d_ff)), _const_spec((d_ff, D)), _const_spec((1, D))]
    args = [x, norm_g.reshape(norm_g.shape[0], 1, D), w_gate, w_up, w_down, gf.reshape(1, D)]
    if mixer is not None:
        hn, xc, z, gain, skip, w_down_m, j = mixer
        inner = hn.shape[-1]
        in_specs += [tok(inner), tok(inner), tok(inner), _layer_spec((1, inner), j),
                     _layer_spec((1, inner), j), _layer_spec((inner, D), j)]
        args += [hn, xc, z, gain.reshape(gain.shape[0], 1, inner),
                 skip.reshape(skip.shape[0], 1, inner), w_down_m]
    out_shape = [jax.ShapeDtypeStruct((T, D), F32)]
    out_specs = [tok(D)]
    if cast is not None:
        mats, cast_layer = cast
        assert n_steps % CAST_BLOCKS == 0
        per_block = n_steps // CAST_BLOCKS
        for w in mats:
            rows, cols = w.shape[1] // CAST_BLOCKS, w.shape[2]
            in_specs.append(pl.BlockSpec((None, rows, cols), lambda i: (cast_layer, i // per_block, 0)))
            args.append(w)
            out_shape.append(jax.ShapeDtypeStruct(w.shape[1:], BF16))
            out_specs.append(pl.BlockSpec((rows, cols), lambda i: (i // per_block, 0)))
    outs = pl.pallas_call(
        body,
        out_shape=tuple(out_shape),
        grid=(n_steps,),
        in_specs=in_specs,
        out_specs=tuple(out_specs),
        compiler_params=_cparams(1, VMEM_LIMIT if cast is None else VMEM_LIMIT_CORE),
        name="ffn",
    )(*args)
    return outs[0] if cast is None else (outs[0], tuple(outs[1:]))


def _ml_pre_body(x_ref, g_ref, wup_ref, wc_ref, bc_ref, bq_ref, bk_ref, bv_ref, wgt_ref, bgt_ref,
                 hist_ref, q_ref, k_ref, v_ref, xc_ref, z_ref, gates_ref, tail_ref, xbuf,
                 *, tm, inner, header, stride, tiles_per_seq):
    i = pl.program_id(0)

    n_groups = inner // MXU_DIM

    @pl.when(i % tiles_per_seq == 0)
    def _():
        for g in range(n_groups):
            xbuf[g, 0:header, :] = hist_ref[:, g * MXU_DIM:(g + 1) * MXU_DIM]

    xn = _rmsnorm(x_ref[...], g_ref[...]).astype(BF16)

    def up_proj(g):
        cols = slice(g * MXU_DIM, (g + 1) * MXU_DIM)
        zcols = slice(inner + g * MXU_DIM, inner + (g + 1) * MXU_DIM)
        xm_g = _dot(xn, wup_ref[:, cols])
        z_ref[:, cols] = _dot(xn, wup_ref[:, zcols]).astype(BF16)
        return xm_g

    def gate_part(g, xcb, xmb):
        rows = lambda base: slice(base + g * MXU_DIM, base + (g + 1) * MXU_DIM)
        return _dot(xcb, wgt_ref[rows(0), :]) + _dot(xmb, wgt_ref[rows(inner), :])

    gacc = bgt_ref[...]
    prev_act = None
    xm_next = up_proj(0)
    for g in range(n_groups):
        cols = slice(g * MXU_DIM, (g + 1) * MXU_DIM)
        xm = xm_next
        if g + 1 < n_groups:
            xm_next = up_proj(g + 1)
        xbuf[g, header:header + tm, :] = xm
        acc = bc_ref[:, cols] + wc_ref[CONV_W - 1:CONV_W, cols] * xm
        for d in range(1, CONV_W):
            lo = header - d * stride
            acc = acc + wc_ref[CONV_W - 1 - d:CONV_W - d, cols] * xbuf[g, lo:lo + tm, :]
        xcb = _silu(acc).astype(BF16)
        xc_ref[:, cols] = xcb
        xmb = xm.astype(BF16)
        q_ref[:, cols] = _dot(xcb, bq_ref[g]).astype(BF16)
        k_ref[:, cols] = _dot(xcb, bk_ref[g]).astype(BF16)
        v_ref[:, cols] = _dot(xmb, bv_ref[g]).astype(BF16)
        if g > 0:
            gacc = gacc + gate_part(g - 1, *prev_act)
        prev_act = (xcb, xmb)
        last = xbuf[g, tm:tm + header, :]
        tail_ref[:, cols] = last
        xbuf[g, 0:header, :] = last
    gates_ref[...] = gacc + gate_part(n_groups - 1, *prev_act)


def _ml_pre(x, norm_g, layer, w_up, w_conv, b_conv, bq, bk, bv, w_gates, b_gates, hist, j,
            tm, stride, tiles_per_seq):
    T, D = x.shape
    inner = w_conv.shape[-1]
    header = hist.shape[1]
    n_tiles = T // tm
    tok = lambda w: pl.BlockSpec((tm, w), lambda i: (i, 0))
    in_specs = [
        tok(D),
        _layer_spec((1, D), layer),
        _layer_spec((D, 2 * inner), j),
        _layer_spec((CONV_W, inner), j),
        _layer_spec((1, inner), j),
        _const_spec(bq.shape), _const_spec(bk.shape), _const_spec(bv.shape),
        _const_spec(w_gates.shape), _const_spec(b_gates.shape),
        pl.BlockSpec((None, header, inner), lambda i: (i // tiles_per_seq, 0, 0)),
    ]
    args = [x, norm_g.reshape(norm_g.shape[0], 1, D), w_up, w_conv,
            b_conv.reshape(b_conv.shape[0], 1, inner), bq, bk, bv, w_gates, b_gates, hist]
    out_shape = [
        jax.ShapeDtypeStruct((T, inner), BF16),
        jax.ShapeDtypeStruct((T, inner), BF16),
        jax.ShapeDtypeStruct((T, inner), BF16),
        jax.ShapeDtypeStruct((T, inner), BF16),
        jax.ShapeDtypeStruct((T, inner), BF16),
        jax.ShapeDtypeStruct((T, 2 * LANE), F32),
        jax.ShapeDtypeStruct((n_tiles, header, inner), F32),
    ]
    out_specs = [tok(inner), tok(inner), tok(inner), tok(inner), tok(inner), tok(2 * LANE),
                 pl.BlockSpec((None, header, inner), lambda i: (i, 0, 0))]
    body = functools.partial(_ml_pre_body, tm=tm, inner=inner, header=header, stride=stride,
                             tiles_per_seq=tiles_per_seq)
    return pl.pallas_call(
        body,
        out_shape=tuple(out_shape),
        grid=(n_tiles,),
        in_specs=in_specs,
        out_specs=tuple(out_specs),
        scratch_shapes=[pltpu.VMEM((inner // MXU_DIM, header + tm, MXU_DIM), F32)],
        compiler_params=_cparams(1),
        name="ml_pre",
    )(*args)


def _log_sigmoid(x):
    return jnp.minimum(x, 0.0) - jnp.log(1.0 + jnp.exp(-jnp.abs(x)))


def _token_scan(x, op, fill):
    n = x.shape[0]
    tok = lax.broadcasted_iota(jnp.int32, x.shape, 0)
    d = 1
    while d < n:
        x = op(x, jnp.where(tok >= d, pltpu.roll(x, d, axis=0), fill))
        d *= 2
    return x


def _gate_vectors(gates, m_prev, L):
    ig = gates[:, :LANE]
    bt = _token_scan(_log_sigmoid(gates[:, LANE:]), jnp.add, 0.0)
    a = ig - bt
    m = bt + jnp.maximum(_token_scan(a, jnp.maximum, NEG), m_prev)
    dcol = bt - m
    bl, ml = bt[L - 1:L, :], m[L - 1:L, :]
    return dict(a=a, dcol=dcol, w_inter=jnp.exp(m_prev + dcol), exp_neg_m=jnp.exp(-m),
                gs=jnp.exp(bl - bt + ig - ml) * HEAD_DIM ** -0.5,
                g_inter=jnp.exp(m_prev + bl - ml), ml=ml)


def _ml_core_body(*refs, cfgs):
    n_in = [8 if has_state else 5 for (_, _, _, has_state) in cfgs]
    pos_out = sum(n_in)
    pos_in = 0
    for (L, nb, nc, has_state), k_in in zip(cfgs, n_in):
        _ml_core_group(refs[pos_in:pos_in + k_in], refs[pos_out:pos_out + 4], L, nb, nc, has_state)
        pos_in += k_in
        pos_out += 4


def _ml_core_group(in_refs, out_refs, L, nb, nc, has_state):
    h_ref, c_ref, n_ref, m_ref = out_refs
    if has_state:
        q_ref, k_ref, v_ref, gates_ref, c0_ref, n0_ref, m0_ref, _ = in_refs
    else:
        q_ref, k_ref, v_ref, gates_ref, _ = in_refs
    if has_state and nc == 1:
        c_prev, n_prev_ref, m_prev_ref = c0_ref, n0_ref, m0_ref
    else:
        c_prev, n_prev_ref, m_prev_ref = c_ref, n_ref, m_ref

        @pl.when(pl.program_id(0) % nc == 0)
        def _():
            if has_state:
                c_ref[...] = c0_ref[...]
                n_ref[...] = n0_ref[...]
                m_ref[...] = m0_ref[...]
            else:
                c_ref[...] = jnp.zeros_like(c_ref)
                n_ref[...] = jnp.zeros_like(n_ref)
                m_ref[...] = jnp.zeros_like(m_ref)

    log_scale = -0.5 * math.log(HEAD_DIM)
    row = lax.broadcasted_iota(jnp.int32, (L, L), 0)
    col = lax.broadcasted_iota(jnp.int32, (L, L), 1)
    causal = col <= row
    heads = range(N_HEADS)
    seqs = range(nb)
    hcols = [slice(h * HEAD_DIM, (h + 1) * HEAD_DIM) for h in heads]
    ones_rows = jnp.ones((SUBLANE, L), BF16)
    n_prev = [n_prev_ref[s] for s in seqs]


    qc = [[_dot(q_ref[s, :, hcols[h]], c_prev[s, h].astype(BF16)) for h in heads] for s in seqs]

    gvs, a_rows = [], []
    for s in seqs:
        gv = _gate_vectors(gates_ref[s], m_prev_ref[s, 0:1, :], L)
        m_ref[s] = jnp.broadcast_to(gv['ml'], m_ref.shape[1:])
        a_shift = gv['a'] + log_scale
        if L % LANE == 0:
            a_t = a_shift.T
            a_rows.append([a_t[h:h + 1, :] for h in heads])
        else:
            a_rows.append([jnp.sum(jnp.where(row == col, a_shift[:, h:h + 1], 0.0), axis=0,
                                   keepdims=True) for h in heads])
        gvs.append(gv)

    for s in seqs:
        hcol = lambda name, h: gvs[s][name][:, h:h + 1]
        for h in heads:
            qh, kh, vh = q_ref[s, :, hcols[h]], k_ref[s, :, hcols[h]], v_ref[s, :, hcols[h]]

            g_inter = hcol('g_inter', h)
            kg = kh * hcol('gs', h).astype(BF16)
            upd = lax.dot_general(kg, vh, (((0,), (0,)), ((), ())), preferred_element_type=F32)
            c_ref[s, h] = g_inter * c_prev[s, h] + upd
            n_ref[s, h:h + 1, :] = (g_inter * n_prev[s][h:h + 1, :]
                                    + _dot(ones_rows, kg)[0:1, :])

            w_inter = hcol('w_inter', h)
            w_intra = jnp.exp(jnp.where(causal, hcol('dcol', h) + a_rows[s][h], NEG))
            sc = lax.dot_general(qh, kh, (((1,), (1,)), ((), ())),
                                 preferred_element_type=F32) * w_intra
            num = _dot(sc.astype(BF16), vh) + qc[s][h] * w_inter
            qn = jnp.sum((qh * n_prev[s][h:h + 1, :].astype(BF16)).astype(F32), axis=1,
                         keepdims=True)
            nq = jnp.sum(sc, axis=1, keepdims=True) + w_inter * qn
            den = jnp.maximum(jnp.abs(nq), hcol('exp_neg_m', h))
            mu = jnp.mean(num, axis=-1, keepdims=True)
            hc = num - mu
            var = jnp.mean(hc * hc, axis=-1, keepdims=True)
            h_ref[s, :, hcols[h]] = (hc * lax.rsqrt(var + EPS * den * den)).astype(BF16)


def _ml_core(groups, j):
    ios = [_core_group_io(g, j) for g in groups]
    steps = {io['steps'] for io in ios}
    assert len(steps) == 1, "token groups must have the same number of grid steps"
    in_specs, args, out_specs, out_shape, aliases = [], [], [], [], {}
    for gi, io in enumerate(ios):
        if io['alias_in'] is not None:
            aliases[len(args) + io['alias_in']] = N_CORE_OUT * gi + 1
        in_specs += io['in_specs']
        args += io['args']
        out_specs += io['out_specs']
        out_shape += io['out_shape']
    outs = pl.pallas_call(
        functools.partial(_ml_core_body, cfgs=tuple(io['cfg'] for io in ios)),
        out_shape=tuple(out_shape),
        grid=(steps.pop(),),
        in_specs=in_specs,
        out_specs=tuple(out_specs),
        input_output_aliases=aliases,
        compiler_params=_cparams(1, VMEM_LIMIT_CORE),
        name="ml_core",
    )(*args)
    return [_core_group_result(g, outs[N_CORE_OUT * gi:N_CORE_OUT * (gi + 1)])
            for gi, g in enumerate(groups)]


N_CORE_OUT = 4


def _core_group_io(g, j):
    n_ml = 2
    B, nc, L, nb = g['B'], g['nc'], g['L'], g['nb']
    inner = g['q'].shape[-1]
    has_state = g['state'] is not None
    seq3 = lambda a: a.reshape(B, nc * L, a.shape[-1])
    tok = lambda w: pl.BlockSpec((nb, L, w), lambda t: (t // nc, t % nc, 0))
    per_seq = lambda shape: pl.BlockSpec((nb,) + shape, lambda t: (t // nc,) + (0,) * len(shape))
    per_seq_layer = lambda shape: pl.BlockSpec(
        (None, nb) + shape, lambda t: (j, t // nc) + (0,) * len(shape))
    c_block = per_seq_layer((N_HEADS, HEAD_DIM, HEAD_DIM))
    in_specs = [tok(inner), tok(inner), tok(inner), tok(2 * LANE)]
    args = [seq3(g['q']), seq3(g['k']), seq3(g['v']), seq3(g['gates'])]
    if has_state:
        in_specs += [c_block, per_seq_layer((N_HEADS, HEAD_DIM)), per_seq((SUBLANE, LANE))]
        args += list(g['state'])
    in_specs.append(pl.BlockSpec(memory_space=pl.ANY))
    alias_in = None
    if g['c_all'] is not None:
        args.append(g['c_all'])
        alias_in = len(args) - 1
    else:
        args.append(jnp.zeros((SUBLANE, LANE), F32))
    out_shape = [jax.ShapeDtypeStruct((B, nc * L, inner), BF16),
                 jax.ShapeDtypeStruct((n_ml, B, N_HEADS, HEAD_DIM, HEAD_DIM), F32),
                 jax.ShapeDtypeStruct((B, N_HEADS, HEAD_DIM), F32),
                 jax.ShapeDtypeStruct((B, SUBLANE, LANE), F32)]
    out_specs = [tok(inner), c_block, per_seq((N_HEADS, HEAD_DIM)), per_seq((SUBLANE, LANE))]
    return dict(in_specs=in_specs, args=args, out_specs=out_specs, out_shape=out_shape,
                cfg=(L, nb, nc, has_state), alias_in=alias_in, steps=B // nb * nc)


def _core_group_result(g, outs):
    hn, c_all, n, m = outs
    return hn.reshape(g['q'].shape), c_all, n, m


def _gelu(x):
    return 0.5 * x * (1.0 + lax.erf(x * (2.0 ** -0.5)))


def _cm_body(*refs, width, sub, emit_vn):
    if emit_vn:
        x_ref, g_ref, win_ref, bin_ref, lg_ref, wmix_ref, bmix_ref, wout_ref, o_ref, vn_ref = refs
    else:
        x_ref, g_ref, win_ref, bin_ref, lg_ref, wmix_ref, bmix_ref, wout_ref, o_ref = refs
    gd = width // CM_GROUPS
    n_sub = x_ref.shape[0] // sub

    def in_proj(s):
        x = x_ref[s * sub:(s + 1) * sub, :]
        xn = _rmsnorm(x, g_ref[...]).astype(BF16)
        v = _gelu(_dot(xn, win_ref[:, width:]) + bin_ref[:, width:])
        u = _gelu(_dot(xn, win_ref[:, :width]) + bin_ref[:, :width])
        return x, u, v

    def gate_and_out(s, x, u, v):
        rows = slice(s * sub, (s + 1) * sub)
        mu = jnp.mean(v, axis=-1, keepdims=True)
        vc = v - mu
        var = jnp.mean(vc * vc, axis=-1, keepdims=True)
        vn = vc * lax.rsqrt(var + EPS) * lg_ref[...]
        if emit_vn:
            vn_ref[rows, :] = vn
        vnb = vn.astype(BF16)
        bmix = bmix_ref[...]
        parts = []
        for g in range(CM_GROUPS):
            cols = slice(g * gd, (g + 1) * gd)
            mix = _dot(wmix_ref[g], vnb[:, cols]) + bmix[:, g:g + 1]
            parts.append((u[:, cols] * mix).astype(BF16))
        o_ref[rows, :] = x + _dot(jnp.concatenate(parts, axis=-1), wout_ref[...])

    nxt = in_proj(0)
    for s in range(n_sub):
        cur = nxt
        if s + 1 < n_sub:
            nxt = in_proj(s + 1)
        gate_and_out(s, *cur)


def _chunk_mlp(x, norm_g, layer, w_in, b_in, ln_g, wmix, bmix, w_out, j, tm, emit_vn):
    T, D = x.shape
    width = ln_g.shape[-1]
    body = functools.partial(_cm_body, width=width, sub=wmix.shape[1], emit_vn=emit_vn)
    tok = lambda w: pl.BlockSpec((tm, w), lambda i: (i, 0))
    out_shape = [jax.ShapeDtypeStruct((T, D), F32)]
    out_specs = [tok(D)]
    if emit_vn:
        out_shape.append(jax.ShapeDtypeStruct((T, width), F32))
        out_specs.append(tok(width))
    return pl.pallas_call(
        body,
        out_shape=tuple(out_shape),
        grid=(T // tm,),
        in_specs=[tok(D), _layer_spec((1, D), layer),
                  _layer_spec((D, 2 * width), j), _layer_spec((1, 2 * width), j),
                  _layer_spec((1, width), j),
                  _const_spec(wmix.shape), _const_spec(bmix.shape),
                  _layer_spec((width, D), j)],
        out_specs=tuple(out_specs),
        compiler_params=_cparams(1),
        name="chunk_mlp",
    )(x, norm_g.reshape(norm_g.shape[0], 1, D), w_in, b_in.reshape(b_in.shape[0], 1, 2 * width),
      ln_g.reshape(ln_g.shape[0], 1, width), wmix, bmix, w_out)


def _blockdiag_dense(w):
    rows = w.reshape(-1, MXU_DIM, QKV_BLOCK)
    r = lax.broadcasted_iota(jnp.int32, (MXU_DIM, MXU_DIM), 0)
    c = lax.broadcasted_iota(jnp.int32, (MXU_DIM, MXU_DIM), 1)
    spread = (lax.broadcasted_iota(jnp.int32, (QKV_BLOCK, MXU_DIM), 1) % QKV_BLOCK
              == lax.broadcasted_iota(jnp.int32, (QKV_BLOCK, MXU_DIM), 0)).astype(w.dtype)
    dense = jnp.einsum('grd,dc->grc', rows, spread, precision=lax.Precision.HIGHEST)
    return jnp.where(r // QKV_BLOCK == c // QKV_BLOCK, dense, 0).astype(BF16)


def _gate_weights(w_ig, b_ig, w_fg, b_fg, w_q, w_k, w_v):
    inner = w_q.shape[0] * QKV_BLOCK

    def fold(blocks, w_rows):
        wr = w_rows.reshape(-1, QKV_BLOCK, w_rows.shape[-1])
        return jnp.einsum('ncd,ndh->nch', blocks, wr,
                          precision=lax.Precision.HIGHEST).reshape(inner, w_rows.shape[-1])

    def both(w):
        on_xc = fold(w_q, w[:inner]) + fold(w_k, w[inner:2 * inner])
        return jnp.concatenate([on_xc, fold(w_v, w[2 * inner:])], axis=0)

    wpad = jnp.zeros((2 * inner, LANE - N_HEADS), w_ig.dtype)
    bpad = jnp.zeros((LANE - N_HEADS,), b_ig.dtype)
    w = jnp.concatenate([both(w_ig), wpad, both(w_fg), wpad], axis=1)
    b = jnp.concatenate([b_ig, bpad, b_fg, bpad])
    return w.astype(BF16), b.reshape(1, 2 * LANE)


def _trunks(xs, grps, p):
    depth = p['norm_ff1'].shape[0]
    sts = [dict(x=x, c_all=None, n=[], m=[], conv=[], v=[]) for x in xs]

    ffn_w = {'cur': tuple(w[0].astype(BF16) for w in p['ffn1_w'])}

    def ffn_round(norm_key, i, nxt, use_mixer=False, final_g=None):
        new_w = None
        for gi, (grp, st) in enumerate(zip(grps, sts)):
            mixer = st['mixer'] if use_mixer else None
            tm = grp['tm'] if mixer is None else grp['tm_mixer']
            cast = nxt if gi == 0 else None
            out = _ffn(st['x'], p[norm_key], ffn_w['cur'], i, tm, final_g=final_g, mixer=mixer,
                       cast=cast)
            if cast is not None:
                out, new_w = out
            st['x'] = out
        ffn_w['cur'] = new_w

    for i in range(depth):
        j = i // 2
        ffn_round('norm_ff1', i, (p['ffn2_w'], i))
        if i % 2 == 0:
            core_in = []
            for grp, st in zip(grps, sts):
                q, k, v, xc, z, gates, tail = _ml_pre(
                    st['x'], p['norm_mix'], i, p['ml_w_up'], p['ml_w_conv'], p['ml_b_conv'],
                    p['bq'][j], p['bk'][j], p['bv'][j], p['w_gates'][j], p['b_gates'][j],
                    grp['hist'][j], j, grp['pre_tm'], grp['stride'], grp['tiles_per_seq'])
                st['conv'].append(grp['conv_out'](tail))
                st['xc'], st['z'] = xc, z
                q, k, v, gates = grp['to_core'](q, k, v, gates)
                core_in.append(dict(q=q, k=k, v=v, gates=gates, B=grp['B'], nc=grp['nc'], L=grp['L'],
                                    nb=grp['nb'], state=grp['state'](j), c_all=st['c_all']))
            for grp, st, (hn, c_all, n, m) in zip(grps, sts, _ml_core(core_in, j)):
                st['c_all'] = c_all
                st['n'].append(n)
                st['m'].append(m[:, 0, :N_HEADS])
                st['mixer'] = (grp['from_core'](hn), st['xc'], st['z'], p['ml_hn_g'], p['ml_skip'],
                               p['ml_w_down'], j)
        else:
            for grp, st in zip(grps, sts):
                outs = _chunk_mlp(st['x'], p['norm_mix'], i, p['cm_w_in'], p['cm_b_in'], p['cm_ln_g'],
                                  grp['wmix'][j], grp['bmix'][j], p['cm_w_out'], j, grp['cm_tm'],
                                  emit_vn=grp['emit_vn'])
                st['x'] = outs[0]
                st['v'].append(outs[1] if grp['emit_vn'] else None)
                st['mixer'] = None
        last = i == depth - 1
        ffn_round('norm_ff2', i, None if last else (p['ffn1_w'], i + 1), use_mixer=True,
                  final_g=p['norm_final'] if last else None)
    return [(st['x'], st['c_all'], jnp.stack(st['n']), jnp.stack(st['m']), jnp.stack(st['conv']),
             st['v']) for st in sts]


def kernel(x_prompt, x_sample, state_C, state_n, state_m, state_conv, norm_ff1, norm_mix, norm_ff2, norm_final, ffn1_w_gate, ffn1_w_up, ffn1_w_down, ffn2_w_gate, ffn2_w_up, ffn2_w_down, ml_w_up, ml_w_conv, ml_b_conv, ml_w_q, ml_w_k, ml_w_v, ml_w_ig, ml_b_ig, ml_w_fg, ml_b_fg, ml_hn_g, ml_skip, ml_w_down, cm_w_in, cm_b_in, cm_ln_g, cm_w_s, cm_b_s, cm_w_out):
    bp, sp, d_model = x_prompt.shape
    bs, ss, _ = x_sample.shape
    n_ml, inner = ml_w_conv.shape[0], ml_w_conv.shape[-1]
    n_cm, width = cm_ln_g.shape
    hist_rows = CONV_W - 1
    sb = min(64, bs)

    gate_wb = [_gate_weights(ml_w_ig[j], ml_b_ig[j], ml_w_fg[j], ml_b_fg[j],
                             ml_w_q[j], ml_w_k[j], ml_w_v[j]) for j in range(n_ml)]
    p = {
        'norm_ff1': norm_ff1, 'norm_mix': norm_mix, 'norm_ff2': norm_ff2, 'norm_final': norm_final,
        'ffn1_w': (ffn1_w_gate, ffn1_w_up, ffn1_w_down),
        'ffn2_w': (ffn2_w_gate, ffn2_w_up, ffn2_w_down),
        'ml_w_up': ml_w_up.astype(BF16), 'ml_w_conv': ml_w_conv, 'ml_b_conv': ml_b_conv,
        'bq': [_blockdiag_dense(ml_w_q[j]) for j in range(n_ml)],
        'bk': [_blockdiag_dense(ml_w_k[j]) for j in range(n_ml)],
        'bv': [_blockdiag_dense(ml_w_v[j]) for j in range(n_ml)],
        'w_gates': [wb[0] for wb in gate_wb], 'b_gates': [wb[1] for wb in gate_wb],
        'ml_hn_g': ml_hn_g, 'ml_skip': ml_skip, 'ml_w_down': ml_w_down.astype(BF16),
        'cm_w_in': cm_w_in.astype(BF16), 'cm_b_in': cm_b_in, 'cm_ln_g': cm_ln_g,
        'cm_w_out': cm_w_out.astype(BF16),
    }

    def mix_weights(L, kron_left):
        causal = jnp.tril(jnp.ones((L, L), dtype=bool))
        ws = jnp.where(causal, cm_w_s[:, :, :L, :L], 0)
        bs_ = cm_b_s[:, :, :L]
        if kron_left:
            reps = kron_left
            eye = jnp.eye(reps, dtype=ws.dtype)
            wm = ws[:, :, None, :, None, :] * eye[None, None, :, None, :, None]
            wm = wm.reshape(n_cm, CM_GROUPS, reps * L, reps * L)
            bm = jnp.tile(bs_, (1, 1, reps))
        else:
            eye = jnp.eye(sb, dtype=ws.dtype)
            wm = ws[:, :, :, None, :, None] * eye[None, None, None, :, None, :]
            wm = wm.reshape(n_cm, CM_GROUPS, L * sb, L * sb)
            bm = jnp.repeat(bs_, sb, axis=2)
        bm = jnp.swapaxes(bm, 1, 2)
        bm = jnp.concatenate([bm, jnp.zeros(bm.shape[:2] + (LANE - CM_GROUPS,), bm.dtype)], axis=2)
        return wm.astype(BF16), bm

    tm_p = 1024
    pre_tm_p = 512
    cm_tm_p = 1024
    chunk_p = min(256, sp)
    cm_chunk_p = min(128, sp)
    wmix_p, bmix_p = mix_weights(cm_chunk_p, MXU_DIM // cm_chunk_p)
    grp_p = {
        'tm': tm_p, 'tm_mixer': 512, 'pre_tm': pre_tm_p, 'cm_tm': cm_tm_p, 'emit_vn': False,
        'stride': 1,
        'tiles_per_seq': sp // pre_tm_p,
        'hist': [jnp.zeros((bp, SUBLANE, inner), F32)] * n_ml,
        'conv_out': lambda tail: tail.reshape(bp, sp // pre_tm_p, SUBLANE, inner)[:, -1, SUBLANE - hist_rows:, :],
        'to_core': lambda q, k, v, g: (q, k, v, g),
        'from_core': lambda hn: hn,
        'state': lambda j: None,
        'B': bp, 'nc': sp // chunk_p, 'L': chunk_p, 'nb': 1,
        'wmix': wmix_p, 'bmix': bmix_p,
    }

    ts = bs * ss
    n_sg = bs // sb
    l_pad = SUBLANE

    def tile_major(a):
        steps, feat = a.shape[1], a.shape[2]
        return a.reshape(n_sg, sb, steps, feat).transpose(0, 2, 1, 3).reshape(n_sg * steps * sb, feat)

    def batch_major(a, steps):
        feat = a.shape[-1]
        return a.reshape(n_sg, steps, sb, feat).transpose(0, 2, 1, 3).reshape(bs, steps, feat)

    def to_core(q, k, v, g):
        def seq_major(a, fill):
            a = batch_major(a, ss)
            pad = jnp.broadcast_to(fill.astype(a.dtype), (bs, l_pad - ss, a.shape[-1]))
            return jnp.concatenate([a, pad], axis=1).reshape(bs * l_pad, a.shape[-1])
        zero = jnp.zeros((inner,), F32)
        lane_pad = jnp.zeros((LANE - N_HEADS,), F32)
        gfill = jnp.concatenate([jnp.full((N_HEADS,), NEG, F32), lane_pad,
                                 jnp.full((N_HEADS,), -NEG, F32), lane_pad])
        return seq_major(q, zero), seq_major(k, zero), seq_major(v, zero), seq_major(g, gfill)

    def from_core(hn):
        return tile_major(hn.reshape(bs, l_pad, inner)[:, :ss])

    def sample_state(j):
        m0 = jnp.pad(state_m[j], ((0, 0), (0, LANE - N_HEADS)))
        return state_C, state_n, jnp.broadcast_to(m0[:, None, :], (bs, SUBLANE, LANE))

    wmix_s, bmix_s = mix_weights(ss, 0)
    tile_s = ss * sb
    hist_s = [tile_major(state_conv[j]).reshape(n_sg, hist_rows * sb, inner) for j in range(n_ml)]
    grp_s = {
        'tm': tile_s, 'tm_mixer': tile_s, 'pre_tm': tile_s, 'cm_tm': tile_s, 'emit_vn': True,
        'stride': sb,
        'tiles_per_seq': 1,
        'hist': hist_s,
        'conv_out': lambda tail: batch_major(tail.reshape(n_sg * hist_rows * sb, inner), hist_rows),
        'to_core': to_core, 'from_core': from_core, 'state': sample_state,
        'B': bs, 'nc': 1, 'L': l_pad, 'nb': 2,
        'wmix': wmix_s, 'bmix': bmix_s,
    }
    ((yp, c_p, n_p, m_p, conv_p, _), (ys, c_s, n_s, m_s, conv_s, v_rows)) = _trunks(
        [x_prompt.reshape(bp * sp, d_model), tile_major(x_sample)], [grp_p, grp_s], p)

    y_prompt = yp.reshape(bp, sp, d_model)
    y_sample = batch_major(ys, ss)
    v_sample = jnp.stack([batch_major(vr, ss) for vr in v_rows])
    return (y_prompt, y_sample, c_p, n_p, m_p, conv_p, c_s, n_s, m_s, conv_s, v_sample)
```

```python
import functools
import math

import jax
import jax.numpy as jnp
from jax import lax
from jax.experimental import pallas as pl
from jax.experimental.pallas import tpu as pltpu

F32 = jnp.float32
BF16 = jnp.bfloat16

EPS = 1e-6
N_HEADS = 4
HEAD_DIM = 512
QKV_BLOCK = 4
CONV_W = 4
CM_GROUPS = 4
NEG = -1e30

LANE = 128
SUBLANE = 8
MXU_DIM = 256
VMEM_LIMIT = 56 * 1024 * 1024
VMEM_LIMIT_CORE = 60 * 1024 * 1024


def _cparams(n_axes, vmem_limit=VMEM_LIMIT):
    return pltpu.CompilerParams(dimension_semantics=("arbitrary",) * n_axes,
                                vmem_limit_bytes=vmem_limit)


def _const_spec(shape):
    nd = len(shape)
    return pl.BlockSpec(shape, lambda *_: (0,) * nd, pipeline_mode=pl.Buffered(1))


def _layer_spec(shape, layer):
    nd = len(shape)
    return pl.BlockSpec((None,) + tuple(shape), lambda *_: (layer,) + (0,) * nd,
                        pipeline_mode=pl.Buffered(1))


def _rmsnorm(x, g):
    return x * lax.rsqrt(jnp.mean(x * x, axis=-1, keepdims=True) + EPS) * g


def _silu(x):
    return x * jax.nn.sigmoid(x)


def _dot(a, b):
    return jnp.dot(a, b, preferred_element_type=F32)


def _ffn_body(*refs, ff_chunks, final_norm, mixer_out, n_cast):
    if n_cast:
        cast_in, cast_out = refs[-2 * n_cast - 1:-n_cast - 1], refs[-n_cast:]
        refs = refs[:-2 * n_cast - 1] + (refs[-n_cast - 1],)
        for src, dst in zip(cast_in, cast_out):
            dst[...] = src[...].astype(dst.dtype)
    if mixer_out:
        (x_ref, g_ref, wg_ref, wu_ref, wd_ref, gf_ref,
         hn_ref, xc_ref, z_ref, hg_ref, sk_ref, wdm_ref, o_ref) = refs
    else:
        x_ref, g_ref, wg_ref, wu_ref, wd_ref, gf_ref, o_ref = refs
    x = x_ref[...]
    if mixer_out:
        for lo in range(0, hn_ref.shape[1], HEAD_DIM):
            cols = slice(lo, lo + HEAD_DIM)
            out = ((hn_ref[:, cols].astype(F32) * hg_ref[:, cols]
                    + sk_ref[:, cols] * xc_ref[:, cols].astype(F32))
                   * _silu(z_ref[:, cols].astype(F32)))
            x = x + _dot(out.astype(BF16), wdm_ref[cols, :])
    xn = _rmsnorm(x, g_ref[...]).astype(wg_ref.dtype)
    acc = None
    for lo, hi in ff_chunks:
        gate = _dot(xn, wg_ref[:, lo:hi])
        up = _dot(xn, wu_ref[:, lo:hi])
        h = (_silu(gate) * up).astype(wd_ref.dtype)
        part = _dot(h, wd_ref[lo:hi, :])
        acc = part if acc is None else acc + part
    y = x + 0.5 * acc
    if final_norm:
        y = _rmsnorm(y, gf_ref[...])
    o_ref[...] = y


CAST_BLOCKS = 16


def _ffn(x, norm_g, weights, layer, tm, final_g=None, mixer=None, cast=None):
    T, D = x.shape
    w_gate, w_up, w_down = weights
    d_ff = w_gate.shape[-1]
    n_steps = T // tm
    step = 4 * MXU_DIM
    ff_chunks = tuple((lo, min(lo + step, d_ff)) for lo in range(0, d_ff, step))
    final_norm = final_g is not None
    gf = final_g if final_norm else norm_g[layer]
    n_cast = 0 if cast is None else len(cast[0])
    body = functools.partial(_ffn_body, ff_chunks=ff_chunks, final_norm=final_norm,
                             mixer_out=mixer is not None, n_cast=n_cast)
    tok = lambda w: pl.BlockSpec((tm, w), lambda i: (i, 0))
    in_specs = [tok(D), _layer_spec((1, D), layer), _const_spec((D, d_ff)),
                _const_spec((D, d_ff)), _const_spec((d_ff, D)), _const_spec((1, D))]
    args = [x, norm_g.reshape(norm_g.shape[0], 1, D), w_gate, w_up, w_down, gf.reshape(1, D)]
    if mixer is not None:
        hn, xc, z, gain, skip, w_down_m, j = mixer
        inner = hn.shape[-1]
        in_specs += [tok(inner), tok(inner), tok(inner), _layer_spec((1, inner), j),
                     _layer_spec((1, inner), j), _layer_spec((inner, D), j)]
        args += [hn, xc, z, gain.reshape(gain.shape[0], 1, inner),
                 skip.reshape(skip.shape[0], 1, inner), w_down_m]
    out_shape = [jax.ShapeDtypeStruct((T, D), F32)]
    out_specs = [tok(D)]
    if cast is not None:
        mats, cast_layer = cast
        assert n_steps % CAST_BLOCKS == 0
        per_block = n_steps // CAST_BLOCKS
        for w in mats:
            rows, cols = w.shape[1] // CAST_BLOCKS, w.shape[2]
            in_specs.append(pl.BlockSpec((None, rows, cols), lambda i: (cast_layer, i // per_block, 0)))
            args.append(w)
            out_shape.append(jax.ShapeDtypeStruct(w.shape[1:], BF16))
            out_specs.append(pl.BlockSpec((rows, cols), lambda i: (i // per_block, 0)))
    outs = pl.pallas_call(
        body,
        out_shape=tuple(out_shape),
        grid=(n_steps,),
        in_specs=in_specs,
        out_specs=tuple(out_specs),
        compiler_params=_cparams(1, VMEM_LIMIT if cast is None else VMEM_LIMIT_CORE),
        name="ffn",
    )(*args)
    return outs[0] if cast is None else (outs[0], tuple(outs[1:]))


def _ml_pre_body(x_ref, g_ref, wup_ref, wc_ref, bc_ref, bq_ref, bk_ref, bv_ref, wgt_ref, bgt_ref,
                 hist_ref, q_ref, k_ref, v_ref, xc_ref, z_ref, gates_ref, tail_ref, xbuf,
                 *, tm, inner, header, stride, tiles_per_seq):
    i = pl.program_id(0)

    n_groups = inner // MXU_DIM

    @pl.when(i % tiles_per_seq == 0)
    def _():
        for g in range(n_groups):
            xbuf[g, 0:header, :] = hist_ref[:, g * MXU_DIM:(g + 1) * MXU_DIM]

    xn = _rmsnorm(x_ref[...], g_ref[...]).astype(BF16)

    def up_proj(g):
        cols = slice(g * MXU_DIM, (g + 1) * MXU_DIM)
        zcols = slice(inner + g * MXU_DIM, inner + (g + 1) * MXU_DIM)
        xm_g = _dot(xn, wup_ref[:, cols])
        z_ref[:, cols] = _dot(xn, wup_ref[:, zcols]).astype(BF16)
        return xm_g

    def gate_part(g, xcb, xmb):
        rows = lambda base: slice(base + g * MXU_DIM, base + (g + 1) * MXU_DIM)
        return _dot(xcb, wgt_ref[rows(0), :]) + _dot(xmb, wgt_ref[rows(inner), :])

    gacc = bgt_ref[...]
    prev_act = None
    xm_next = up_proj(0)
    for g in range(n_groups):
        cols = slice(g * MXU_DIM, (g + 1) * MXU_DIM)
        xm = xm_next
        if g + 1 < n_groups:
            xm_next = up_proj(g + 1)
        xbuf[g, header:header + tm, :] = xm
        acc = bc_ref[:, cols] + wc_ref[CONV_W - 1:CONV_W, cols] * xm
        for d in range(1, CONV_W):
            lo = header - d * stride
            acc = acc + wc_ref[CONV_W - 1 - d:CONV_W - d, cols] * xbuf[g, lo:lo + tm, :]
        xcb = _silu(acc).astype(BF16)
        xc_ref[:, cols] = xcb
        xmb = xm.astype(BF16)
        q_ref[:, cols] = _dot(xcb, bq_ref[g]).astype(BF16)
        k_ref[:, cols] = _dot(xcb, bk_ref[g]).astype(BF16)
        v_ref[:, cols] = _dot(xmb, bv_ref[g]).astype(BF16)
        if g > 0:
            gacc = gacc + gate_part(g - 1, *prev_act)
        prev_act = (xcb, xmb)
        last = xbuf[g, tm:tm + header, :]
        tail_ref[:, cols] = last
        xbuf[g, 0:header, :] = last
    gates_ref[...] = gacc + gate_part(n_groups - 1, *prev_act)


def _ml_pre(x, norm_g, layer, w_up, w_conv, b_conv, bq, bk, bv, w_gates, b_gates, hist, j,
            tm, stride, tiles_per_seq):
    T, D = x.shape
    inner = w_conv.shape[-1]
    header = hist.shape[1]
    n_tiles = T // tm
    tok = lambda w: pl.BlockSpec((tm, w), lambda i: (i, 0))
    in_specs = [
        tok(D),
        _layer_spec((1, D), layer),
        _layer_spec((D, 2 * inner), j),
        _layer_spec((CONV_W, inner), j),
        _layer_spec((1, inner), j),
        _const_spec(bq.shape), _const_spec(bk.shape), _const_spec(bv.shape),
        _const_spec(w_gates.shape), _const_spec(b_gates.shape),
        pl.BlockSpec((None, header, inner), lambda i: (i // tiles_per_seq, 0, 0)),
    ]
    args = [x, norm_g.reshape(norm_g.shape[0], 1, D), w_up, w_conv,
            b_conv.reshape(b_conv.shape[0], 1, inner), bq, bk, bv, w_gates, b_gates, hist]
    out_shape = [
        jax.ShapeDtypeStruct((T, inner), BF16),
        jax.ShapeDtypeStruct((T, inner), BF16),
        jax.ShapeDtypeStruct((T, inner), BF16),
        jax.ShapeDtypeStruct((T, inner), BF16),
        jax.ShapeDtypeStruct((T, inner), BF16),
        jax.ShapeDtypeStruct((T, 2 * LANE), F32),
        jax.ShapeDtypeStruct((n_tiles, header, inner), F32),
    ]
    out_specs = [tok(inner), tok(inner), tok(inner), tok(inner), tok(inner), tok(2 * LANE),
                 pl.BlockSpec((None, header, inner), lambda i: (i, 0, 0))]
    body = functools.partial(_ml_pre_body, tm=tm, inner=inner, header=header, stride=stride,
                             tiles_per_seq=tiles_per_seq)
    return pl.pallas_call(
        body,
        out_shape=tuple(out_shape),
        grid=(n_tiles,),
        in_specs=in_specs,
        out_specs=tuple(out_specs),
        scratch_shapes=[pltpu.VMEM((inner // MXU_DIM, header + tm, MXU_DIM), F32)],
        compiler_params=_cparams(1),
        name="ml_pre",
    )(*args)


def _log_sigmoid(x):
    return jnp.minimum(x, 0.0) - jnp.log(1.0 + jnp.exp(-jnp.abs(x)))


def _token_scan(x, op, fill):
    n = x.shape[0]
    tok = lax.broadcasted_iota(jnp.int32, x.shape, 0)
    d = 1
    while d < n:
        x = op(x, jnp.where(tok >= d, pltpu.roll(x, d, axis=0), fill))
        d *= 2
    return x


def _gate_vectors(gates, m_prev, L):
    ig = gates[:, :LANE]
    bt = _token_scan(_log_sigmoid(gates[:, LANE:]), jnp.add, 0.0)
    a = ig - bt
    m = bt + jnp.maximum(_token_scan(a, jnp.maximum, NEG), m_prev)
    dcol = bt - m
    bl, ml = bt[L - 1:L, :], m[L - 1:L, :]
    return dict(a=a, dcol=dcol, w_inter=jnp.exp(m_prev + dcol), exp_neg_m=jnp.exp(-m),
                gs=jnp.exp(bl - bt + ig - ml) * HEAD_DIM ** -0.5,
                g_inter=jnp.exp(m_prev + bl - ml), ml=ml)


def _ml_core_body(*refs, cfgs):
    n_in = [8 if has_state else 5 for (_, _, _, has_state) in cfgs]
    pos_out = sum(n_in)
    pos_in = 0
    for (L, nb, nc, has_state), k_in in zip(cfgs, n_in):
        _ml_core_group(refs[pos_in:pos_in + k_in], refs[pos_out:pos_out + 4], L, nb, nc, has_state)
        pos_in += k_in
        pos_out += 4


def _ml_core_group(in_refs, out_refs, L, nb, nc, has_state):
    h_ref, c_ref, n_ref, m_ref = out_refs
    if has_state:
        q_ref, k_ref, v_ref, gates_ref, c0_ref, n0_ref, m0_ref, _ = in_refs
    else:
        q_ref, k_ref, v_ref, gates_ref, _ = in_refs
    if has_state and nc == 1:
        c_prev, n_prev_ref, m_prev_ref = c0_ref, n0_ref, m0_ref
    else:
        c_prev, n_prev_ref, m_prev_ref = c_ref, n_ref, m_ref

        @pl.when(pl.program_id(0) % nc == 0)
        def _():
            if has_state:
                c_ref[...] = c0_ref[...]
                n_ref[...] = n0_ref[...]
                m_ref[...] = m0_ref[...]
            else:
                c_ref[...] = jnp.zeros_like(c_ref)
                n_ref[...] = jnp.zeros_like(n_ref)
                m_ref[...] = jnp.zeros_like(m_ref)

    log_scale = -0.5 * math.log(HEAD_DIM)
    row = lax.broadcasted_iota(jnp.int32, (L, L), 0)
    col = lax.broadcasted_iota(jnp.int32, (L, L), 1)
    causal = col <= row
    heads = range(N_HEADS)
    seqs = range(nb)
    hcols = [slice(h * HEAD_DIM, (h + 1) * HEAD_DIM) for h in heads]
    ones_rows = jnp.ones((SUBLANE, L), BF16)
    n_prev = [n_prev_ref[s] for s in seqs]


    qc = [[_dot(q_ref[s, :, hcols[h]], c_prev[s, h].astype(BF16)) for h in heads] for s in seqs]

    gvs, a_rows = [], []
    for s in seqs:
        gv = _gate_vectors(gates_ref[s], m_prev_ref[s, 0:1, :], L)
        m_ref[s] = jnp.broadcast_to(gv['ml'], m_ref.shape[1:])
        a_shift = gv['a'] + log_scale
        if L % LANE == 0:
            a_t = a_shift.T
            a_rows.append([a_t[h:h + 1, :] for h in heads])
        else:
            a_rows.append([jnp.sum(jnp.where(row == col, a_shift[:, h:h + 1], 0.0), axis=0,
                                   keepdims=True) for h in heads])
        gvs.append(gv)

    for s in seqs:
        hcol = lambda name, h: gvs[s][name][:, h:h + 1]
        for h in heads:
            qh, kh, vh = q_ref[s, :, hcols[h]], k_ref[s, :, hcols[h]], v_ref[s, :, hcols[h]]

            g_inter = hcol('g_inter', h)
            kg = kh * hcol('gs', h).astype(BF16)
            upd = lax.dot_general(kg, vh, (((0,), (0,)), ((), ())), preferred_element_type=F32)
            c_ref[s, h] = g_inter * c_prev[s, h] + upd
            n_ref[s, h:h + 1, :] = (g_inter * n_prev[s][h:h + 1, :]
                                    + _dot(ones_rows, kg)[0:1, :])

            w_inter = hcol('w_inter', h)
            w_intra = jnp.exp(jnp.where(causal, hcol('dcol', h) + a_rows[s][h], NEG))
            sc = lax.dot_general(qh, kh, (((1,), (1,)), ((), ())),
                                 preferred_element_type=F32) * w_intra
            num = _dot(sc.astype(BF16), vh) + qc[s][h] * w_inter
            qn = jnp.sum((qh * n_prev[s][h:h + 1, :].astype(BF16)).astype(F32), axis=1,
                         keepdims=True)
            nq = jnp.sum(sc, axis=1, keepdims=True) + w_inter * qn
            den = jnp.maximum(jnp.abs(nq), hcol('exp_neg_m', h))
            mu = jnp.mean(num, axis=-1, keepdims=True)
            hc = num - mu
            var = jnp.mean(hc * hc, axis=-1, keepdims=True)
            h_ref[s, :, hcols[h]] = (hc * lax.rsqrt(var + EPS * den * den)).astype(BF16)


def _ml_core(groups, j):
    ios = [_core_group_io(g, j) for g in groups]
    steps = {io['steps'] for io in ios}
    assert len(steps) == 1, "token groups must have the same number of grid steps"
    in_specs, args, out_specs, out_shape, aliases = [], [], [], [], {}
    for gi, io in enumerate(ios):
        if io['alias_in'] is not None:
            aliases[len(args) + io['alias_in']] = N_CORE_OUT * gi + 1
        in_specs += io['in_specs']
        args += io['args']
        out_specs += io['out_specs']
        out_shape += io['out_shape']
    outs = pl.pallas_call(
        functools.partial(_ml_core_body, cfgs=tuple(io['cfg'] for io in ios)),
        out_shape=tuple(out_shape),
        grid=(steps.pop(),),
        in_specs=in_specs,
        out_specs=tuple(out_specs),
        input_output_aliases=aliases,
        compiler_params=_cparams(1, VMEM_LIMIT_CORE),
        name="ml_core",
    )(*args)
    return [_core_group_result(g, outs[N_CORE_OUT * gi:N_CORE_OUT * (gi + 1)])
            for gi, g in enumerate(groups)]


N_CORE_OUT = 4


def _core_group_io(g, j):
    n_ml = 2
    B, nc, L, nb = g['B'], g['nc'], g['L'], g['nb']
    inner = g['q'].shape[-1]
    has_state = g['state'] is not None
    seq3 = lambda a: a.reshape(B, nc * L, a.shape[-1])
    tok = lambda w: pl.BlockSpec((nb, L, w), lambda t: (t // nc, t % nc, 0))
    per_seq = lambda shape: pl.BlockSpec((nb,) + shape, lambda t: (t // nc,) + (0,) * len(shape))
    per_seq_layer = lambda shape: pl.BlockSpec(
        (None, nb) + shape, lambda t: (j, t // nc) + (0,) * len(shape))
    c_block = per_seq_layer((N_HEADS, HEAD_DIM, HEAD_DIM))
    in_specs = [tok(inner), tok(inner), tok(inner), tok(2 * LANE)]
    args = [seq3(g['q']), seq3(g['k']), seq3(g['v']), seq3(g['gates'])]
    if has_state:
        in_specs += [c_block, per_seq_layer((N_HEADS, HEAD_DIM)), per_seq((SUBLANE, LANE))]
        args += list(g['state'])
    in_specs.append(pl.BlockSpec(memory_space=pl.ANY))
    alias_in = None
    if g['c_all'] is not None:
        args.append(g['c_all'])
        alias_in = len(args) - 1
    else:
        args.append(jnp.zeros((SUBLANE, LANE), F32))
    out_shape = [jax.ShapeDtypeStruct((B, nc * L, inner), BF16),
                 jax.ShapeDtypeStruct((n_ml, B, N_HEADS, HEAD_DIM, HEAD_DIM), F32),
                 jax.ShapeDtypeStruct((B, N_HEADS, HEAD_DIM), F32),
                 jax.ShapeDtypeStruct((B, SUBLANE, LANE), F32)]
    out_specs = [tok(inner), c_block, per_seq((N_HEADS, HEAD_DIM)), per_seq((SUBLANE, LANE))]
    return dict(in_specs=in_specs, args=args, out_specs=out_specs, out_shape=out_shape,
                cfg=(L, nb, nc, has_state), alias_in=alias_in, steps=B // nb * nc)


def _core_group_result(g, outs):
    hn, c_all, n, m = outs
    return hn.reshape(g['q'].shape), c_all, n, m


def _gelu(x):
    return 0.5 * x * (1.0 + lax.erf(x * (2.0 ** -0.5)))


def _cm_body(*refs, width, sub, emit_vn):
    if emit_vn:
        x_ref, g_ref, win_ref, bin_ref, lg_ref, wmix_ref, bmix_ref, wout_ref, o_ref, vn_ref = refs
    else:
        x_ref, g_ref, win_ref, bin_ref, lg_ref, wmix_ref, bmix_ref, wout_ref, o_ref = refs
    gd = width // CM_GROUPS
    n_sub = x_ref.shape[0] // sub

    def in_proj(s):
        x = x_ref[s * sub:(s + 1) * sub, :]
        xn = _rmsnorm(x, g_ref[...]).astype(BF16)
        v = _gelu(_dot(xn, win_ref[:, width:]) + bin_ref[:, width:])
        u = _gelu(_dot(xn, win_ref[:, :width]) + bin_ref[:, :width])
        return x, u, v

    def gate_and_out(s, x, u, v):
        rows = slice(s * sub, (s + 1) * sub)
        mu = jnp.mean(v, axis=-1, keepdims=True)
        vc = v - mu
        var = jnp.mean(vc * vc, axis=-1, keepdims=True)
        vn = vc * lax.rsqrt(var + EPS) * lg_ref[...]
        if emit_vn:
            vn_ref[rows, :] = vn
        vnb = vn.astype(BF16)
        bmix = bmix_ref[...]
        parts = []
        for g in range(CM_GROUPS):
            cols = slice(g * gd, (g + 1) * gd)
            mix = _dot(wmix_ref[g], vnb[:, cols]) + bmix[:, g:g + 1]
            parts.append((u[:, cols] * mix).astype(BF16))
        o_ref[rows, :] = x + _dot(jnp.concatenate(parts, axis=-1), wout_ref[...])

    nxt = in_proj(0)
    for s in range(n_sub):
        cur = nxt
        if s + 1 < n_sub:
            nxt = in_proj(s + 1)
        gate_and_out(s, *cur)


def _chunk_mlp(x, norm_g, layer, w_in, b_in, ln_g, wmix, bmix, w_out, j, tm, emit_vn):
    T, D = x.shape
    width = ln_g.shape[-1]
    body = functools.partial(_cm_body, width=width, sub=wmix.shape[1], emit_vn=emit_vn)
    tok = lambda w: pl.BlockSpec((tm, w), lambda i: (i, 0))
    out_shape = [jax.ShapeDtypeStruct((T, D), F32)]
    out_specs = [tok(D)]
    if emit_vn:
        out_shape.append(jax.ShapeDtypeStruct((T, width), F32))
        out_specs.append(tok(width))
    return pl.pallas_call(
        body,
        out_shape=tuple(out_shape),
        grid=(T // tm,),
        in_specs=[tok(D), _layer_spec((1, D), layer),
                  _layer_spec((D, 2 * width), j), _layer_spec((1, 2 * width), j),
                  _layer_spec((1, width), j),
                  _const_spec(wmix.shape), _const_spec(bmix.shape),
                  _layer_spec((width, D), j)],
        out_specs=tuple(out_specs),
        compiler_params=_cparams(1),
        name="chunk_mlp",
    )(x, norm_g.reshape(norm_g.shape[0], 1, D), w_in, b_in.reshape(b_in.shape[0], 1, 2 * width),
      ln_g.reshape(ln_g.shape[0], 1, width), wmix, bmix, w_out)


def _blockdiag_dense(w):
    rows = w.reshape(-1, MXU_DIM, QKV_BLOCK)
    r = lax.broadcasted_iota(jnp.int32, (MXU_DIM, MXU_DIM), 0)
    c = lax.broadcasted_iota(jnp.int32, (MXU_DIM, MXU_DIM), 1)
    spread = (lax.broadcasted_iota(jnp.int32, (QKV_BLOCK, MXU_DIM), 1) % QKV_BLOCK
              == lax.broadcasted_iota(jnp.int32, (QKV_BLOCK, MXU_DIM), 0)).astype(w.dtype)
    dense = jnp.einsum('grd,dc->grc', rows, spread, precision=lax.Precision.HIGHEST)
    return jnp.where(r // QKV_BLOCK == c // QKV_BLOCK, dense, 0)


def _gate_weights(w_ig, b_ig, w_fg, b_fg, dense_qkv):
    n_tiles = dense_qkv[0].shape[0]
    lane = lax.broadcasted_iota(jnp.int32, (2 * N_HEADS, 2 * LANE), 1)
    src = lax.broadcasted_iota(jnp.int32, (2 * N_HEADS, 2 * LANE), 0)
    place = (lane == (src % N_HEADS) + LANE * (src // N_HEADS)).astype(w_ig.dtype)
    w = jnp.dot(jnp.concatenate([w_ig, w_fg], axis=1), place,
                precision=lax.Precision.HIGHEST)
    w = w.reshape(3, n_tiles, MXU_DIM, 2 * LANE)
    folded = jnp.einsum('mgrc,mgcl->mgrl', jnp.stack(dense_qkv), w,
                        precision=lax.Precision.HIGHEST)
    folded = folded.reshape(3, n_tiles * MXU_DIM, 2 * LANE)
    w_all = jnp.concatenate([folded[0] + folded[1], folded[2]], axis=0)
    bpad = jnp.zeros((LANE - N_HEADS,), b_ig.dtype)
    b = jnp.concatenate([b_ig, bpad, b_fg, bpad])
    return w_all.astype(BF16), b.reshape(1, 2 * LANE)


def _trunks(xs, grps, p):
    depth = p['norm_ff1'].shape[0]
    sts = [dict(x=x, c_all=None, n=[], m=[], conv=[], v=[]) for x in xs]

    ffn_w = {'cur': tuple(w[0].astype(BF16) for w in p['ffn1_w'])}

    def ffn_round(norm_key, i, nxt, use_mixer=False, final_g=None):
        new_w = None
        for gi, (grp, st) in enumerate(zip(grps, sts)):
            mixer = st['mixer'] if use_mixer else None
            tm = grp['tm'] if mixer is None else grp['tm_mixer']
            cast = nxt if gi == 0 else None
            out = _ffn(st['x'], p[norm_key], ffn_w['cur'], i, tm, final_g=final_g, mixer=mixer,
                       cast=cast)
            if cast is not None:
                out, new_w = out
            st['x'] = out
        ffn_w['cur'] = new_w

    for i in range(depth):
        j = i // 2
        ffn_round('norm_ff1', i, (p['ffn2_w'], i))
        if i % 2 == 0:
            core_in = []
            for grp, st in zip(grps, sts):
                q, k, v, xc, z, gates, tail = _ml_pre(
                    st['x'], p['norm_mix'], i, p['ml_w_up'], p['ml_w_conv'], p['ml_b_conv'],
                    p['bq'][j], p['bk'][j], p['bv'][j], p['w_gates'][j], p['b_gates'][j],
                    grp['hist'][j], j, grp['pre_tm'], grp['stride'], grp['tiles_per_seq'])
                st['conv'].append(grp['conv_out'](tail))
                st['xc'], st['z'] = xc, z
                q, k, v, gates = grp['to_core'](q, k, v, gates)
                core_in.append(dict(q=q, k=k, v=v, gates=gates, B=grp['B'], nc=grp['nc'], L=grp['L'],
                                    nb=grp['nb'], state=grp['state'](j), c_all=st['c_all']))
            for grp, st, (hn, c_all, n, m) in zip(grps, sts, _ml_core(core_in, j)):
                st['c_all'] = c_all
                st['n'].append(n)
                st['m'].append(m[:, 0, :N_HEADS])
                st['mixer'] = (grp['from_core'](hn), st['xc'], st['z'], p['ml_hn_g'], p['ml_skip'],
                               p['ml_w_down'], j)
        else:
            for grp, st in zip(grps, sts):
                outs = _chunk_mlp(st['x'], p['norm_mix'], i, p['cm_w_in'], p['cm_b_in'], p['cm_ln_g'],
                                  grp['wmix'][j], grp['bmix'][j], p['cm_w_out'], j, grp['cm_tm'],
                                  emit_vn=grp['emit_vn'])
                st['x'] = outs[0]
                st['v'].append(outs[1] if grp['emit_vn'] else None)
                st['mixer'] = None
        last = i == depth - 1
        ffn_round('norm_ff2', i, None if last else (p['ffn1_w'], i + 1), use_mixer=True,
                  final_g=p['norm_final'] if last else None)
    return [(st['x'], st['c_all'], jnp.stack(st['n']), jnp.stack(st['m']), jnp.stack(st['conv']),
             st['v']) for st in sts]


def kernel(x_prompt, x_sample, state_C, state_n, state_m, state_conv, norm_ff1, norm_mix, norm_ff2, norm_final, ffn1_w_gate, ffn1_w_up, ffn1_w_down, ffn2_w_gate, ffn2_w_up, ffn2_w_down, ml_w_up, ml_w_conv, ml_b_conv, ml_w_q, ml_w_k, ml_w_v, ml_w_ig, ml_b_ig, ml_w_fg, ml_b_fg, ml_hn_g, ml_skip, ml_w_down, cm_w_in, cm_b_in, cm_ln_g, cm_w_s, cm_b_s, cm_w_out):
    bp, sp, d_model = x_prompt.shape
    bs, ss, _ = x_sample.shape
    n_ml, inner = ml_w_conv.shape[0], ml_w_conv.shape[-1]
    n_cm, width = cm_ln_g.shape
    hist_rows = CONV_W - 1
    sb = min(64, bs)

    dense_qkv = [tuple(_blockdiag_dense(w[j]) for w in (ml_w_q, ml_w_k, ml_w_v)) for j in range(n_ml)]
    gate_wb = [_gate_weights(ml_w_ig[j], ml_b_ig[j], ml_w_fg[j], ml_b_fg[j], dense_qkv[j])
               for j in range(n_ml)]
    p = {
        'norm_ff1': norm_ff1, 'norm_mix': norm_mix, 'norm_ff2': norm_ff2, 'norm_final': norm_final,
        'ffn1_w': (ffn1_w_gate, ffn1_w_up, ffn1_w_down),
        'ffn2_w': (ffn2_w_gate, ffn2_w_up, ffn2_w_down),
        'ml_w_up': ml_w_up.astype(BF16), 'ml_w_conv': ml_w_conv, 'ml_b_conv': ml_b_conv,
        'bq': [d[0].astype(BF16) for d in dense_qkv], 'bk': [d[1].astype(BF16) for d in dense_qkv],
        'bv': [d[2].astype(BF16) for d in dense_qkv],
        'w_gates': [wb[0] for wb in gate_wb], 'b_gates': [wb[1] for wb in gate_wb],
        'ml_hn_g': ml_hn_g, 'ml_skip': ml_skip, 'ml_w_down': ml_w_down.astype(BF16),
        'cm_w_in': cm_w_in.astype(BF16), 'cm_b_in': cm_b_in, 'cm_ln_g': cm_ln_g,
        'cm_w_out': cm_w_out.astype(BF16),
    }

    def mix_weights(L, kron_left):
        causal = jnp.tril(jnp.ones((L, L), dtype=bool))
        ws = jnp.where(causal, cm_w_s[:, :, :L, :L], 0)
        bs_ = cm_b_s[:, :, :L]
        if kron_left:
            reps = kron_left
            eye = jnp.eye(reps, dtype=ws.dtype)
            wm = ws[:, :, None, :, None, :] * eye[None, None, :, None, :, None]
            wm = wm.reshape(n_cm, CM_GROUPS, reps * L, reps * L)
            bm = jnp.tile(bs_, (1, 1, reps))
        else:
            eye = jnp.eye(sb, dtype=ws.dtype)
            wm = ws[:, :, :, None, :, None] * eye[None, None, None, :, None, :]
            wm = wm.reshape(n_cm, CM_GROUPS, L * sb, L * sb)
            bm = jnp.repeat(bs_, sb, axis=2)
        bm = jnp.swapaxes(bm, 1, 2)
        bm = jnp.concatenate([bm, jnp.zeros(bm.shape[:2] + (LANE - CM_GROUPS,), bm.dtype)], axis=2)
        return wm.astype(BF16), bm

    tm_p = 1024
    pre_tm_p = 512
    cm_tm_p = 1024
    chunk_p = min(256, sp)
    cm_chunk_p = min(128, sp)
    wmix_p, bmix_p = mix_weights(cm_chunk_p, MXU_DIM // cm_chunk_p)
    grp_p = {
        'tm': tm_p, 'tm_mixer': 512, 'pre_tm': pre_tm_p, 'cm_tm': cm_tm_p, 'emit_vn': False,
        'stride': 1,
        'tiles_per_seq': sp // pre_tm_p,
        'hist': [jnp.zeros((bp, SUBLANE, inner), F32)] * n_ml,
        'conv_out': lambda tail: tail.reshape(bp, sp // pre_tm_p, SUBLANE, inner)[:, -1, SUBLANE - hist_rows:, :],
        'to_core': lambda q, k, v, g: (q, k, v, g),
        'from_core': lambda hn: hn,
        'state': lambda j: None,
        'B': bp, 'nc': sp // chunk_p, 'L': chunk_p, 'nb': 1,
        'wmix': wmix_p, 'bmix': bmix_p,
    }

    ts = bs * ss
    n_sg = bs // sb
    l_pad = SUBLANE

    def tile_major(a):
        steps, feat = a.shape[1], a.shape[2]
        return a.reshape(n_sg, sb, steps, feat).transpose(0, 2, 1, 3).reshape(n_sg * steps * sb, feat)

    def batch_major(a, steps):
        feat = a.shape[-1]
        return a.reshape(n_sg, steps, sb, feat).transpose(0, 2, 1, 3).reshape(bs, steps, feat)

    def to_core(q, k, v, g):
        def seq_major(a, fill):
            a = batch_major(a, ss)
            pad = jnp.broadcast_to(fill.astype(a.dtype), (bs, l_pad - ss, a.shape[-1]))
            return jnp.concatenate([a, pad], axis=1).reshape(bs * l_pad, a.shape[-1])
        zero = jnp.zeros((inner,), F32)
        lane_pad = jnp.zeros((LANE - N_HEADS,), F32)
        gfill = jnp.concatenate([jnp.full((N_HEADS,), NEG, F32), lane_pad,
                                 jnp.full((N_HEADS,), -NEG, F32), lane_pad])
        return seq_major(q, zero), seq_major(k, zero), seq_major(v, zero), seq_major(g, gfill)

    def from_core(hn):
        return tile_major(hn.reshape(bs, l_pad, inner)[:, :ss])

    def sample_state(j):
        m0 = jnp.pad(state_m[j], ((0, 0), (0, LANE - N_HEADS)))
        return state_C, state_n, jnp.broadcast_to(m0[:, None, :], (bs, SUBLANE, LANE))

    wmix_s, bmix_s = mix_weights(ss, 0)
    tile_s = ss * sb
    hist_s = [tile_major(state_conv[j]).reshape(n_sg, hist_rows * sb, inner) for j in range(n_ml)]
    grp_s = {
        'tm': tile_s, 'tm_mixer': tile_s, 'pre_tm': tile_s, 'cm_tm': tile_s, 'emit_vn': True,
        'stride': sb,
        'tiles_per_seq': 1,
        'hist': hist_s,
        'conv_out': lambda tail: batch_major(tail.reshape(n_sg * hist_rows * sb, inner), hist_rows),
        'to_core': to_core, 'from_core': from_core, 'state': sample_state,
        'B': bs, 'nc': 1, 'L': l_pad, 'nb': 2,
        'wmix': wmix_s, 'bmix': bmix_s,
    }
    ((yp, c_p, n_p, m_p, conv_p, _), (ys, c_s, n_s, m_s, conv_s, v_rows)) = _trunks(
        [x_prompt.reshape(bp * sp, d_model), tile_major(x_sample)], [grp_p, grp_s], p)

    y_prompt = yp.reshape(bp, sp, d_model)
    y_sample = batch_major(ys, ss)
    v_sample = jnp.stack([batch_major(vr, ss) for vr in v_rows])
    return (y_prompt, y_sample, c_p, n_p, m_p, conv_p, c_s, n_s, m_s, conv_s, v_sample)
```

```python
import functools
import math

import jax
import jax.numpy as jnp
from jax import lax
from jax.experimental import pallas as pl
from jax.experimental.pallas import tpu as pltpu

F32 = jnp.float32
BF16 = jnp.bfloat16

EPS = 1e-6
N_HEADS = 4
HEAD_DIM = 512
QKV_BLOCK = 4
CONV_W = 4
CM_GROUPS = 4
NEG = -1e30

LANE = 128
SUBLANE = 8
MXU_DIM = 256
VMEM_LIMIT = 56 * 1024 * 1024
VMEM_LIMIT_CORE = 60 * 1024 * 1024


def _cparams(n_axes, vmem_limit=VMEM_LIMIT):
    return pltpu.CompilerParams(dimension_semantics=("arbitrary",) * n_axes,
                                vmem_limit_bytes=vmem_limit)


def _const_spec(shape):
    nd = len(shape)
    return pl.BlockSpec(shape, lambda *_: (0,) * nd, pipeline_mode=pl.Buffered(1))


def _layer_spec(shape, layer):
    nd = len(shape)
    return pl.BlockSpec((None,) + tuple(shape), lambda *_: (layer,) + (0,) * nd,
                        pipeline_mode=pl.Buffered(1))


def _rmsnorm(x, g):
    return x * lax.rsqrt(jnp.mean(x * x, axis=-1, keepdims=True) + EPS) * g


def _silu(x):
    return x * jax.nn.sigmoid(x)


def _dot(a, b):
    return jnp.dot(a, b, preferred_element_type=F32)


def _ffn_body(*refs, ff_chunks, final_norm, mixer_out, n_cast):
    if n_cast:
        cast_in, cast_out = refs[-2 * n_cast - 1:-n_cast - 1], refs[-n_cast:]
        refs = refs[:-2 * n_cast - 1] + (refs[-n_cast - 1],)
        for src, dst in zip(cast_in, cast_out):
            dst[...] = src[...].astype(dst.dtype)
    if mixer_out:
        (x_ref, g_ref, wg_ref, wu_ref, wd_ref, gf_ref,
         hn_ref, xc_ref, z_ref, hg_ref, sk_ref, wdm_ref, o_ref) = refs
    else:
        x_ref, g_ref, wg_ref, wu_ref, wd_ref, gf_ref, o_ref = refs
    x = x_ref[...]
    if mixer_out:
        for lo in range(0, hn_ref.shape[1], HEAD_DIM):
            cols = slice(lo, lo + HEAD_DIM)
            out = ((hn_ref[:, cols].astype(F32) * hg_ref[:, cols]
                    + sk_ref[:, cols] * xc_ref[:, cols].astype(F32))
                   * _silu(z_ref[:, cols].astype(F32)))
            x = x + _dot(out.astype(BF16), wdm_ref[cols, :])
    xn = _rmsnorm(x, g_ref[...]).astype(wg_ref.dtype)
    acc = None
    for lo, hi in ff_chunks:
        gate = _dot(xn, wg_ref[:, lo:hi])
        up = _dot(xn, wu_ref[:, lo:hi])
        h = (_silu(gate) * up).astype(wd_ref.dtype)
        part = _dot(h, wd_ref[lo:hi, :])
        acc = part if acc is None else acc + part
    y = x + 0.5 * acc
    if final_norm:
        y = _rmsnorm(y, gf_ref[...])
    o_ref[...] = y


CAST_BLOCKS = 16


def _ffn(x, norm_g, weights, layer, tm, final_g=None, mixer=None, cast=None):
    T, D = x.shape
    w_gate, w_up, w_down = weights
    d_ff = w_gate.shape[-1]
    n_steps = T // tm
    step = 4 * MXU_DIM
    ff_chunks = tuple((lo, min(lo + step, d_ff)) for lo in range(0, d_ff, step))
    final_norm = final_g is not None
    gf = final_g if final_norm else norm_g[layer]
    n_cast = 0 if cast is None else len(cast[0])
    body = functools.partial(_ffn_body, ff_chunks=ff_chunks, final_norm=final_norm,
                             mixer_out=mixer is not None, n_cast=n_cast)
    tok = lambda w: pl.BlockSpec((tm, w), lambda i: (i, 0))
    in_specs = [tok(D), _layer_spec((1, D), layer), _const_spec((D, d_ff)),
                _const_spec((D, d_ff)), _const_spec((d_ff, D)), _const_spec((1, D))]
    args = [x, norm_g.reshape(norm_g.shape[0], 1, D), w_gate, w_up, w_down, gf.reshape(1, D)]
    if mixer is not None:
        hn, xc, z, gain, skip, w_down_m, j = mixer
        inner = hn.shape[-1]
        in_specs += [tok(inner), tok(inner), tok(inner), _layer_spec((1, inner), j),
                     _layer_spec((1, inner), j), _layer_spec((inner, D), j)]
        args += [hn, xc, z, gain.reshape(gain.shape[0], 1, inner),
                 skip.reshape(skip.shape[0], 1, inner), w_down_m]
    out_shape = [jax.ShapeDtypeStruct((T, D), F32)]
    out_specs = [tok(D)]
    if cast is not None:
        mats, cast_layer = cast
        assert n_steps % CAST_BLOCKS == 0
        per_block = n_steps // CAST_BLOCKS
        for w in mats:
            rows, cols = w.shape[1] // CAST_BLOCKS, w.shape[2]
            in_specs.append(pl.BlockSpec((None, rows, cols), lambda i: (cast_layer, i // per_block, 0)))
            args.append(w)
            out_shape.append(jax.ShapeDtypeStruct(w.shape[1:], BF16))
            out_specs.append(pl.BlockSpec((rows, cols), lambda i: (i // per_block, 0)))
    outs = pl.pallas_call(
        body,
        out_shape=tuple(out_shape),
        grid=(n_steps,),
        in_specs=in_specs,
        out_specs=tuple(out_specs),
        compiler_params=_cparams(1, VMEM_LIMIT if cast is None else VMEM_LIMIT_CORE),
        name="ffn",
    )(*args)
    return outs[0] if cast is None else (outs[0], tuple(outs[1:]))


def _ml_pre_body(x_ref, g_ref, wup_ref, wc_ref, bc_ref, bq_ref, bk_ref, bv_ref, wgt_ref, bgt_ref,
                 hist_ref, q_ref, k_ref, v_ref, xc_ref, z_ref, gates_ref, tail_ref, xbuf,
                 *, tm, inner, header, stride, tiles_per_seq):
    i = pl.program_id(0)

    n_groups = inner // MXU_DIM

    @pl.when(i % tiles_per_seq == 0)
    def _():
        for g in range(n_groups):
            xbuf[g, 0:header, :] = hist_ref[:, g * MXU_DIM:(g + 1) * MXU_DIM]

    xn = _rmsnorm(x_ref[...], g_ref[...]).astype(BF16)

    def up_proj(g):
        cols = slice(g * MXU_DIM, (g + 1) * MXU_DIM)
        zcols = slice(inner + g * MXU_DIM, inner + (g + 1) * MXU_DIM)
        xm_g = _dot(xn, wup_ref[:, cols])
        z_ref[:, cols] = _dot(xn, wup_ref[:, zcols]).astype(BF16)
        return xm_g

    def gate_part(g, xcb, xmb):
        rows = lambda base: slice(base + g * MXU_DIM, base + (g + 1) * MXU_DIM)
        return _dot(xcb, wgt_ref[rows(0), :]) + _dot(xmb, wgt_ref[rows(inner), :])

    gacc = bgt_ref[...]
    prev_act = None
    xm_next = up_proj(0)
    for g in range(n_groups):
        cols = slice(g * MXU_DIM, (g + 1) * MXU_DIM)
        xm = xm_next
        if g + 1 < n_groups:
            xm_next = up_proj(g + 1)
        xbuf[g, header:header + tm, :] = xm
        acc = bc_ref[:, cols] + wc_ref[CONV_W - 1:CONV_W, cols] * xm
        for d in range(1, CONV_W):
            lo = header - d * stride
            acc = acc + wc_ref[CONV_W - 1 - d:CONV_W - d, cols] * xbuf[g, lo:lo + tm, :]
        xcb = _silu(acc).astype(BF16)
        xc_ref[:, cols] = xcb
        xmb = xm.astype(BF16)
        q_ref[:, cols] = _dot(xcb, bq_ref[g]).astype(BF16)
        k_ref[:, cols] = _dot(xcb, bk_ref[g]).astype(BF16)
        v_ref[:, cols] = _dot(xmb, bv_ref[g]).astype(BF16)
        if g > 0:
            gacc = gacc + gate_part(g - 1, *prev_act)
        prev_act = (xcb, xmb)
        last = xbuf[g, tm:tm + header, :]
        tail_ref[:, cols] = last
        xbuf[g, 0:header, :] = last
    gates_ref[...] = gacc + gate_part(n_groups - 1, *prev_act)


def _ml_pre(x, norm_g, layer, w_up, w_conv, b_conv, bq, bk, bv, w_gates, b_gates, hist, j,
            tm, stride, tiles_per_seq):
    T, D = x.shape
    inner = w_conv.shape[-1]
    header = hist.shape[1]
    n_tiles = T // tm
    tok = lambda w: pl.BlockSpec((tm, w), lambda i: (i, 0))
    in_specs = [
        tok(D),
        _layer_spec((1, D), layer),
        _layer_spec((D, 2 * inner), j),
        _layer_spec((CONV_W, inner), j),
        _layer_spec((1, inner), j),
        _const_spec(bq.shape), _const_spec(bk.shape), _const_spec(bv.shape),
        _const_spec(w_gates.shape), _const_spec(b_gates.shape),
        pl.BlockSpec((None, header, inner), lambda i: (i // tiles_per_seq, 0, 0)),
    ]
    args = [x, norm_g.reshape(norm_g.shape[0], 1, D), w_up, w_conv,
            b_conv.reshape(b_conv.shape[0], 1, inner), bq, bk, bv, w_gates, b_gates, hist]
    out_shape = [
        jax.ShapeDtypeStruct((T, inner), BF16),
        jax.ShapeDtypeStruct((T, inner), BF16),
        jax.ShapeDtypeStruct((T, inner), BF16),
        jax.ShapeDtypeStruct((T, inner), BF16),
        jax.ShapeDtypeStruct((T, inner), BF16),
        jax.ShapeDtypeStruct((T, 2 * LANE), F32),
        jax.ShapeDtypeStruct((n_tiles, header, inner), F32),
    ]
    out_specs = [tok(inner), tok(inner), tok(inner), tok(inner), tok(inner), tok(2 * LANE),
                 pl.BlockSpec((None, header, inner), lambda i: (i, 0, 0))]
    body = functools.partial(_ml_pre_body, tm=tm, inner=inner, header=header, stride=stride,
                             tiles_per_seq=tiles_per_seq)
    return pl.pallas_call(
        body,
        out_shape=tuple(out_shape),
        grid=(n_tiles,),
        in_specs=in_specs,
        out_specs=tuple(out_specs),
        scratch_shapes=[pltpu.VMEM((inner // MXU_DIM, header + tm, MXU_DIM), F32)],
        compiler_params=_cparams(1),
        name="ml_pre",
    )(*args)


def _log_sigmoid(x):
    return jnp.minimum(x, 0.0) - jnp.log(1.0 + jnp.exp(-jnp.abs(x)))


def _token_scan(x, op, fill):
    n = x.shape[0]
    tok = lax.broadcasted_iota(jnp.int32, x.shape, 0)
    d = 1
    while d < n:
        x = op(x, jnp.where(tok >= d, pltpu.roll(x, d, axis=0), fill))
        d *= 2
    return x


def _gate_vectors(gates, m_prev, L):
    ig = gates[:, :LANE]
    bt = _token_scan(_log_sigmoid(gates[:, LANE:]), jnp.add, 0.0)
    a = ig - bt
    m = bt + jnp.maximum(_token_scan(a, jnp.maximum, NEG), m_prev)
    dcol = bt - m
    bl, ml = bt[L - 1:L, :], m[L - 1:L, :]
    return dict(a=a, dcol=dcol, w_inter=jnp.exp(m_prev + dcol), exp_neg_m=jnp.exp(-m),
                gs=jnp.exp(bl - bt + ig - ml) * HEAD_DIM ** -0.5,
                g_inter=jnp.exp(m_prev + bl - ml), ml=ml)


def _ml_core_body(*refs, cfgs):
    n_in = [8 if has_state else 5 for (_, _, _, has_state) in cfgs]
    pos_out = sum(n_in)
    pos_in = 0
    for (L, nb, nc, has_state), k_in in zip(cfgs, n_in):
        _ml_core_group(refs[pos_in:pos_in + k_in], refs[pos_out:pos_out + 4], L, nb, nc, has_state)
        pos_in += k_in
        pos_out += 4


def _ml_core_group(in_refs, out_refs, L, nb, nc, has_state):
    h_ref, c_ref, n_ref, m_ref = out_refs
    if has_state:
        q_ref, k_ref, v_ref, gates_ref, c0_ref, n0_ref, m0_ref, _ = in_refs
    else:
        q_ref, k_ref, v_ref, gates_ref, _ = in_refs
    if has_state and nc == 1:
        c_prev, n_prev_ref, m_prev_ref = c0_ref, n0_ref, m0_ref
    else:
        c_prev, n_prev_ref, m_prev_ref = c_ref, n_ref, m_ref

        @pl.when(pl.program_id(0) % nc == 0)
        def _():
            if has_state:
                c_ref[...] = c0_ref[...]
                n_ref[...] = n0_ref[...]
                m_ref[...] = m0_ref[...]
            else:
                c_ref[...] = jnp.zeros_like(c_ref)
                n_ref[...] = jnp.zeros_like(n_ref)
                m_ref[...] = jnp.zeros_like(m_ref)

    log_scale = -0.5 * math.log(HEAD_DIM)
    row = lax.broadcasted_iota(jnp.int32, (L, L), 0)
    col = lax.broadcasted_iota(jnp.int32, (L, L), 1)
    causal = col <= row
    heads = range(N_HEADS)
    seqs = range(nb)
    hcols = [slice(h * HEAD_DIM, (h + 1) * HEAD_DIM) for h in heads]
    ones_rows = jnp.ones((SUBLANE, L), BF16)
    n_prev = [n_prev_ref[s] for s in seqs]


    qc = [[_dot(q_ref[s, :, hcols[h]], c_prev[s, h].astype(BF16)) for h in heads] for s in seqs]

    gvs, a_rows = [], []
    for s in seqs:
        gv = _gate_vectors(gates_ref[s], m_prev_ref[s, 0:1, :], L)
        m_ref[s] = jnp.broadcast_to(gv['ml'], m_ref.shape[1:])
        a_shift = gv['a'] + log_scale
        if L % LANE == 0:
            a_t = a_shift.T
            a_rows.append([a_t[h:h + 1, :] for h in heads])
        else:
            a_rows.append([jnp.sum(jnp.where(row == col, a_shift[:, h:h + 1], 0.0), axis=0,
                                   keepdims=True) for h in heads])
        gvs.append(gv)

    for s in seqs:
        hcol = lambda name, h: gvs[s][name][:, h:h + 1]
        for h in heads:
            qh, kh, vh = q_ref[s, :, hcols[h]], k_ref[s, :, hcols[h]], v_ref[s, :, hcols[h]]

            g_inter = hcol('g_inter', h)
            kg = kh * hcol('gs', h).astype(BF16)
            upd = lax.dot_general(kg, vh, (((0,), (0,)), ((), ())), preferred_element_type=F32)
            c_ref[s, h] = g_inter * c_prev[s, h] + upd
            n_ref[s, h:h + 1, :] = (g_inter * n_prev[s][h:h + 1, :]
                                    + _dot(ones_rows, kg)[0:1, :])

            w_inter = hcol('w_inter', h)
            w_intra = jnp.exp(jnp.where(causal, hcol('dcol', h) + a_rows[s][h], NEG))
            sc = lax.dot_general(qh, kh, (((1,), (1,)), ((), ())),
                                 preferred_element_type=F32) * w_intra
            num = _dot(sc.astype(BF16), vh) + qc[s][h] * w_inter
            qn = jnp.sum((qh * n_prev[s][h:h + 1, :].astype(BF16)).astype(F32), axis=1,
                         keepdims=True)
            nq = jnp.sum(sc, axis=1, keepdims=True) + w_inter * qn
            den = jnp.maximum(jnp.abs(nq), hcol('exp_neg_m', h))
            mu = jnp.mean(num, axis=-1, keepdims=True)
            hc = num - mu
            var = jnp.mean(hc * hc, axis=-1, keepdims=True)
            h_ref[s, :, hcols[h]] = (hc * lax.rsqrt(var + EPS * den * den)).astype(BF16)


def _ml_core(groups, j):
    ios = [_core_group_io(g, j) for g in groups]
    steps = {io['steps'] for io in ios}
    assert len(steps) == 1, "token groups must have the same number of grid steps"
    in_specs, args, out_specs, out_shape, aliases = [], [], [], [], {}
    for gi, io in enumerate(ios):
        if io['alias_in'] is not None:
            aliases[len(args) + io['alias_in']] = N_CORE_OUT * gi + 1
        in_specs += io['in_specs']
        args += io['args']
        out_specs += io['out_specs']
        out_shape += io['out_shape']
    outs = pl.pallas_call(
        functools.partial(_ml_core_body, cfgs=tuple(io['cfg'] for io in ios)),
        out_shape=tuple(out_shape),
        grid=(steps.pop(),),
        in_specs=in_specs,
        out_specs=tuple(out_specs),
        input_output_aliases=aliases,
        compiler_params=_cparams(1, VMEM_LIMIT_CORE),
        name="ml_core",
    )(*args)
    return [_core_group_result(g, outs[N_CORE_OUT * gi:N_CORE_OUT * (gi + 1)])
            for gi, g in enumerate(groups)]


N_CORE_OUT = 4


def _core_group_io(g, j):
    n_ml = 2
    B, nc, L, nb = g['B'], g['nc'], g['L'], g['nb']
    inner = g['q'].shape[-1]
    has_state = g['state'] is not None
    seq3 = lambda a: a.reshape(B, nc * L, a.shape[-1])
    tok = lambda w: pl.BlockSpec((nb, L, w), lambda t: (t // nc, t % nc, 0))
    per_seq = lambda shape: pl.BlockSpec((nb,) + shape, lambda t: (t // nc,) + (0,) * len(shape))
    per_seq_layer = lambda shape: pl.BlockSpec(
        (None, nb) + shape, lambda t: (j, t // nc) + (0,) * len(shape))
    c_block = per_seq_layer((N_HEADS, HEAD_DIM, HEAD_DIM))
    in_specs = [tok(inner), tok(inner), tok(inner), tok(2 * LANE)]
    args = [seq3(g['q']), seq3(g['k']), seq3(g['v']), seq3(g['gates'])]
    if has_state:
        in_specs += [c_block, per_seq_layer((N_HEADS, HEAD_DIM)), per_seq((SUBLANE, LANE))]
        args += list(g['state'])
    in_specs.append(pl.BlockSpec(memory_space=pl.ANY))
    alias_in = None
    if g['c_all'] is not None:
        args.append(g['c_all'])
        alias_in = len(args) - 1
    else:
        args.append(jnp.zeros((SUBLANE, LANE), F32))
    out_shape = [jax.ShapeDtypeStruct((B, nc * L, inner), BF16),
                 jax.ShapeDtypeStruct((n_ml, B, N_HEADS, HEAD_DIM, HEAD_DIM), F32),
                 jax.ShapeDtypeStruct((B, N_HEADS, HEAD_DIM), F32),
                 jax.ShapeDtypeStruct((B, SUBLANE, LANE), F32)]
    out_specs = [tok(inner), c_block, per_seq((N_HEADS, HEAD_DIM)), per_seq((SUBLANE, LANE))]
    return dict(in_specs=in_specs, args=args, out_specs=out_specs, out_shape=out_shape,
                cfg=(L, nb, nc, has_state), alias_in=alias_in, steps=B // nb * nc)


def _core_group_result(g, outs):
    hn, c_all, n, m = outs
    return hn.reshape(g['q'].shape), c_all, n, m


def _gelu(x):
    return 0.5 * x * (1.0 + lax.erf(x * (2.0 ** -0.5)))


def _cm_body(*refs, width, sub, emit_vn):
    if emit_vn:
        x_ref, g_ref, win_ref, bin_ref, lg_ref, wmix_ref, bmix_ref, wout_ref, o_ref, vn_ref = refs
    else:
        x_ref, g_ref, win_ref, bin_ref, lg_ref, wmix_ref, bmix_ref, wout_ref, o_ref = refs
    gd = width // CM_GROUPS
    n_sub = x_ref.shape[0] // sub

    def in_proj(s):
        x = x_ref[s * sub:(s + 1) * sub, :]
        xn = _rmsnorm(x, g_ref[...]).astype(BF16)
        v = _gelu(_dot(xn, win_ref[:, width:]) + bin_ref[:, width:])
        u = _gelu(_dot(xn, win_ref[:, :width]) + bin_ref[:, :width])
        return x, u, v

    def gate_and_out(s, x, u, v):
        rows = slice(s * sub, (s + 1) * sub)
        mu = jnp.mean(v, axis=-1, keepdims=True)
        vc = v - mu
        var = jnp.mean(vc * vc, axis=-1, keepdims=True)
        vn = vc * lax.rsqrt(var + EPS) * lg_ref[...]
        if emit_vn:
            vn_ref[rows, :] = vn
        vnb = vn.astype(BF16)
        bmix = bmix_ref[...]
        parts = []
        for g in range(CM_GROUPS):
            cols = slice(g * gd, (g + 1) * gd)
            mix = _dot(wmix_ref[g], vnb[:, cols]) + bmix[:, g:g + 1]
            parts.append((u[:, cols] * mix).astype(BF16))
        o_ref[rows, :] = x + _dot(jnp.concatenate(parts, axis=-1), wout_ref[...])

    nxt = in_proj(0)
    for s in range(n_sub):
        cur = nxt
        if s + 1 < n_sub:
            nxt = in_proj(s + 1)
        gate_and_out(s, *cur)


def _chunk_mlp(x, norm_g, layer, w_in, b_in, ln_g, wmix, bmix, w_out, j, tm, emit_vn):
    T, D = x.shape
    width = ln_g.shape[-1]
    body = functools.partial(_cm_body, width=width, sub=wmix.shape[1], emit_vn=emit_vn)
    tok = lambda w: pl.BlockSpec((tm, w), lambda i: (i, 0))
    out_shape = [jax.ShapeDtypeStruct((T, D), F32)]
    out_specs = [tok(D)]
    if emit_vn:
        out_shape.append(jax.ShapeDtypeStruct((T, width), F32))
        out_specs.append(tok(width))
    return pl.pallas_call(
        body,
        out_shape=tuple(out_shape),
        grid=(T // tm,),
        in_specs=[tok(D), _layer_spec((1, D), layer),
                  _layer_spec((D, 2 * width), j), _layer_spec((1, 2 * width), j),
                  _layer_spec((1, width), j),
                  _const_spec(wmix.shape), _const_spec(bmix.shape),
                  _layer_spec((width, D), j)],
        out_specs=tuple(out_specs),
        compiler_params=_cparams(1),
        name="chunk_mlp",
    )(x, norm_g.reshape(norm_g.shape[0], 1, D), w_in, b_in.reshape(b_in.shape[0], 1, 2 * width),
      ln_g.reshape(ln_g.shape[0], 1, width), wmix, bmix, w_out)


def _blockdiag_dense(w):
    rows = w.reshape(-1, MXU_DIM, QKV_BLOCK)
    r = lax.broadcasted_iota(jnp.int32, (MXU_DIM, MXU_DIM), 0)
    c = lax.broadcasted_iota(jnp.int32, (MXU_DIM, MXU_DIM), 1)
    spread = (lax.broadcasted_iota(jnp.int32, (QKV_BLOCK, MXU_DIM), 1) % QKV_BLOCK
              == lax.broadcasted_iota(jnp.int32, (QKV_BLOCK, MXU_DIM), 0)).astype(w.dtype)
    dense = jnp.einsum('grd,dc->grc', rows, spread, precision=lax.Precision.HIGHEST)
    return jnp.where(r // QKV_BLOCK == c // QKV_BLOCK, dense, 0)


def _gate_weights(w_ig, b_ig, w_fg, b_fg, dense_qkv):
    n_tiles = dense_qkv[0].shape[0]
    w = jnp.concatenate([w_ig, w_fg], axis=1).reshape(3, n_tiles, MXU_DIM, 2 * N_HEADS)
    folded = jnp.einsum('mgrc,mgch->mgrh', jnp.stack(dense_qkv), w,
                        precision=lax.Precision.HIGHEST)
    folded = folded.reshape(3, n_tiles * MXU_DIM, 2 * N_HEADS)
    lane = lax.broadcasted_iota(jnp.int32, (2 * N_HEADS, 2 * LANE), 1)
    src = lax.broadcasted_iota(jnp.int32, (2 * N_HEADS, 2 * LANE), 0)
    place = (lane == (src % N_HEADS) + LANE * (src // N_HEADS)).astype(w_ig.dtype)
    w_all = jnp.dot(jnp.concatenate([folded[0] + folded[1], folded[2]], axis=0), place,
                    precision=lax.Precision.HIGHEST)
    bpad = jnp.zeros((LANE - N_HEADS,), b_ig.dtype)
    b = jnp.concatenate([b_ig, bpad, b_fg, bpad])
    return w_all.astype(BF16), b.reshape(1, 2 * LANE)


def _trunks(xs, grps, p):
    depth = p['norm_ff1'].shape[0]
    sts = [dict(x=x, c_all=None, n=[], m=[], conv=[], v=[]) for x in xs]

    ffn_w = {'cur': tuple(w[0].astype(BF16) for w in p['ffn1_w'])}

    def ffn_round(norm_key, i, nxt, use_mixer=False, final_g=None):
        new_w = None
        for gi, (grp, st) in enumerate(zip(grps, sts)):
            mixer = st['mixer'] if use_mixer else None
            tm = grp['tm'] if mixer is None else grp['tm_mixer']
            cast = nxt if gi == 0 else None
            out = _ffn(st['x'], p[norm_key], ffn_w['cur'], i, tm, final_g=final_g, mixer=mixer,
                       cast=cast)
            if cast is not None:
                out, new_w = out
            st['x'] = out
        ffn_w['cur'] = new_w

    for i in range(depth):
        j = i // 2
        ffn_round('norm_ff1', i, (p['ffn2_w'], i))
        if i % 2 == 0:
            core_in = []
            for grp, st in zip(grps, sts):
                q, k, v, xc, z, gates, tail = _ml_pre(
                    st['x'], p['norm_mix'], i, p['ml_w_up'], p['ml_w_conv'], p['ml_b_conv'],
                    p['bq'][j], p['bk'][j], p['bv'][j], p['w_gates'][j], p['b_gates'][j],
                    grp['hist'][j], j, grp['pre_tm'], grp['stride'], grp['tiles_per_seq'])
                st['conv'].append(grp['conv_out'](tail))
                st['xc'], st['z'] = xc, z
                q, k, v, gates = grp['to_core'](q, k, v, gates)
                core_in.append(dict(q=q, k=k, v=v, gates=gates, B=grp['B'], nc=grp['nc'], L=grp['L'],
                                    nb=grp['nb'], state=grp['state'](j), c_all=st['c_all']))
            for grp, st, (hn, c_all, n, m) in zip(grps, sts, _ml_core(core_in, j)):
                st['c_all'] = c_all
                st['n'].append(n)
                st['m'].append(m[:, 0, :N_HEADS])
                st['mixer'] = (grp['from_core'](hn), st['xc'], st['z'], p['ml_hn_g'], p['ml_skip'],
                               p['ml_w_down'], j)
        else:
            for grp, st in zip(grps, sts):
                outs = _chunk_mlp(st['x'], p['norm_mix'], i, p['cm_w_in'], p['cm_b_in'], p['cm_ln_g'],
                                  grp['wmix'][j], grp['bmix'][j], p['cm_w_out'], j, grp['cm_tm'],
                                  emit_vn=grp['emit_vn'])
                st['x'] = outs[0]
                st['v'].append(outs[1] if grp['emit_vn'] else None)
                st['mixer'] = None
        last = i == depth - 1
        ffn_round('norm_ff2', i, None if last else (p['ffn1_w'], i + 1), use_mixer=True,
                  final_g=p['norm_final'] if last else None)
    return [(st['x'], st['c_all'], jnp.stack(st['n']), jnp.stack(st['m']), jnp.stack(st['conv']),
             st['v']) for st in sts]


def kernel(x_prompt, x_sample, state_C, state_n, state_m, state_conv, norm_ff1, norm_mix, norm_ff2, norm_final, ffn1_w_gate, ffn1_w_up, ffn1_w_down, ffn2_w_gate, ffn2_w_up, ffn2_w_down, ml_w_up, ml_w_conv, ml_b_conv, ml_w_q, ml_w_k, ml_w_v, ml_w_ig, ml_b_ig, ml_w_fg, ml_b_fg, ml_hn_g, ml_skip, ml_w_down, cm_w_in, cm_b_in, cm_ln_g, cm_w_s, cm_b_s, cm_w_out):
    bp, sp, d_model = x_prompt.shape
    bs, ss, _ = x_sample.shape
    n_ml, inner = ml_w_conv.shape[0], ml_w_conv.shape[-1]
    n_cm, width = cm_ln_g.shape
    hist_rows = CONV_W - 1
    sb = min(64, bs)

    dense_qkv = [tuple(_blockdiag_dense(w[j]) for w in (ml_w_q, ml_w_k, ml_w_v)) for j in range(n_ml)]
    gate_wb = [_gate_weights(ml_w_ig[j], ml_b_ig[j], ml_w_fg[j], ml_b_fg[j], dense_qkv[j])
               for j in range(n_ml)]
    p = {
        'norm_ff1': norm_ff1, 'norm_mix': norm_mix, 'norm_ff2': norm_ff2, 'norm_final': norm_final,
        'ffn1_w': (ffn1_w_gate, ffn1_w_up, ffn1_w_down),
        'ffn2_w': (ffn2_w_gate, ffn2_w_up, ffn2_w_down),
        'ml_w_up': ml_w_up.astype(BF16), 'ml_w_conv': ml_w_conv, 'ml_b_conv': ml_b_conv,
        'bq': [d[0].astype(BF16) for d in dense_qkv], 'bk': [d[1].astype(BF16) for d in dense_qkv],
        'bv': [d[2].astype(BF16) for d in dense_qkv],
        'w_gates': [wb[0] for wb in gate_wb], 'b_gates': [wb[1] for wb in gate_wb],
        'ml_hn_g': ml_hn_g, 'ml_skip': ml_skip, 'ml_w_down': ml_w_down.astype(BF16),
        'cm_w_in': cm_w_in.astype(BF16), 'cm_b_in': cm_b_in, 'cm_ln_g': cm_ln_g,
        'cm_w_out': cm_w_out.astype(BF16),
    }

    def mix_weights(L, kron_left):
        causal = jnp.tril(jnp.ones((L, L), dtype=bool))
        ws = jnp.where(causal, cm_w_s[:, :, :L, :L], 0)
        bs_ = cm_b_s[:, :, :L]
        if kron_left:
            reps = kron_left
            eye = jnp.eye(reps, dtype=ws.dtype)
            wm = ws[:, :, None, :, None, :] * eye[None, None, :, None, :, None]
            wm = wm.reshape(n_cm, CM_GROUPS, reps * L, reps * L)
            bm = jnp.tile(bs_, (1, 1, reps))
        else:
            eye = jnp.eye(sb, dtype=ws.dtype)
            wm = ws[:, :, :, None, :, None] * eye[None, None, None, :, None, :]
            wm = wm.reshape(n_cm, CM_GROUPS, L * sb, L * sb)
            bm = jnp.repeat(bs_, sb, axis=2)
        bm = jnp.swapaxes(bm, 1, 2)
        bm = jnp.concatenate([bm, jnp.zeros(bm.shape[:2] + (LANE - CM_GROUPS,), bm.dtype)], axis=2)
        return wm.astype(BF16), bm

    tm_p = 1024
    pre_tm_p = 512
    cm_tm_p = 1024
    chunk_p = min(256, sp)
    cm_chunk_p = min(128, sp)
    wmix_p, bmix_p = mix_weights(cm_chunk_p, MXU_DIM // cm_chunk_p)
    grp_p = {
        'tm': tm_p, 'tm_mixer': 512, 'pre_tm': pre_tm_p, 'cm_tm': cm_tm_p, 'emit_vn': False,
        'stride': 1,
        'tiles_per_seq': sp // pre_tm_p,
        'hist': [jnp.zeros((bp, SUBLANE, inner), F32)] * n_ml,
        'conv_out': lambda tail: tail.reshape(bp, sp // pre_tm_p, SUBLANE, inner)[:, -1, SUBLANE - hist_rows:, :],
        'to_core': lambda q, k, v, g: (q, k, v, g),
        'from_core': lambda hn: hn,
        'state': lambda j: None,
        'B': bp, 'nc': sp // chunk_p, 'L': chunk_p, 'nb': 1,
        'wmix': wmix_p, 'bmix': bmix_p,
    }

    ts = bs * ss
    n_sg = bs // sb
    l_pad = SUBLANE

    def tile_major(a):
        steps, feat = a.shape[1], a.shape[2]
        return a.reshape(n_sg, sb, steps, feat).transpose(0, 2, 1, 3).reshape(n_sg * steps * sb, feat)

    def batch_major(a, steps):
        feat = a.shape[-1]
        return a.reshape(n_sg, steps, sb, feat).transpose(0, 2, 1, 3).reshape(bs, steps, feat)

    def to_core(q, k, v, g):
        def seq_major(a, fill):
            a = batch_major(a, ss)
            pad = jnp.broadcast_to(fill.astype(a.dtype), (bs, l_pad - ss, a.shape[-1]))
            return jnp.concatenate([a, pad], axis=1).reshape(bs * l_pad, a.shape[-1])
        zero = jnp.zeros((inner,), F32)
        lane_pad = jnp.zeros((LANE - N_HEADS,), F32)
        gfill = jnp.concatenate([jnp.full((N_HEADS,), NEG, F32), lane_pad,
                                 jnp.full((N_HEADS,), -NEG, F32), lane_pad])
        return seq_major(q, zero), seq_major(k, zero), seq_major(v, zero), seq_major(g, gfill)

    def from_core(hn):
        return tile_major(hn.reshape(bs, l_pad, inner)[:, :ss])

    def sample_state(j):
        m0 = jnp.pad(state_m[j], ((0, 0), (0, LANE - N_HEADS)))
        return state_C, state_n, jnp.broadcast_to(m0[:, None, :], (bs, SUBLANE, LANE))

    wmix_s, bmix_s = mix_weights(ss, 0)
    tile_s = ss * sb
    hist_s = [tile_major(state_conv[j]).reshape(n_sg, hist_rows * sb, inner) for j in range(n_ml)]
    grp_s = {
        'tm': tile_s, 'tm_mixer': tile_s, 'pre_tm': tile_s, 'cm_tm': tile_s, 'emit_vn': True,
        'stride': sb,
        'tiles_per_seq': 1,
        'hist': hist_s,
        'conv_out': lambda tail: batch_major(tail.reshape(n_sg * hist_rows * sb, inner), hist_rows),
        'to_core': to_core, 'from_core': from_core, 'state': sample_state,
        'B': bs, 'nc': 1, 'L': l_pad, 'nb': 2,
        'wmix': wmix_s, 'bmix': bmix_s,
    }
    ((yp, c_p, n_p, m_p, conv_p, _), (ys, c_s, n_s, m_s, conv_s, v_rows)) = _trunks(
        [x_prompt.reshape(bp * sp, d_model), tile_major(x_sample)], [grp_p, grp_s], p)

    y_prompt = yp.reshape(bp, sp, d_model)
    y_sample = batch_major(ys, ss)
    v_sample = jnp.stack([batch_major(vr, ss) for vr in v_rows])
    return (y_prompt, y_sample, c_p, n_p, m_p, conv_p, c_s, n_s, m_s, conv_s, v_sample)
```

```python
import functools
import math

import jax
import jax.numpy as jnp
from jax import lax
from jax.experimental import pallas as pl
from jax.experimental.pallas import tpu as pltpu

F32 = jnp.float32
BF16 = jnp.bfloat16

EPS = 1e-6
N_HEADS = 4
HEAD_DIM = 512
QKV_BLOCK = 4
CONV_W = 4
CM_GROUPS = 4
NEG = -1e30

LANE = 128
SUBLANE = 8
MXU_DIM = 256
VMEM_LIMIT = 56 * 1024 * 1024
VMEM_LIMIT_CORE = 60 * 1024 * 1024


def _cparams(n_axes, vmem_limit=VMEM_LIMIT):
    return pltpu.CompilerParams(dimension_semantics=("arbitrary",) * n_axes,
                                vmem_limit_bytes=vmem_limit)


def _const_spec(shape):
    nd = len(shape)
    return pl.BlockSpec(shape, lambda *_: (0,) * nd, pipeline_mode=pl.Buffered(1))


def _layer_spec(shape, layer):
    nd = len(shape)
    return pl.BlockSpec((None,) + tuple(shape), lambda *_: (layer,) + (0,) * nd,
                        pipeline_mode=pl.Buffered(1))


def _rmsnorm(x, g):
    return x * lax.rsqrt(jnp.mean(x * x, axis=-1, keepdims=True) + EPS) * g


def _silu(x):
    return x * jax.nn.sigmoid(x)


def _dot(a, b):
    return jnp.dot(a, b, preferred_element_type=F32)


def _ffn_body(*refs, ff_chunks, final_norm, mixer_out, n_cast):
    if n_cast:
        cast_in, cast_out = refs[-2 * n_cast - 1:-n_cast - 1], refs[-n_cast:]
        refs = refs[:-2 * n_cast - 1] + (refs[-n_cast - 1],)
        for src, dst in zip(cast_in, cast_out):
            dst[...] = src[...].astype(dst.dtype)
    if mixer_out:
        (x_ref, g_ref, wg_ref, wu_ref, wd_ref, gf_ref,
         hn_ref, xc_ref, z_ref, wdm_ref, o_ref) = refs
    else:
        x_ref, g_ref, wg_ref, wu_ref, wd_ref, gf_ref, o_ref = refs
    x = x_ref[...]
    if mixer_out:
        for lo in range(0, hn_ref.shape[1], HEAD_DIM):
            cols = slice(lo, lo + HEAD_DIM)
            out = ((hn_ref[:, cols].astype(F32) + xc_ref[:, cols].astype(F32))
                   * z_ref[:, cols].astype(F32))
            x = x + _dot(out.astype(BF16), wdm_ref[cols, :])
    xn = _rmsnorm(x, g_ref[...]).astype(wg_ref.dtype)
    acc = None
    for lo, hi in ff_chunks:
        gate = _dot(xn, wg_ref[:, lo:hi])
        up = _dot(xn, wu_ref[:, lo:hi])
        h = (_silu(gate) * up).astype(wd_ref.dtype)
        part = _dot(h, wd_ref[lo:hi, :])
        acc = part if acc is None else acc + part
    y = x + 0.5 * acc
    if final_norm:
        y = _rmsnorm(y, gf_ref[...])
    o_ref[...] = y


CAST_BLOCKS = 16


def _ffn(x, norm_g, weights, layer, tm, final_g=None, mixer=None, cast=None):
    T, D = x.shape
    w_gate, w_up, w_down = weights
    d_ff = w_gate.shape[-1]
    n_steps = T // tm
    step = 4 * MXU_DIM
    ff_chunks = tuple((lo, min(lo + step, d_ff)) for lo in range(0, d_ff, step))
    final_norm = final_g is not None
    gf = final_g if final_norm else norm_g[layer]
    n_cast = 0 if cast is None else len(cast[0])
    body = functools.partial(_ffn_body, ff_chunks=ff_chunks, final_norm=final_norm,
                             mixer_out=mixer is not None, n_cast=n_cast)
    tok = lambda w: pl.BlockSpec((tm, w), lambda i: (i, 0))
    in_specs = [tok(D), _layer_spec((1, D), layer), _const_spec((D, d_ff)),
                _const_spec((D, d_ff)), _const_spec((d_ff, D)), _const_spec((1, D))]
    args = [x, norm_g.reshape(norm_g.shape[0], 1, D), w_gate, w_up, w_down, gf.reshape(1, D)]
    if mixer is not None:
        hn, xc, z, w_down_m, j = mixer
        inner = hn.shape[-1]
        in_specs += [tok(inner), tok(inner), tok(inner), _layer_spec((inner, D), j)]
        args += [hn, xc, z, w_down_m]
    out_shape = [jax.ShapeDtypeStruct((T, D), F32)]
    out_specs = [tok(D)]
    if cast is not None:
        mats, cast_layer = cast
        assert n_steps % CAST_BLOCKS == 0
        per_block = n_steps // CAST_BLOCKS
        for w in mats:
            rows, cols = w.shape[1] // CAST_BLOCKS, w.shape[2]
            in_specs.append(pl.BlockSpec((None, rows, cols), lambda i: (cast_layer, i // per_block, 0)))
            args.append(w)
            out_shape.append(jax.ShapeDtypeStruct(w.shape[1:], BF16))
            out_specs.append(pl.BlockSpec((rows, cols), lambda i: (i // per_block, 0)))
    outs = pl.pallas_call(
        body,
        out_shape=tuple(out_shape),
        grid=(n_steps,),
        in_specs=in_specs,
        out_specs=tuple(out_specs),
        compiler_params=_cparams(1, VMEM_LIMIT if cast is None else VMEM_LIMIT_CORE),
        name="ffn",
    )(*args)
    return outs[0] if cast is None else (outs[0], tuple(outs[1:]))


def _ml_pre_body(x_ref, g_ref, wup_ref, wc_ref, bc_ref, sk_ref, bq_ref, bk_ref, bv_ref, wgt_ref,
                 bgt_ref, hist_ref, q_ref, k_ref, v_ref, xc_ref, z_ref, gates_ref, tail_ref, xbuf,
                 *, tm, inner, header, stride, tiles_per_seq):
    i = pl.program_id(0)

    n_groups = inner // MXU_DIM

    @pl.when(i % tiles_per_seq == 0)
    def _():
        for g in range(n_groups):
            xbuf[g, 0:header, :] = hist_ref[:, g * MXU_DIM:(g + 1) * MXU_DIM]

    xn = _rmsnorm(x_ref[...], g_ref[...]).astype(BF16)

    def up_proj(g):
        cols = slice(g * MXU_DIM, (g + 1) * MXU_DIM)
        zcols = slice(inner + g * MXU_DIM, inner + (g + 1) * MXU_DIM)
        xm_g = _dot(xn, wup_ref[:, cols])
        z_ref[:, cols] = _silu(_dot(xn, wup_ref[:, zcols])).astype(BF16)
        return xm_g

    def gate_part(g, xcb, xmb):
        rows = lambda base: slice(base + g * MXU_DIM, base + (g + 1) * MXU_DIM)
        return _dot(xcb, wgt_ref[rows(0), :]) + _dot(xmb, wgt_ref[rows(inner), :])

    gacc = bgt_ref[...]
    prev_act = None
    xm_next = up_proj(0)
    for g in range(n_groups):
        cols = slice(g * MXU_DIM, (g + 1) * MXU_DIM)
        xm = xm_next
        if g + 1 < n_groups:
            xm_next = up_proj(g + 1)
        xbuf[g, header:header + tm, :] = xm
        acc = bc_ref[:, cols] + wc_ref[CONV_W - 1:CONV_W, cols] * xm
        for d in range(1, CONV_W):
            lo = header - d * stride
            acc = acc + wc_ref[CONV_W - 1 - d:CONV_W - d, cols] * xbuf[g, lo:lo + tm, :]
        xc = _silu(acc)
        xc_ref[:, cols] = (sk_ref[:, cols] * xc).astype(BF16)
        xcb = xc.astype(BF16)
        xmb = xm.astype(BF16)
        q_ref[:, cols] = _dot(xcb, bq_ref[g]).astype(BF16)
        k_ref[:, cols] = _dot(xcb, bk_ref[g]).astype(BF16)
        v_ref[:, cols] = _dot(xmb, bv_ref[g]).astype(BF16)
        if g > 0:
            gacc = gacc + gate_part(g - 1, *prev_act)
        prev_act = (xcb, xmb)
        last = xbuf[g, tm:tm + header, :]
        tail_ref[:, cols] = last
        xbuf[g, 0:header, :] = last
    gates_ref[...] = gacc + gate_part(n_groups - 1, *prev_act)


def _ml_pre(x, norm_g, layer, w_up, w_conv, b_conv, skip, bq, bk, bv, w_gates, b_gates, hist, j,
            tm, stride, tiles_per_seq):
    T, D = x.shape
    inner = w_conv.shape[-1]
    header = hist.shape[1]
    n_tiles = T // tm
    tok = lambda w: pl.BlockSpec((tm, w), lambda i: (i, 0))
    in_specs = [
        tok(D),
        _layer_spec((1, D), layer),
        _layer_spec((D, 2 * inner), j),
        _layer_spec((CONV_W, inner), j),
        _layer_spec((1, inner), j),
        _layer_spec((1, inner), j),
        _const_spec(bq.shape), _const_spec(bk.shape), _const_spec(bv.shape),
        _const_spec(w_gates.shape), _const_spec(b_gates.shape),
        pl.BlockSpec((None, header, inner), lambda i: (i // tiles_per_seq, 0, 0)),
    ]
    args = [x, norm_g.reshape(norm_g.shape[0], 1, D), w_up, w_conv,
            b_conv.reshape(b_conv.shape[0], 1, inner), skip.reshape(skip.shape[0], 1, inner),
            bq, bk, bv, w_gates, b_gates, hist]
    out_shape = [
        jax.ShapeDtypeStruct((T, inner), BF16),
        jax.ShapeDtypeStruct((T, inner), BF16),
        jax.ShapeDtypeStruct((T, inner), BF16),
        jax.ShapeDtypeStruct((T, inner), BF16),
        jax.ShapeDtypeStruct((T, inner), BF16),
        jax.ShapeDtypeStruct((T, 2 * LANE), F32),
        jax.ShapeDtypeStruct((n_tiles, header, inner), F32),
    ]
    out_specs = [tok(inner), tok(inner), tok(inner), tok(inner), tok(inner), tok(2 * LANE),
                 pl.BlockSpec((None, header, inner), lambda i: (i, 0, 0))]
    body = functools.partial(_ml_pre_body, tm=tm, inner=inner, header=header, stride=stride,
                             tiles_per_seq=tiles_per_seq)
    return pl.pallas_call(
        body,
        out_shape=tuple(out_shape),
        grid=(n_tiles,),
        in_specs=in_specs,
        out_specs=tuple(out_specs),
        scratch_shapes=[pltpu.VMEM((inner // MXU_DIM, header + tm, MXU_DIM), F32)],
        compiler_params=_cparams(1),
        name="ml_pre",
    )(*args)


def _log_sigmoid(x):
    return jnp.minimum(x, 0.0) - jnp.log(1.0 + jnp.exp(-jnp.abs(x)))


def _token_scan(x, op, fill):
    n = x.shape[0]
    tok = lax.broadcasted_iota(jnp.int32, x.shape, 0)
    d = 1
    while d < n:
        x = op(x, jnp.where(tok >= d, pltpu.roll(x, d, axis=0), fill))
        d *= 2
    return x


def _gate_vectors(gates, m_prev, L):
    ig = gates[:, :LANE]
    bt = _token_scan(_log_sigmoid(gates[:, LANE:]), jnp.add, 0.0)
    a = ig - bt
    m = bt + jnp.maximum(_token_scan(a, jnp.maximum, NEG), m_prev)
    dcol = bt - m
    bl, ml = bt[L - 1:L, :], m[L - 1:L, :]
    return dict(a=a, dcol=dcol, w_inter=jnp.exp(m_prev + dcol), exp_neg_m=jnp.exp(-m),
                gs=jnp.exp(bl - bt + ig - ml) * HEAD_DIM ** -0.5,
                g_inter=jnp.exp(m_prev + bl - ml), ml=ml)


def _ml_core_body(*refs, cfgs):
    n_in = [9 if has_state else 6 for (_, _, _, has_state) in cfgs]
    pos_out = sum(n_in)
    pos_in = 0
    for (L, nb, nc, has_state), k_in in zip(cfgs, n_in):
        _ml_core_group(refs[pos_in:pos_in + k_in], refs[pos_out:pos_out + 4], L, nb, nc, has_state)
        pos_in += k_in
        pos_out += 4


def _ml_core_group(in_refs, out_refs, L, nb, nc, has_state):
    h_ref, c_ref, n_ref, m_ref = out_refs
    if has_state:
        q_ref, k_ref, v_ref, gates_ref, gain_ref, c0_ref, n0_ref, m0_ref, _ = in_refs
    else:
        q_ref, k_ref, v_ref, gates_ref, gain_ref, _ = in_refs
    if has_state and nc == 1:
        c_prev, n_prev_ref, m_prev_ref = c0_ref, n0_ref, m0_ref
    else:
        c_prev, n_prev_ref, m_prev_ref = c_ref, n_ref, m_ref

        @pl.when(pl.program_id(0) % nc == 0)
        def _():
            if has_state:
                c_ref[...] = c0_ref[...]
                n_ref[...] = n0_ref[...]
                m_ref[...] = m0_ref[...]
            else:
                c_ref[...] = jnp.zeros_like(c_ref)
                n_ref[...] = jnp.zeros_like(n_ref)
                m_ref[...] = jnp.zeros_like(m_ref)

    log_scale = -0.5 * math.log(HEAD_DIM)
    row = lax.broadcasted_iota(jnp.int32, (L, L), 0)
    col = lax.broadcasted_iota(jnp.int32, (L, L), 1)
    causal = col <= row
    heads = range(N_HEADS)
    seqs = range(nb)
    hcols = [slice(h * HEAD_DIM, (h + 1) * HEAD_DIM) for h in heads]
    ones_rows = jnp.ones((SUBLANE, L), BF16)
    n_prev = [n_prev_ref[s] for s in seqs]


    qc = [[_dot(q_ref[s, :, hcols[h]], c_prev[s, h].astype(BF16)) for h in heads] for s in seqs]

    gvs, a_rows = [], []
    for s in seqs:
        gv = _gate_vectors(gates_ref[s], m_prev_ref[s, 0:1, :], L)
        m_ref[s] = jnp.broadcast_to(gv['ml'], m_ref.shape[1:])
        a_shift = gv['a'] + log_scale
        if L % LANE == 0:
            a_t = a_shift.T
            a_rows.append([a_t[h:h + 1, :] for h in heads])
        else:
            a_rows.append([jnp.sum(jnp.where(row == col, a_shift[:, h:h + 1], 0.0), axis=0,
                                   keepdims=True) for h in heads])
        gvs.append(gv)

    for s in seqs:
        hcol = lambda name, h: gvs[s][name][:, h:h + 1]
        for h in heads:
            qh, kh, vh = q_ref[s, :, hcols[h]], k_ref[s, :, hcols[h]], v_ref[s, :, hcols[h]]

            g_inter = hcol('g_inter', h)
            kg = kh * hcol('gs', h).astype(BF16)
            upd = lax.dot_general(kg, vh, (((0,), (0,)), ((), ())), preferred_element_type=F32)
            c_ref[s, h] = g_inter * c_prev[s, h] + upd
            n_ref[s, h:h + 1, :] = (g_inter * n_prev[s][h:h + 1, :]
                                    + _dot(ones_rows, kg)[0:1, :])

            w_inter = hcol('w_inter', h)
            w_intra = jnp.exp(jnp.where(causal, hcol('dcol', h) + a_rows[s][h], NEG))
            sc = lax.dot_general(qh, kh, (((1,), (1,)), ((), ())),
                                 preferred_element_type=F32) * w_intra
            num = _dot(sc.astype(BF16), vh) + qc[s][h] * w_inter
            qn = jnp.sum((qh * n_prev[s][h:h + 1, :].astype(BF16)).astype(F32), axis=1,
                         keepdims=True)
            nq = jnp.sum(sc, axis=1, keepdims=True) + w_inter * qn
            den = jnp.maximum(jnp.abs(nq), hcol('exp_neg_m', h))
            mu = jnp.mean(num, axis=-1, keepdims=True)
            hc = num - mu
            var = jnp.mean(hc * hc, axis=-1, keepdims=True)
            h_ref[s, :, hcols[h]] = (hc * lax.rsqrt(var + EPS * den * den)
                                     * gain_ref[:, hcols[h]]).astype(BF16)


def _ml_core(groups, j):
    ios = [_core_group_io(g, j) for g in groups]
    steps = {io['steps'] for io in ios}
    assert len(steps) == 1, "token groups must have the same number of grid steps"
    in_specs, args, out_specs, out_shape, aliases = [], [], [], [], {}
    for gi, io in enumerate(ios):
        if io['alias_in'] is not None:
            aliases[len(args) + io['alias_in']] = N_CORE_OUT * gi + 1
        in_specs += io['in_specs']
        args += io['args']
        out_specs += io['out_specs']
        out_shape += io['out_shape']
    outs = pl.pallas_call(
        functools.partial(_ml_core_body, cfgs=tuple(io['cfg'] for io in ios)),
        out_shape=tuple(out_shape),
        grid=(steps.pop(),),
        in_specs=in_specs,
        out_specs=tuple(out_specs),
        input_output_aliases=aliases,
        compiler_params=_cparams(1, VMEM_LIMIT_CORE),
        name="ml_core",
    )(*args)
    return [_core_group_result(g, outs[N_CORE_OUT * gi:N_CORE_OUT * (gi + 1)])
            for gi, g in enumerate(groups)]


N_CORE_OUT = 4


def _core_group_io(g, j):
    n_ml = 2
    B, nc, L, nb = g['B'], g['nc'], g['L'], g['nb']
    inner = g['q'].shape[-1]
    has_state = g['state'] is not None
    seq3 = lambda a: a.reshape(B, nc * L, a.shape[-1])
    tok = lambda w: pl.BlockSpec((nb, L, w), lambda t: (t // nc, t % nc, 0))
    per_seq = lambda shape: pl.BlockSpec((nb,) + shape, lambda t: (t // nc,) + (0,) * len(shape))
    per_seq_layer = lambda shape: pl.BlockSpec(
        (None, nb) + shape, lambda t: (j, t // nc) + (0,) * len(shape))
    c_block = per_seq_layer((N_HEADS, HEAD_DIM, HEAD_DIM))
    in_specs = [tok(inner), tok(inner), tok(inner), tok(2 * LANE), _layer_spec((1, inner), j)]
    args = [seq3(g['q']), seq3(g['k']), seq3(g['v']), seq3(g['gates']),
            g['gain'].reshape(g['gain'].shape[0], 1, inner)]
    if has_state:
        in_specs += [c_block, per_seq_layer((N_HEADS, HEAD_DIM)), per_seq((SUBLANE, LANE))]
        args += list(g['state'])
    in_specs.append(pl.BlockSpec(memory_space=pl.ANY))
    alias_in = None
    if g['c_all'] is not None:
        args.append(g['c_all'])
        alias_in = len(args) - 1
    else:
        args.append(jnp.zeros((SUBLANE, LANE), F32))
    out_shape = [jax.ShapeDtypeStruct((B, nc * L, inner), BF16),
                 jax.ShapeDtypeStruct((n_ml, B, N_HEADS, HEAD_DIM, HEAD_DIM), F32),
                 jax.ShapeDtypeStruct((B, N_HEADS, HEAD_DIM), F32),
                 jax.ShapeDtypeStruct((B, SUBLANE, LANE), F32)]
    out_specs = [tok(inner), c_block, per_seq((N_HEADS, HEAD_DIM)), per_seq((SUBLANE, LANE))]
    return dict(in_specs=in_specs, args=args, out_specs=out_specs, out_shape=out_shape,
                cfg=(L, nb, nc, has_state), alias_in=alias_in, steps=B // nb * nc)


def _core_group_result(g, outs):
    hn, c_all, n, m = outs
    return hn.reshape(g['q'].shape), c_all, n, m


def _gelu(x):
    return 0.5 * x * (1.0 + lax.erf(x * (2.0 ** -0.5)))


def _cm_body(*refs, width, sub, emit_vn):
    if emit_vn:
        x_ref, g_ref, win_ref, bin_ref, lg_ref, wmix_ref, bmix_ref, wout_ref, o_ref, vn_ref = refs
    else:
        x_ref, g_ref, win_ref, bin_ref, lg_ref, wmix_ref, bmix_ref, wout_ref, o_ref = refs
    gd = width // CM_GROUPS
    n_sub = x_ref.shape[0] // sub

    def in_proj(s):
        x = x_ref[s * sub:(s + 1) * sub, :]
        xn = _rmsnorm(x, g_ref[...]).astype(BF16)
        v = _gelu(_dot(xn, win_ref[:, width:]) + bin_ref[:, width:])
        u = _gelu(_dot(xn, win_ref[:, :width]) + bin_ref[:, :width])
        return x, u, v

    def gate_and_out(s, x, u, v):
        rows = slice(s * sub, (s + 1) * sub)
        mu = jnp.mean(v, axis=-1, keepdims=True)
        vc = v - mu
        var = jnp.mean(vc * vc, axis=-1, keepdims=True)
        vn = vc * lax.rsqrt(var + EPS) * lg_ref[...]
        if emit_vn:
            vn_ref[rows, :] = vn
        vnb = vn.astype(BF16)
        bmix = bmix_ref[...]
        parts = []
        for g in range(CM_GROUPS):
            cols = slice(g * gd, (g + 1) * gd)
            mix = _dot(wmix_ref[g], vnb[:, cols]) + bmix[:, g:g + 1]
            parts.append((u[:, cols] * mix).astype(BF16))
        o_ref[rows, :] = x + _dot(jnp.concatenate(parts, axis=-1), wout_ref[...])

    nxt = in_proj(0)
    for s in range(n_sub):
        cur = nxt
        if s + 1 < n_sub:
            nxt = in_proj(s + 1)
        gate_and_out(s, *cur)


def _chunk_mlp(x, norm_g, layer, w_in, b_in, ln_g, wmix, bmix, w_out, j, tm, emit_vn):
    T, D = x.shape
    width = ln_g.shape[-1]
    body = functools.partial(_cm_body, width=width, sub=wmix.shape[1], emit_vn=emit_vn)
    tok = lambda w: pl.BlockSpec((tm, w), lambda i: (i, 0))
    out_shape = [jax.ShapeDtypeStruct((T, D), F32)]
    out_specs = [tok(D)]
    if emit_vn:
        out_shape.append(jax.ShapeDtypeStruct((T, width), F32))
        out_specs.append(tok(width))
    return pl.pallas_call(
        body,
        out_shape=tuple(out_shape),
        grid=(T // tm,),
        in_specs=[tok(D), _layer_spec((1, D), layer),
                  _layer_spec((D, 2 * width), j), _layer_spec((1, 2 * width), j),
                  _layer_spec((1, width), j),
                  _const_spec(wmix.shape), _const_spec(bmix.shape),
                  _layer_spec((width, D), j)],
        out_specs=tuple(out_specs),
        compiler_params=_cparams(1),
        name="chunk_mlp",
    )(x, norm_g.reshape(norm_g.shape[0], 1, D), w_in, b_in.reshape(b_in.shape[0], 1, 2 * width),
      ln_g.reshape(ln_g.shape[0], 1, width), wmix, bmix, w_out)


def _blockdiag_dense(w):
    rows = w.reshape(-1, MXU_DIM, QKV_BLOCK)
    r = lax.broadcasted_iota(jnp.int32, (MXU_DIM, MXU_DIM), 0)
    c = lax.broadcasted_iota(jnp.int32, (MXU_DIM, MXU_DIM), 1)
    spread = (lax.broadcasted_iota(jnp.int32, (QKV_BLOCK, MXU_DIM), 1) % QKV_BLOCK
              == lax.broadcasted_iota(jnp.int32, (QKV_BLOCK, MXU_DIM), 0)).astype(w.dtype)
    dense = jnp.einsum('grd,dc->grc', rows, spread, precision=lax.Precision.HIGHEST)
    return jnp.where(r // QKV_BLOCK == c // QKV_BLOCK, dense, 0)


def _gate_weights(w_ig, b_ig, w_fg, b_fg, dense_qkv):
    n_tiles = dense_qkv[0].shape[0]
    w = jnp.concatenate([w_ig, w_fg], axis=1).reshape(3, n_tiles, MXU_DIM, 2 * N_HEADS)
    folded = jnp.einsum('mgrc,mgch->mgrh', jnp.stack(dense_qkv), w,
                        precision=lax.Precision.HIGHEST)
    folded = folded.reshape(3, n_tiles * MXU_DIM, 2 * N_HEADS)
    lane = lax.broadcasted_iota(jnp.int32, (2 * N_HEADS, 2 * LANE), 1)
    src = lax.broadcasted_iota(jnp.int32, (2 * N_HEADS, 2 * LANE), 0)
    place = (lane == (src % N_HEADS) + LANE * (src // N_HEADS)).astype(w_ig.dtype)
    w_all = jnp.dot(jnp.concatenate([folded[0] + folded[1], folded[2]], axis=0), place,
                    precision=lax.Precision.HIGHEST)
    bpad = jnp.zeros((LANE - N_HEADS,), b_ig.dtype)
    b = jnp.concatenate([b_ig, bpad, b_fg, bpad])
    return w_all.astype(BF16), b.reshape(1, 2 * LANE)


def _trunks(xs, grps, p):
    depth = p['norm_ff1'].shape[0]
    sts = [dict(x=x, c_all=None, n=[], m=[], conv=[], v=[]) for x in xs]

    ffn_w = {'cur': tuple(w[0].astype(BF16) for w in p['ffn1_w'])}

    def ffn_round(norm_key, i, nxt, use_mixer=False, final_g=None):
        new_w = None
        for gi, (grp, st) in enumerate(zip(grps, sts)):
            mixer = st['mixer'] if use_mixer else None
            tm = grp['tm'] if mixer is None else grp['tm_mixer']
            cast = nxt if gi == 0 else None
            out = _ffn(st['x'], p[norm_key], ffn_w['cur'], i, tm, final_g=final_g, mixer=mixer,
                       cast=cast)
            if cast is not None:
                out, new_w = out
            st['x'] = out
        ffn_w['cur'] = new_w

    for i in range(depth):
        j = i // 2
        ffn_round('norm_ff1', i, (p['ffn2_w'], i))
        if i % 2 == 0:
            core_in = []
            for grp, st in zip(grps, sts):
                q, k, v, xc, z, gates, tail = _ml_pre(
                    st['x'], p['norm_mix'], i, p['ml_w_up'], p['ml_w_conv'], p['ml_b_conv'],
                    p['ml_skip'], p['bq'][j], p['bk'][j], p['bv'][j], p['w_gates'][j],
                    p['b_gates'][j], grp['hist'][j], j, grp['pre_tm'], grp['stride'],
                    grp['tiles_per_seq'])
                st['conv'].append(grp['conv_out'](tail))
                st['xc'], st['z'] = xc, z
                q, k, v, gates = grp['to_core'](q, k, v, gates)
                core_in.append(dict(q=q, k=k, v=v, gates=gates, gain=p['ml_hn_g'], B=grp['B'],
                                    nc=grp['nc'], L=grp['L'], nb=grp['nb'], state=grp['state'](j),
                                    c_all=st['c_all']))
            for grp, st, (hn, c_all, n, m) in zip(grps, sts, _ml_core(core_in, j)):
                st['c_all'] = c_all
                st['n'].append(n)
                st['m'].append(m[:, 0, :N_HEADS])
                st['mixer'] = (grp['from_core'](hn), st['xc'], st['z'], p['ml_w_down'], j)
        else:
            for grp, st in zip(grps, sts):
                outs = _chunk_mlp(st['x'], p['norm_mix'], i, p['cm_w_in'], p['cm_b_in'], p['cm_ln_g'],
                                  grp['wmix'][j], grp['bmix'][j], p['cm_w_out'], j, grp['cm_tm'],
                                  emit_vn=grp['emit_vn'])
                st['x'] = outs[0]
                st['v'].append(outs[1] if grp['emit_vn'] else None)
                st['mixer'] = None
        last = i == depth - 1
        ffn_round('norm_ff2', i, None if last else (p['ffn1_w'], i + 1), use_mixer=True,
                  final_g=p['norm_final'] if last else None)
    return [(st['x'], st['c_all'], jnp.stack(st['n']), jnp.stack(st['m']), jnp.stack(st['conv']),
             st['v']) for st in sts]


def kernel(x_prompt, x_sample, state_C, state_n, state_m, state_conv, norm_ff1, norm_mix, norm_ff2, norm_final, ffn1_w_gate, ffn1_w_up, ffn1_w_down, ffn2_w_gate, ffn2_w_up, ffn2_w_down, ml_w_up, ml_w_conv, ml_b_conv, ml_w_q, ml_w_k, ml_w_v, ml_w_ig, ml_b_ig, ml_w_fg, ml_b_fg, ml_hn_g, ml_skip, ml_w_down, cm_w_in, cm_b_in, cm_ln_g, cm_w_s, cm_b_s, cm_w_out):
    bp, sp, d_model = x_prompt.shape
    bs, ss, _ = x_sample.shape
    n_ml, inner = ml_w_conv.shape[0], ml_w_conv.shape[-1]
    n_cm, width = cm_ln_g.shape
    hist_rows = CONV_W - 1
    sb = min(64, bs)

    dense_qkv = [tuple(_blockdiag_dense(w[j]) for w in (ml_w_q, ml_w_k, ml_w_v)) for j in range(n_ml)]
    gate_wb = [_gate_weights(ml_w_ig[j], ml_b_ig[j], ml_w_fg[j], ml_b_fg[j], dense_qkv[j])
               for j in range(n_ml)]
    p = {
        'norm_ff1': norm_ff1, 'norm_mix': norm_mix, 'norm_ff2': norm_ff2, 'norm_final': norm_final,
        'ffn1_w': (ffn1_w_gate, ffn1_w_up, ffn1_w_down),
        'ffn2_w': (ffn2_w_gate, ffn2_w_up, ffn2_w_down),
        'ml_w_up': ml_w_up.astype(BF16), 'ml_w_conv': ml_w_conv, 'ml_b_conv': ml_b_conv,
        'bq': [d[0].astype(BF16) for d in dense_qkv], 'bk': [d[1].astype(BF16) for d in dense_qkv],
        'bv': [d[2].astype(BF16) for d in dense_qkv],
        'w_gates': [wb[0] for wb in gate_wb], 'b_gates': [wb[1] for wb in gate_wb],
        'ml_hn_g': ml_hn_g, 'ml_skip': ml_skip, 'ml_w_down': ml_w_down.astype(BF16),
        'cm_w_in': cm_w_in.astype(BF16), 'cm_b_in': cm_b_in, 'cm_ln_g': cm_ln_g,
        'cm_w_out': cm_w_out.astype(BF16),
    }

    def mix_weights(L, kron_left):
        causal = jnp.tril(jnp.ones((L, L), dtype=bool))
        ws = jnp.where(causal, cm_w_s[:, :, :L, :L], 0)
        bs_ = cm_b_s[:, :, :L]
        if kron_left:
            reps = kron_left
            eye = jnp.eye(reps, dtype=ws.dtype)
            wm = ws[:, :, None, :, None, :] * eye[None, None, :, None, :, None]
            wm = wm.reshape(n_cm, CM_GROUPS, reps * L, reps * L)
            bm = jnp.tile(bs_, (1, 1, reps))
        else:
            eye = jnp.eye(sb, dtype=ws.dtype)
            wm = ws[:, :, :, None, :, None] * eye[None, None, None, :, None, :]
            wm = wm.reshape(n_cm, CM_GROUPS, L * sb, L * sb)
            bm = jnp.repeat(bs_, sb, axis=2)
        bm = jnp.swapaxes(bm, 1, 2)
        bm = jnp.concatenate([bm, jnp.zeros(bm.shape[:2] + (LANE - CM_GROUPS,), bm.dtype)], axis=2)
        return wm.astype(BF16), bm

    tm_p = 1024
    pre_tm_p = 512
    cm_tm_p = 1024
    chunk_p = min(256, sp)
    cm_chunk_p = min(128, sp)
    wmix_p, bmix_p = mix_weights(cm_chunk_p, MXU_DIM // cm_chunk_p)
    grp_p = {
        'tm': tm_p, 'tm_mixer': 512, 'pre_tm': pre_tm_p, 'cm_tm': cm_tm_p, 'emit_vn': False,
        'stride': 1,
        'tiles_per_seq': sp // pre_tm_p,
        'hist': [jnp.zeros((bp, SUBLANE, inner), F32)] * n_ml,
        'conv_out': lambda tail: tail.reshape(bp, sp // pre_tm_p, SUBLANE, inner)[:, -1, SUBLANE - hist_rows:, :],
        'to_core': lambda q, k, v, g: (q, k, v, g),
        'from_core': lambda hn: hn,
        'state': lambda j: None,
        'B': bp, 'nc': sp // chunk_p, 'L': chunk_p, 'nb': 1,
        'wmix': wmix_p, 'bmix': bmix_p,
    }

    ts = bs * ss
    n_sg = bs // sb
    l_pad = SUBLANE

    def tile_major(a):
        steps, feat = a.shape[1], a.shape[2]
        return a.reshape(n_sg, sb, steps, feat).transpose(0, 2, 1, 3).reshape(n_sg * steps * sb, feat)

    def batch_major(a, steps):
        feat = a.shape[-1]
        return a.reshape(n_sg, steps, sb, feat).transpose(0, 2, 1, 3).reshape(bs, steps, feat)

    def to_core(q, k, v, g):
        def seq_major(a, fill):
            a = batch_major(a, ss)
            pad = jnp.broadcast_to(fill.astype(a.dtype), (bs, l_pad - ss, a.shape[-1]))
            return jnp.concatenate([a, pad], axis=1).reshape(bs * l_pad, a.shape[-1])
        zero = jnp.zeros((inner,), F32)
        lane_pad = jnp.zeros((LANE - N_HEADS,), F32)
        gfill = jnp.concatenate([jnp.full((N_HEADS,), NEG, F32), lane_pad,
                                 jnp.full((N_HEADS,), -NEG, F32), lane_pad])
        return seq_major(q, zero), seq_major(k, zero), seq_major(v, zero), seq_major(g, gfill)

    def from_core(hn):
        return tile_major(hn.reshape(bs, l_pad, inner)[:, :ss])

    def sample_state(j):
        m0 = jnp.pad(state_m[j], ((0, 0), (0, LANE - N_HEADS)))
        return state_C, state_n, jnp.broadcast_to(m0[:, None, :], (bs, SUBLANE, LANE))

    wmix_s, bmix_s = mix_weights(ss, 0)
    tile_s = ss * sb
    hist_s = [tile_major(state_conv[j]).reshape(n_sg, hist_rows * sb, inner) for j in range(n_ml)]
    grp_s = {
        'tm': tile_s, 'tm_mixer': tile_s, 'pre_tm': tile_s, 'cm_tm': tile_s, 'emit_vn': True,
        'stride': sb,
        'tiles_per_seq': 1,
        'hist': hist_s,
        'conv_out': lambda tail: batch_major(tail.reshape(n_sg * hist_rows * sb, inner), hist_rows),
        'to_core': to_core, 'from_core': from_core, 'state': sample_state,
        'B': bs, 'nc': 1, 'L': l_pad, 'nb': 2,
        'wmix': wmix_s, 'bmix': bmix_s,
    }
    ((yp, c_p, n_p, m_p, conv_p, _), (ys, c_s, n_s, m_s, conv_s, v_rows)) = _trunks(
        [x_prompt.reshape(bp * sp, d_model), tile_major(x_sample)], [grp_p, grp_s], p)

    y_prompt = yp.reshape(bp, sp, d_model)
    y_sample = batch_major(ys, ss)
    v_sample = jnp.stack([batch_major(vr, ss) for vr in v_rows])
    return (y_prompt, y_sample, c_p, n_p, m_p, conv_p, c_s, n_s, m_s, conv_s, v_sample)
```

```python
import functools
import math

import jax
import jax.numpy as jnp
from jax import lax
from jax.experimental import pallas as pl
from jax.experimental.pallas import tpu as pltpu

F32 = jnp.float32
BF16 = jnp.bfloat16

EPS = 1e-6
N_HEADS = 4
HEAD_DIM = 512
QKV_BLOCK = 4
CONV_W = 4
CM_GROUPS = 4
NEG = -1e30

LANE = 128
SUBLANE = 8
MXU_DIM = 256
VMEM_LIMIT = 56 * 1024 * 1024
VMEM_LIMIT_CORE = 60 * 1024 * 1024


def _cparams(n_axes, vmem_limit=VMEM_LIMIT):
    return pltpu.CompilerParams(dimension_semantics=("arbitrary",) * n_axes,
                                vmem_limit_bytes=vmem_limit)


def _const_spec(shape):
    nd = len(shape)
    return pl.BlockSpec(shape, lambda *_: (0,) * nd, pipeline_mode=pl.Buffered(1))


def _layer_spec(shape, layer):
    nd = len(shape)
    return pl.BlockSpec((None,) + tuple(shape), lambda *_: (layer,) + (0,) * nd,
                        pipeline_mode=pl.Buffered(1))


def _rmsnorm(x, g):
    return x * lax.rsqrt(jnp.mean(x * x, axis=-1, keepdims=True) + EPS) * g


def _silu(x):
    return x * jax.nn.sigmoid(x)


def _dot(a, b):
    return jnp.dot(a, b, preferred_element_type=F32)


def _ffn_body(*refs, ff_chunks, final_norm, mixer_out, n_cast):
    if n_cast:
        cast_in, cast_out = refs[-2 * n_cast - 1:-n_cast - 1], refs[-n_cast:]
        refs = refs[:-2 * n_cast - 1] + (refs[-n_cast - 1],)
        for src, dst in zip(cast_in, cast_out):
            dst[...] = src[...].astype(dst.dtype)
    if mixer_out:
        (x_ref, g_ref, wg_ref, wu_ref, wd_ref, gf_ref,
         hn_ref, xc_ref, z_ref, wdm_ref, o_ref) = refs
    else:
        x_ref, g_ref, wg_ref, wu_ref, wd_ref, gf_ref, o_ref = refs
    x = x_ref[...]
    if mixer_out:
        for lo in range(0, hn_ref.shape[1], HEAD_DIM):
            cols = slice(lo, lo + HEAD_DIM)
            out = ((hn_ref[:, cols].astype(F32) + xc_ref[:, cols].astype(F32))
                   * z_ref[:, cols].astype(F32))
            x = x + _dot(out.astype(BF16), wdm_ref[cols, :])
    xn = _rmsnorm(x, g_ref[...]).astype(wg_ref.dtype)
    acc = None
    for lo, hi in ff_chunks:
        gate = _dot(xn, wg_ref[:, lo:hi])
        up = _dot(xn, wu_ref[:, lo:hi])
        h = (_silu(gate) * up).astype(wd_ref.dtype)
        part = _dot(h, wd_ref[lo:hi, :])
        acc = part if acc is None else acc + part
    y = x + 0.5 * acc
    if final_norm:
        y = _rmsnorm(y, gf_ref[...])
    o_ref[...] = y


CAST_BLOCKS = 16


def _ffn(x, norm_g, weights, layer, tm, final_g=None, mixer=None, cast=None):
    T, D = x.shape
    w_gate, w_up, w_down = weights
    d_ff = w_gate.shape[-1]
    n_steps = T // tm
    step = 4 * MXU_DIM
    ff_chunks = tuple((lo, min(lo + step, d_ff)) for lo in range(0, d_ff, step))
    final_norm = final_g is not None
    gf = final_g if final_norm else norm_g[layer]
    n_cast = 0 if cast is None else len(cast[0])
    body = functools.partial(_ffn_body, ff_chunks=ff_chunks, final_norm=final_norm,
                             mixer_out=mixer is not None, n_cast=n_cast)
    tok = lambda w: pl.BlockSpec((tm, w), lambda i: (i, 0))
    in_specs = [tok(D), _layer_spec((1, D), layer), _const_spec((D, d_ff)),
                _const_spec((D, d_ff)), _const_spec((d_ff, D)), _const_spec((1, D))]
    args = [x, norm_g.reshape(norm_g.shape[0], 1, D), w_gate, w_up, w_down, gf.reshape(1, D)]
    if mixer is not None:
        hn, xc, z, w_down_m, j = mixer
        inner = hn.shape[-1]
        in_specs += [tok(inner), tok(inner), tok(inner), _layer_spec((inner, D), j)]
        args += [hn, xc, z, w_down_m]
    out_shape = [jax.ShapeDtypeStruct((T, D), F32)]
    out_specs = [tok(D)]
    if cast is not None:
        mats, cast_layer = cast
        assert n_steps % CAST_BLOCKS == 0
        per_block = n_steps // CAST_BLOCKS
        for w in mats:
            rows, cols = w.shape[1] // CAST_BLOCKS, w.shape[2]
            in_specs.append(pl.BlockSpec((None, rows, cols), lambda i: (cast_layer, i // per_block, 0)))
            args.append(w)
            out_shape.append(jax.ShapeDtypeStruct(w.shape[1:], BF16))
            out_specs.append(pl.BlockSpec((rows, cols), lambda i: (i // per_block, 0)))
    outs = pl.pallas_call(
        body,
        out_shape=tuple(out_shape),
        grid=(n_steps,),
        in_specs=in_specs,
        out_specs=tuple(out_specs),
        compiler_params=_cparams(1, VMEM_LIMIT if cast is None else VMEM_LIMIT_CORE),
        name="ffn",
    )(*args)
    return outs[0] if cast is None else (outs[0], tuple(outs[1:]))


def _ml_pre_body(x_ref, g_ref, wup_ref, wc_ref, bc_ref, sk_ref, bq_ref, bk_ref, bv_ref, wgt_ref,
                 bgt_ref, hist_ref, q_ref, k_ref, v_ref, xc_ref, z_ref, gates_ref, tail_ref, xbuf,
                 *, tm, inner, header, stride, tiles_per_seq):
    i = pl.program_id(0)

    n_groups = inner // MXU_DIM

    @pl.when(i % tiles_per_seq == 0)
    def _():
        for g in range(n_groups):
            xbuf[g, 0:header, :] = hist_ref[:, g * MXU_DIM:(g + 1) * MXU_DIM]

    xn = _rmsnorm(x_ref[...], g_ref[...]).astype(BF16)

    def up_proj(g):
        cols = slice(g * MXU_DIM, (g + 1) * MXU_DIM)
        zcols = slice(inner + g * MXU_DIM, inner + (g + 1) * MXU_DIM)
        xm_g = _dot(xn, wup_ref[:, cols])
        z_ref[:, cols] = _silu(_dot(xn, wup_ref[:, zcols])).astype(BF16)
        return xm_g

    def gate_part(g, xcb, xmb):
        rows = lambda base: slice(base + g * MXU_DIM, base + (g + 1) * MXU_DIM)
        return _dot(xcb, wgt_ref[rows(0), :]) + _dot(xmb, wgt_ref[rows(inner), :])

    gacc = bgt_ref[...]
    prev_act = None
    xm_next = up_proj(0)
    for g in range(n_groups):
        cols = slice(g * MXU_DIM, (g + 1) * MXU_DIM)
        xm = xm_next
        if g + 1 < n_groups:
            xm_next = up_proj(g + 1)
        xbuf[g, header:header + tm, :] = xm
        acc = bc_ref[:, cols] + wc_ref[CONV_W - 1:CONV_W, cols] * xm
        for d in range(1, CONV_W):
            lo = header - d * stride
            acc = acc + wc_ref[CONV_W - 1 - d:CONV_W - d, cols] * xbuf[g, lo:lo + tm, :]
        xc = _silu(acc)
        xc_ref[:, cols] = (sk_ref[:, cols] * xc).astype(BF16)
        xcb = xc.astype(BF16)
        xmb = xm.astype(BF16)
        q_ref[:, cols] = _dot(xcb, bq_ref[g]).astype(BF16)
        k_ref[:, cols] = _dot(xcb, bk_ref[g]).astype(BF16)
        v_ref[:, cols] = _dot(xmb, bv_ref[g]).astype(BF16)
        if g > 0:
            gacc = gacc + gate_part(g - 1, *prev_act)
        prev_act = (xcb, xmb)
        last = xbuf[g, tm:tm + header, :]
        tail_ref[:, cols] = last
        xbuf[g, 0:header, :] = last
    gates_ref[...] = gacc + gate_part(n_groups - 1, *prev_act)


def _ml_pre(x, norm_g, layer, w_up, w_conv, b_conv, skip, bq, bk, bv, w_gates, b_gates, hist, j,
            tm, stride, tiles_per_seq):
    T, D = x.shape
    inner = w_conv.shape[-1]
    header = hist.shape[1]
    n_tiles = T // tm
    tok = lambda w: pl.BlockSpec((tm, w), lambda i: (i, 0))
    in_specs = [
        tok(D),
        _layer_spec((1, D), layer),
        _layer_spec((D, 2 * inner), j),
        _layer_spec((CONV_W, inner), j),
        _layer_spec((1, inner), j),
        _layer_spec((1, inner), j),
        _const_spec(bq.shape), _const_spec(bk.shape), _const_spec(bv.shape),
        _const_spec(w_gates.shape), _const_spec(b_gates.shape),
        pl.BlockSpec((None, header, inner), lambda i: (i // tiles_per_seq, 0, 0)),
    ]
    args = [x, norm_g.reshape(norm_g.shape[0], 1, D), w_up, w_conv,
            b_conv.reshape(b_conv.shape[0], 1, inner), skip.reshape(skip.shape[0], 1, inner),
            bq, bk, bv, w_gates, b_gates, hist]
    out_shape = [
        jax.ShapeDtypeStruct((T, inner), BF16),
        jax.ShapeDtypeStruct((T, inner), BF16),
        jax.ShapeDtypeStruct((T, inner), BF16),
        jax.ShapeDtypeStruct((T, inner), BF16),
        jax.ShapeDtypeStruct((T, inner), BF16),
        jax.ShapeDtypeStruct((T, 2 * LANE), F32),
        jax.ShapeDtypeStruct((n_tiles, header, inner), F32),
    ]
    out_specs = [tok(inner), tok(inner), tok(inner), tok(inner), tok(inner), tok(2 * LANE),
                 pl.BlockSpec((None, header, inner), lambda i: (i, 0, 0))]
    body = functools.partial(_ml_pre_body, tm=tm, inner=inner, header=header, stride=stride,
                             tiles_per_seq=tiles_per_seq)
    return pl.pallas_call(
        body,
        out_shape=tuple(out_shape),
        grid=(n_tiles,),
        in_specs=in_specs,
        out_specs=tuple(out_specs),
        scratch_shapes=[pltpu.VMEM((inner // MXU_DIM, header + tm, MXU_DIM), F32)],
        compiler_params=_cparams(1),
        name="ml_pre",
    )(*args)


def _log_sigmoid(x):
    return jnp.minimum(x, 0.0) - jnp.log(1.0 + jnp.exp(-jnp.abs(x)))


def _token_scan(x, op, fill):
    n = x.shape[0]
    tok = lax.broadcasted_iota(jnp.int32, x.shape, 0)
    d = 1
    while d < n:
        x = op(x, jnp.where(tok >= d, pltpu.roll(x, d, axis=0), fill))
        d *= 2
    return x


def _gate_vectors(gates, m_prev, L):
    ig = gates[:, :LANE]
    bt = _token_scan(_log_sigmoid(gates[:, LANE:]), jnp.add, 0.0)
    a = ig - bt
    m = bt + jnp.maximum(_token_scan(a, jnp.maximum, NEG), m_prev)
    dcol = bt - m
    bl, ml = bt[L - 1:L, :], m[L - 1:L, :]
    return dict(a=a, dcol=dcol, w_inter=jnp.exp(m_prev + dcol), exp_neg_m=jnp.exp(-m),
                gs=jnp.exp(bl - bt + ig - ml) * HEAD_DIM ** -0.5,
                g_inter=jnp.exp(m_prev + bl - ml), ml=ml)


def _ml_core_body(*refs, cfgs):
    n_in = [9 if has_state else 6 for (_, _, _, has_state) in cfgs]
    pos_out = sum(n_in)
    pos_in = 0
    stages = []
    for (L, nb, nc, has_state), k_in in zip(cfgs, n_in):
        stages.append(_ml_core_group(refs[pos_in:pos_in + k_in], refs[pos_out:pos_out + 4],
                                     L, nb, nc, has_state))
        pos_in += k_in
        pos_out += 4
    while stages:
        stages = [st for st in stages if next(st, 'done') != 'done']


def _ml_core_group(in_refs, out_refs, L, nb, nc, has_state):
    h_ref, c_ref, n_ref, m_ref = out_refs
    if has_state:
        q_ref, k_ref, v_ref, gates_ref, gain_ref, c0_ref, n0_ref, m0_ref, _ = in_refs
    else:
        q_ref, k_ref, v_ref, gates_ref, gain_ref, _ = in_refs
    if has_state and nc == 1:
        c_prev, n_prev_ref, m_prev_ref = c0_ref, n0_ref, m0_ref
    else:
        c_prev, n_prev_ref, m_prev_ref = c_ref, n_ref, m_ref

        @pl.when(pl.program_id(0) % nc == 0)
        def _():
            if has_state:
                c_ref[...] = c0_ref[...]
                n_ref[...] = n0_ref[...]
                m_ref[...] = m0_ref[...]
            else:
                c_ref[...] = jnp.zeros_like(c_ref)
                n_ref[...] = jnp.zeros_like(n_ref)
                m_ref[...] = jnp.zeros_like(m_ref)

    log_scale = -0.5 * math.log(HEAD_DIM)
    row = lax.broadcasted_iota(jnp.int32, (L, L), 0)
    col = lax.broadcasted_iota(jnp.int32, (L, L), 1)
    causal = col <= row
    heads = range(N_HEADS)
    seqs = range(nb)
    hcols = [slice(h * HEAD_DIM, (h + 1) * HEAD_DIM) for h in heads]
    ones_rows = jnp.ones((SUBLANE, L), BF16)
    n_prev = [n_prev_ref[s] for s in seqs]


    qc = [[_dot(q_ref[s, :, hcols[h]], c_prev[s, h].astype(BF16)) for h in heads] for s in seqs]
    yield

    gvs, a_rows = [], []
    for s in seqs:
        gv = _gate_vectors(gates_ref[s], m_prev_ref[s, 0:1, :], L)
        m_ref[s] = jnp.broadcast_to(gv['ml'], m_ref.shape[1:])
        a_shift = gv['a'] + log_scale
        if L % LANE == 0:
            a_t = a_shift.T
            a_rows.append([a_t[h:h + 1, :] for h in heads])
        else:
            a_rows.append([jnp.sum(jnp.where(row == col, a_shift[:, h:h + 1], 0.0), axis=0,
                                   keepdims=True) for h in heads])
        gvs.append(gv)
    yield

    for s in seqs:
        hcol = lambda name, h: gvs[s][name][:, h:h + 1]
        for h in heads:
            qh, kh, vh = q_ref[s, :, hcols[h]], k_ref[s, :, hcols[h]], v_ref[s, :, hcols[h]]

            g_inter = hcol('g_inter', h)
            kg = kh * hcol('gs', h).astype(BF16)
            upd = lax.dot_general(kg, vh, (((0,), (0,)), ((), ())), preferred_element_type=F32)
            c_ref[s, h] = g_inter * c_prev[s, h] + upd
            n_ref[s, h:h + 1, :] = (g_inter * n_prev[s][h:h + 1, :]
                                    + _dot(ones_rows, kg)[0:1, :])
            yield

            w_inter = hcol('w_inter', h)
            w_intra = jnp.exp(jnp.where(causal, hcol('dcol', h) + a_rows[s][h], NEG))
            sc = lax.dot_general(qh, kh, (((1,), (1,)), ((), ())),
                                 preferred_element_type=F32) * w_intra
            num = _dot(sc.astype(BF16), vh) + qc[s][h] * w_inter
            qn = jnp.sum((qh * n_prev[s][h:h + 1, :].astype(BF16)).astype(F32), axis=1,
                         keepdims=True)
            nq = jnp.sum(sc, axis=1, keepdims=True) + w_inter * qn
            den = jnp.maximum(jnp.abs(nq), hcol('exp_neg_m', h))
            mu = jnp.mean(num, axis=-1, keepdims=True)
            hc = num - mu
            var = jnp.mean(hc * hc, axis=-1, keepdims=True)
            h_ref[s, :, hcols[h]] = (hc * lax.rsqrt(var + EPS * den * den)
                                     * gain_ref[:, hcols[h]]).astype(BF16)
            yield


def _ml_core(groups, j):
    ios = [_core_group_io(g, j) for g in groups]
    steps = {io['steps'] for io in ios}
    assert len(steps) == 1, "token groups must have the same number of grid steps"
    in_specs, args, out_specs, out_shape, aliases = [], [], [], [], {}
    for gi, io in enumerate(ios):
        if io['alias_in'] is not None:
            aliases[len(args) + io['alias_in']] = N_CORE_OUT * gi + 1
        in_specs += io['in_specs']
        args += io['args']
        out_specs += io['out_specs']
        out_shape += io['out_shape']
    outs = pl.pallas_call(
        functools.partial(_ml_core_body, cfgs=tuple(io['cfg'] for io in ios)),
        out_shape=tuple(out_shape),
        grid=(steps.pop(),),
        in_specs=in_specs,
        out_specs=tuple(out_specs),
        input_output_aliases=aliases,
        compiler_params=_cparams(1, VMEM_LIMIT_CORE),
        name="ml_core",
    )(*args)
    return [_core_group_result(g, outs[N_CORE_OUT * gi:N_CORE_OUT * (gi + 1)])
            for gi, g in enumerate(groups)]


N_CORE_OUT = 4


def _core_group_io(g, j):
    n_ml = 2
    B, nc, L, nb = g['B'], g['nc'], g['L'], g['nb']
    inner = g['q'].shape[-1]
    has_state = g['state'] is not None
    seq3 = lambda a: a.reshape(B, nc * L, a.shape[-1])
    tok = lambda w: pl.BlockSpec((nb, L, w), lambda t: (t // nc, t % nc, 0))
    per_seq = lambda shape: pl.BlockSpec((nb,) + shape, lambda t: (t // nc,) + (0,) * len(shape))
    per_seq_layer = lambda shape: pl.BlockSpec(
        (None, nb) + shape, lambda t: (j, t // nc) + (0,) * len(shape))
    c_block = per_seq_layer((N_HEADS, HEAD_DIM, HEAD_DIM))
    in_specs = [tok(inner), tok(inner), tok(inner), tok(2 * LANE), _layer_spec((1, inner), j)]
    args = [seq3(g['q']), seq3(g['k']), seq3(g['v']), seq3(g['gates']),
            g['gain'].reshape(g['gain'].shape[0], 1, inner)]
    if has_state:
        in_specs += [c_block, per_seq_layer((N_HEADS, HEAD_DIM)), per_seq((SUBLANE, LANE))]
        args += list(g['state'])
    in_specs.append(pl.BlockSpec(memory_space=pl.ANY))
    alias_in = None
    if g['c_all'] is not None:
        args.append(g['c_all'])
        alias_in = len(args) - 1
    else:
        args.append(jnp.zeros((SUBLANE, LANE), F32))
    out_shape = [jax.ShapeDtypeStruct((B, nc * L, inner), BF16),
                 jax.ShapeDtypeStruct((n_ml, B, N_HEADS, HEAD_DIM, HEAD_DIM), F32),
                 jax.ShapeDtypeStruct((B, N_HEADS, HEAD_DIM), F32),
                 jax.ShapeDtypeStruct((B, SUBLANE, LANE), F32)]
    out_specs = [tok(inner), c_block, per_seq((N_HEADS, HEAD_DIM)), per_seq((SUBLANE, LANE))]
    return dict(in_specs=in_specs, args=args, out_specs=out_specs, out_shape=out_shape,
                cfg=(L, nb, nc, has_state), alias_in=alias_in, steps=B // nb * nc)


def _core_group_result(g, outs):
    hn, c_all, n, m = outs
    return hn.reshape(g['q'].shape), c_all, n, m


def _gelu(x):
    return 0.5 * x * (1.0 + lax.erf(x * (2.0 ** -0.5)))


def _cm_body(*refs, width, sub, emit_vn):
    if emit_vn:
        x_ref, g_ref, win_ref, bin_ref, lg_ref, wmix_ref, bmix_ref, wout_ref, o_ref, vn_ref = refs
    else:
        x_ref, g_ref, win_ref, bin_ref, lg_ref, wmix_ref, bmix_ref, wout_ref, o_ref = refs
    gd = width // CM_GROUPS
    n_sub = x_ref.shape[0] // sub

    def in_proj(s):
        x = x_ref[s * sub:(s + 1) * sub, :]
        xn = _rmsnorm(x, g_ref[...]).astype(BF16)
        v = _gelu(_dot(xn, win_ref[:, width:]) + bin_ref[:, width:])
        u = _gelu(_dot(xn, win_ref[:, :width]) + bin_ref[:, :width])
        return x, u, v

    def gate_and_out(s, x, u, v):
        rows = slice(s * sub, (s + 1) * sub)
        mu = jnp.mean(v, axis=-1, keepdims=True)
        vc = v - mu
        var = jnp.mean(vc * vc, axis=-1, keepdims=True)
        vn = vc * lax.rsqrt(var + EPS) * lg_ref[...]
        if emit_vn:
            vn_ref[rows, :] = vn
        vnb = vn.astype(BF16)
        bmix = bmix_ref[...]
        parts = []
        for g in range(CM_GROUPS):
            cols = slice(g * gd, (g + 1) * gd)
            mix = _dot(wmix_ref[g], vnb[:, cols]) + bmix[:, g:g + 1]
            parts.append((u[:, cols] * mix).astype(BF16))
        o_ref[rows, :] = x + _dot(jnp.concatenate(parts, axis=-1), wout_ref[...])

    nxt = in_proj(0)
    for s in range(n_sub):
        cur = nxt
        if s + 1 < n_sub:
            nxt = in_proj(s + 1)
        gate_and_out(s, *cur)


def _chunk_mlp(x, norm_g, layer, w_in, b_in, ln_g, wmix, bmix, w_out, j, tm, emit_vn):
    T, D = x.shape
    width = ln_g.shape[-1]
    body = functools.partial(_cm_body, width=width, sub=wmix.shape[1], emit_vn=emit_vn)
    tok = lambda w: pl.BlockSpec((tm, w), lambda i: (i, 0))
    out_shape = [jax.ShapeDtypeStruct((T, D), F32)]
    out_specs = [tok(D)]
    if emit_vn:
        out_shape.append(jax.ShapeDtypeStruct((T, width), F32))
        out_specs.append(tok(width))
    return pl.pallas_call(
        body,
        out_shape=tuple(out_shape),
        grid=(T // tm,),
        in_specs=[tok(D), _layer_spec((1, D), layer),
                  _layer_spec((D, 2 * width), j), _layer_spec((1, 2 * width), j),
                  _layer_spec((1, width), j),
                  _const_spec(wmix.shape), _const_spec(bmix.shape),
                  _layer_spec((width, D), j)],
        out_specs=tuple(out_specs),
        compiler_params=_cparams(1),
        name="chunk_mlp",
    )(x, norm_g.reshape(norm_g.shape[0], 1, D), w_in, b_in.reshape(b_in.shape[0], 1, 2 * width),
      ln_g.reshape(ln_g.shape[0], 1, width), wmix, bmix, w_out)


def _blockdiag_dense(w):
    rows = w.reshape(-1, MXU_DIM, QKV_BLOCK)
    r = lax.broadcasted_iota(jnp.int32, (MXU_DIM, MXU_DIM), 0)
    c = lax.broadcasted_iota(jnp.int32, (MXU_DIM, MXU_DIM), 1)
    spread = (lax.broadcasted_iota(jnp.int32, (QKV_BLOCK, MXU_DIM), 1) % QKV_BLOCK
              == lax.broadcasted_iota(jnp.int32, (QKV_BLOCK, MXU_DIM), 0)).astype(w.dtype)
    dense = jnp.einsum('grd,dc->grc', rows, spread, precision=lax.Precision.HIGHEST)
    return jnp.where(r // QKV_BLOCK == c // QKV_BLOCK, dense, 0)


def _gate_weights(w_ig, b_ig, w_fg, b_fg, dense_qkv):
    n_tiles = dense_qkv[0].shape[0]
    w = jnp.concatenate([w_ig, w_fg], axis=1).reshape(3, n_tiles, MXU_DIM, 2 * N_HEADS)
    folded = jnp.einsum('mgrc,mgch->mgrh', jnp.stack(dense_qkv), w,
                        precision=lax.Precision.HIGHEST)
    folded = folded.reshape(3, n_tiles * MXU_DIM, 2 * N_HEADS)
    lane = lax.broadcasted_iota(jnp.int32, (2 * N_HEADS, 2 * LANE), 1)
    src = lax.broadcasted_iota(jnp.int32, (2 * N_HEADS, 2 * LANE), 0)
    place = (lane == (src % N_HEADS) + LANE * (src // N_HEADS)).astype(w_ig.dtype)
    w_all = jnp.dot(jnp.concatenate([folded[0] + folded[1], folded[2]], axis=0), place,
                    precision=lax.Precision.HIGHEST)
    bpad = jnp.zeros((LANE - N_HEADS,), b_ig.dtype)
    b = jnp.concatenate([b_ig, bpad, b_fg, bpad])
    return w_all.astype(BF16), b.reshape(1, 2 * LANE)


def _trunks(xs, grps, p):
    depth = p['norm_ff1'].shape[0]
    sts = [dict(x=x, c_all=None, n=[], m=[], conv=[], v=[]) for x in xs]

    ffn_w = {'cur': tuple(w[0].astype(BF16) for w in p['ffn1_w'])}

    def ffn_round(norm_key, i, nxt, use_mixer=False, final_g=None):
        new_w = None
        for gi, (grp, st) in enumerate(zip(grps, sts)):
            mixer = st['mixer'] if use_mixer else None
            tm = grp['tm'] if mixer is None else grp['tm_mixer']
            cast = nxt if gi == 0 else None
            out = _ffn(st['x'], p[norm_key], ffn_w['cur'], i, tm, final_g=final_g, mixer=mixer,
                       cast=cast)
            if cast is not None:
                out, new_w = out
            st['x'] = out
        ffn_w['cur'] = new_w

    for i in range(depth):
        j = i // 2
        ffn_round('norm_ff1', i, (p['ffn2_w'], i))
        if i % 2 == 0:
            core_in = []
            for grp, st in zip(grps, sts):
                q, k, v, xc, z, gates, tail = _ml_pre(
                    st['x'], p['norm_mix'], i, p['ml_w_up'], p['ml_w_conv'], p['ml_b_conv'],
                    p['ml_skip'], p['bq'][j], p['bk'][j], p['bv'][j], p['w_gates'][j],
                    p['b_gates'][j], grp['hist'][j], j, grp['pre_tm'], grp['stride'],
                    grp['tiles_per_seq'])
                st['conv'].append(grp['conv_out'](tail))
                st['xc'], st['z'] = xc, z
                q, k, v, gates = grp['to_core'](q, k, v, gates)
                core_in.append(dict(q=q, k=k, v=v, gates=gates, gain=p['ml_hn_g'], B=grp['B'],
                                    nc=grp['nc'], L=grp['L'], nb=grp['nb'], state=grp['state'](j),
                                    c_all=st['c_all']))
            for grp, st, (hn, c_all, n, m) in zip(grps, sts, _ml_core(core_in, j)):
                st['c_all'] = c_all
                st['n'].append(n)
                st['m'].append(m[:, 0, :N_HEADS])
                st['mixer'] = (grp['from_core'](hn), st['xc'], st['z'], p['ml_w_down'], j)
        else:
            for grp, st in zip(grps, sts):
                outs = _chunk_mlp(st['x'], p['norm_mix'], i, p['cm_w_in'], p['cm_b_in'], p['cm_ln_g'],
                                  grp['wmix'][j], grp['bmix'][j], p['cm_w_out'], j, grp['cm_tm'],
                                  emit_vn=grp['emit_vn'])
                st['x'] = outs[0]
                st['v'].append(outs[1] if grp['emit_vn'] else None)
                st['mixer'] = None
        last = i == depth - 1
        ffn_round('norm_ff2', i, None if last else (p['ffn1_w'], i + 1), use_mixer=True,
                  final_g=p['norm_final'] if last else None)
    return [(st['x'], st['c_all'], jnp.stack(st['n']), jnp.stack(st['m']), jnp.stack(st['conv']),
             st['v']) for st in sts]


def kernel(x_prompt, x_sample, state_C, state_n, state_m, state_conv, norm_ff1, norm_mix, norm_ff2, norm_final, ffn1_w_gate, ffn1_w_up, ffn1_w_down, ffn2_w_gate, ffn2_w_up, ffn2_w_down, ml_w_up, ml_w_conv, ml_b_conv, ml_w_q, ml_w_k, ml_w_v, ml_w_ig, ml_b_ig, ml_w_fg, ml_b_fg, ml_hn_g, ml_skip, ml_w_down, cm_w_in, cm_b_in, cm_ln_g, cm_w_s, cm_b_s, cm_w_out):
    bp, sp, d_model = x_prompt.shape
    bs, ss, _ = x_sample.shape
    n_ml, inner = ml_w_conv.shape[0], ml_w_conv.shape[-1]
    n_cm, width = cm_ln_g.shape
    hist_rows = CONV_W - 1
    sb = min(64, bs)

    dense_qkv = [tuple(_blockdiag_dense(w[j]) for w in (ml_w_q, ml_w_k, ml_w_v)) for j in range(n_ml)]
    gate_wb = [_gate_weights(ml_w_ig[j], ml_b_ig[j], ml_w_fg[j], ml_b_fg[j], dense_qkv[j])
               for j in range(n_ml)]
    p = {
        'norm_ff1': norm_ff1, 'norm_mix': norm_mix, 'norm_ff2': norm_ff2, 'norm_final': norm_final,
        'ffn1_w': (ffn1_w_gate, ffn1_w_up, ffn1_w_down),
        'ffn2_w': (ffn2_w_gate, ffn2_w_up, ffn2_w_down),
        'ml_w_up': ml_w_up.astype(BF16), 'ml_w_conv': ml_w_conv, 'ml_b_conv': ml_b_conv,
        'bq': [d[0].astype(BF16) for d in dense_qkv], 'bk': [d[1].astype(BF16) for d in dense_qkv],
        'bv': [d[2].astype(BF16) for d in dense_qkv],
        'w_gates': [wb[0] for wb in gate_wb], 'b_gates': [wb[1] for wb in gate_wb],
        'ml_hn_g': ml_hn_g, 'ml_skip': ml_skip, 'ml_w_down': ml_w_down.astype(BF16),
        'cm_w_in': cm_w_in.astype(BF16), 'cm_b_in': cm_b_in, 'cm_ln_g': cm_ln_g,
        'cm_w_out': cm_w_out.astype(BF16),
    }

    def mix_weights(L, kron_left):
        causal = jnp.tril(jnp.ones((L, L), dtype=bool))
        ws = jnp.where(causal, cm_w_s[:, :, :L, :L], 0)
        bs_ = cm_b_s[:, :, :L]
        if kron_left:
            reps = kron_left
            eye = jnp.eye(reps, dtype=ws.dtype)
            wm = ws[:, :, None, :, None, :] * eye[None, None, :, None, :, None]
            wm = wm.reshape(n_cm, CM_GROUPS, reps * L, reps * L)
            bm = jnp.tile(bs_, (1, 1, reps))
        else:
            eye = jnp.eye(sb, dtype=ws.dtype)
            wm = ws[:, :, :, None, :, None] * eye[None, None, None, :, None, :]
            wm = wm.reshape(n_cm, CM_GROUPS, L * sb, L * sb)
            bm = jnp.repeat(bs_, sb, axis=2)
        bm = jnp.swapaxes(bm, 1, 2)
        bm = jnp.concatenate([bm, jnp.zeros(bm.shape[:2] + (LANE - CM_GROUPS,), bm.dtype)], axis=2)
        return wm.astype(BF16), bm

    tm_p = 1024
    pre_tm_p = 512
    cm_tm_p = 1024
    chunk_p = min(256, sp)
    cm_chunk_p = min(128, sp)
    wmix_p, bmix_p = mix_weights(cm_chunk_p, MXU_DIM // cm_chunk_p)
    grp_p = {
        'tm': tm_p, 'tm_mixer': 512, 'pre_tm': pre_tm_p, 'cm_tm': cm_tm_p, 'emit_vn': False,
        'stride': 1,
        'tiles_per_seq': sp // pre_tm_p,
        'hist': [jnp.zeros((bp, SUBLANE, inner), F32)] * n_ml,
        'conv_out': lambda tail: tail.reshape(bp, sp // pre_tm_p, SUBLANE, inner)[:, -1, SUBLANE - hist_rows:, :],
        'to_core': lambda q, k, v, g: (q, k, v, g),
        'from_core': lambda hn: hn,
        'state': lambda j: None,
        'B': bp, 'nc': sp // chunk_p, 'L': chunk_p, 'nb': 1,
        'wmix': wmix_p, 'bmix': bmix_p,
    }

    ts = bs * ss
    n_sg = bs // sb
    l_pad = SUBLANE

    def tile_major(a):
        steps, feat = a.shape[1], a.shape[2]
        return a.reshape(n_sg, sb, steps, feat).transpose(0, 2, 1, 3).reshape(n_sg * steps * sb, feat)

    def batch_major(a, steps):
        feat = a.shape[-1]
        return a.reshape(n_sg, steps, sb, feat).transpose(0, 2, 1, 3).reshape(bs, steps, feat)

    def to_core(q, k, v, g):
        def seq_major(a, fill):
            a = batch_major(a, ss)
            pad = jnp.broadcast_to(fill.astype(a.dtype), (bs, l_pad - ss, a.shape[-1]))
            return jnp.concatenate([a, pad], axis=1).reshape(bs * l_pad, a.shape[-1])
        zero = jnp.zeros((inner,), F32)
        lane_pad = jnp.zeros((LANE - N_HEADS,), F32)
        gfill = jnp.concatenate([jnp.full((N_HEADS,), NEG, F32), lane_pad,
                                 jnp.full((N_HEADS,), -NEG, F32), lane_pad])
        return seq_major(q, zero), seq_major(k, zero), seq_major(v, zero), seq_major(g, gfill)

    def from_core(hn):
        return tile_major(hn.reshape(bs, l_pad, inner)[:, :ss])

    def sample_state(j):
        m0 = jnp.pad(state_m[j], ((0, 0), (0, LANE - N_HEADS)))
        return state_C, state_n, jnp.broadcast_to(m0[:, None, :], (bs, SUBLANE, LANE))

    wmix_s, bmix_s = mix_weights(ss, 0)
    tile_s = ss * sb
    hist_s = [tile_major(state_conv[j]).reshape(n_sg, hist_rows * sb, inner) for j in range(n_ml)]
    grp_s = {
        'tm': tile_s, 'tm_mixer': tile_s, 'pre_tm': tile_s, 'cm_tm': tile_s, 'emit_vn': True,
        'stride': sb,
        'tiles_per_seq': 1,
        'hist': hist_s,
        'conv_out': lambda tail: batch_major(tail.reshape(n_sg * hist_rows * sb, inner), hist_rows),
        'to_core': to_core, 'from_core': from_core, 'state': sample_state,
        'B': bs, 'nc': 1, 'L': l_pad, 'nb': 2,
        'wmix': wmix_s, 'bmix': bmix_s,
    }
    ((yp, c_p, n_p, m_p, conv_p, _), (ys, c_s, n_s, m_s, conv_s, v_rows)) = _trunks(
        [x_prompt.reshape(bp * sp, d_model), tile_major(x_sample)], [grp_p, grp_s], p)

    y_prompt = yp.reshape(bp, sp, d_model)
    y_sample = batch_major(ys, ss)
    v_sample = jnp.stack([batch_major(vr, ss) for vr in v_rows])
    return (y_prompt, y_sample, c_p, n_p, m_p, conv_p, c_s, n_s, m_s, conv_s, v_sample)
```

```python
import functools
import math

import jax
import jax.numpy as jnp
from jax import lax
from jax.experimental import pallas as pl
from jax.experimental.pallas import tpu as pltpu

F32 = jnp.float32
BF16 = jnp.bfloat16

EPS = 1e-6
N_HEADS = 4
HEAD_DIM = 512
QKV_BLOCK = 4
CONV_W = 4
CM_GROUPS = 4
NEG = -1e30

LANE = 128
SUBLANE = 8
MXU_DIM = 256
VMEM_LIMIT = 56 * 1024 * 1024
VMEM_LIMIT_CORE = 60 * 1024 * 1024


def _cparams(n_axes, vmem_limit=VMEM_LIMIT):
    return pltpu.CompilerParams(dimension_semantics=("arbitrary",) * n_axes,
                                vmem_limit_bytes=vmem_limit)


def _const_spec(shape):
    nd = len(shape)
    return pl.BlockSpec(shape, lambda *_: (0,) * nd, pipeline_mode=pl.Buffered(1))


def _layer_spec(shape, layer):
    nd = len(shape)
    return pl.BlockSpec((None,) + tuple(shape), lambda *_: (layer,) + (0,) * nd,
                        pipeline_mode=pl.Buffered(1))


def _rmsnorm(x, g):
    return x * lax.rsqrt(jnp.mean(x * x, axis=-1, keepdims=True) + EPS) * g


def _silu(x):
    return x * jax.nn.sigmoid(x)


def _dot(a, b):
    return jnp.dot(a, b, preferred_element_type=F32)


def _ffn_body(*refs, ff_chunks, final_norm, mixer_out, n_cast):
    if n_cast:
        cast_in, cast_out = refs[-2 * n_cast - 1:-n_cast - 1], refs[-n_cast:]
        refs = refs[:-2 * n_cast - 1] + (refs[-n_cast - 1],)
        for src, dst in zip(cast_in, cast_out):
            dst[...] = src[...].astype(dst.dtype)
    if mixer_out:
        (x_ref, g_ref, wg_ref, wu_ref, wd_ref, gf_ref,
         hn_ref, xc_ref, z_ref, wdm_ref, o_ref) = refs
    else:
        x_ref, g_ref, wg_ref, wu_ref, wd_ref, gf_ref, o_ref = refs
    x = x_ref[...]
    if mixer_out:
        for lo in range(0, hn_ref.shape[1], MXU_DIM):
            cols = slice(lo, lo + MXU_DIM)
            out = ((hn_ref[:, cols].astype(F32) + xc_ref[:, cols].astype(F32))
                   * z_ref[:, cols].astype(F32))
            x = x + _dot(out.astype(BF16), wdm_ref[cols, :])
    xn = _rmsnorm(x, g_ref[...]).astype(wg_ref.dtype)
    acc = None
    for lo, hi in ff_chunks:
        gate = _dot(xn, wg_ref[:, lo:hi])
        up = _dot(xn, wu_ref[:, lo:hi])
        h = (_silu(gate) * up).astype(wd_ref.dtype)
        part = _dot(h, wd_ref[lo:hi, :])
        acc = part if acc is None else acc + part
    y = x + 0.5 * acc
    if final_norm:
        y = _rmsnorm(y, gf_ref[...])
    o_ref[...] = y


CAST_BLOCKS = 16


def _ffn(x, norm_g, weights, layer, tm, final_g=None, mixer=None, cast=None):
    T, D = x.shape
    w_gate, w_up, w_down = weights
    d_ff = w_gate.shape[-1]
    n_steps = T // tm
    step = 4 * MXU_DIM
    ff_chunks = tuple((lo, min(lo + step, d_ff)) for lo in range(0, d_ff, step))
    final_norm = final_g is not None
    gf = final_g if final_norm else norm_g[layer]
    n_cast = 0 if cast is None else len(cast[0])
    body = functools.partial(_ffn_body, ff_chunks=ff_chunks, final_norm=final_norm,
                             mixer_out=mixer is not None, n_cast=n_cast)
    tok = lambda w: pl.BlockSpec((tm, w), lambda i: (i, 0))
    in_specs = [tok(D), _layer_spec((1, D), layer), _const_spec((D, d_ff)),
                _const_spec((D, d_ff)), _const_spec((d_ff, D)), _const_spec((1, D))]
    args = [x, norm_g.reshape(norm_g.shape[0], 1, D), w_gate, w_up, w_down, gf.reshape(1, D)]
    if mixer is not None:
        hn, xc, z, w_down_m, j = mixer
        inner = hn.shape[-1]
        in_specs += [tok(inner), tok(inner), tok(inner), _layer_spec((inner, D), j)]
        args += [hn, xc, z, w_down_m]
    out_shape = [jax.ShapeDtypeStruct((T, D), F32)]
    out_specs = [tok(D)]
    if cast is not None:
        mats, cast_layer = cast
        assert n_steps % CAST_BLOCKS == 0
        per_block = n_steps // CAST_BLOCKS
        for w in mats:
            rows, cols = w.shape[1] // CAST_BLOCKS, w.shape[2]
            in_specs.append(pl.BlockSpec((None, rows, cols), lambda i: (cast_layer, i // per_block, 0)))
            args.append(w)
            out_shape.append(jax.ShapeDtypeStruct(w.shape[1:], BF16))
            out_specs.append(pl.BlockSpec((rows, cols), lambda i: (i // per_block, 0)))
    outs = pl.pallas_call(
        body,
        out_shape=tuple(out_shape),
        grid=(n_steps,),
        in_specs=in_specs,
        out_specs=tuple(out_specs),
        compiler_params=_cparams(1, VMEM_LIMIT if cast is None else VMEM_LIMIT_CORE),
        name="ffn",
    )(*args)
    return outs[0] if cast is None else (outs[0], tuple(outs[1:]))


def _ml_pre_body(x_ref, g_ref, wup_ref, wc_ref, bc_ref, sk_ref, bq_ref, bk_ref, bv_ref, wgt_ref,
                 bgt_ref, hist_ref, q_ref, k_ref, v_ref, xc_ref, z_ref, gates_ref, tail_ref, xbuf,
                 *, tm, inner, header, stride, tiles_per_seq):
    i = pl.program_id(0)

    n_groups = inner // MXU_DIM

    @pl.when(i % tiles_per_seq == 0)
    def _():
        for g in range(n_groups):
            xbuf[g, 0:header, :] = hist_ref[:, g * MXU_DIM:(g + 1) * MXU_DIM]

    xn = _rmsnorm(x_ref[...], g_ref[...]).astype(BF16)

    def up_proj(g):
        cols = slice(g * MXU_DIM, (g + 1) * MXU_DIM)
        zcols = slice(inner + g * MXU_DIM, inner + (g + 1) * MXU_DIM)
        xm_g = _dot(xn, wup_ref[:, cols])
        z_ref[:, cols] = _silu(_dot(xn, wup_ref[:, zcols])).astype(BF16)
        return xm_g

    def gate_part(g, xcb, xmb):
        rows = lambda base: slice(base + g * MXU_DIM, base + (g + 1) * MXU_DIM)
        return _dot(xcb, wgt_ref[rows(0), :]) + _dot(xmb, wgt_ref[rows(inner), :])

    gacc = bgt_ref[...]
    prev_act = None
    xm_next = up_proj(0)
    for g in range(n_groups):
        cols = slice(g * MXU_DIM, (g + 1) * MXU_DIM)
        xm = xm_next
        if g + 1 < n_groups:
            xm_next = up_proj(g + 1)
        xbuf[g, header:header + tm, :] = xm
        acc = bc_ref[:, cols] + wc_ref[CONV_W - 1:CONV_W, cols] * xm
        for d in range(1, CONV_W):
            lo = header - d * stride
            acc = acc + wc_ref[CONV_W - 1 - d:CONV_W - d, cols] * xbuf[g, lo:lo + tm, :]
        xc = _silu(acc)
        xc_ref[:, cols] = (sk_ref[:, cols] * xc).astype(BF16)
        xcb = xc.astype(BF16)
        xmb = xm.astype(BF16)
        q_ref[:, cols] = _dot(xcb, bq_ref[g]).astype(BF16)
        k_ref[:, cols] = _dot(xcb, bk_ref[g]).astype(BF16)
        v_ref[:, cols] = _dot(xmb, bv_ref[g]).astype(BF16)
        if g > 0:
            gacc = gacc + gate_part(g - 1, *prev_act)
        prev_act = (xcb, xmb)
        last = xbuf[g, tm:tm + header, :]
        tail_ref[:, cols] = last
        xbuf[g, 0:header, :] = last
    gates_ref[...] = gacc + gate_part(n_groups - 1, *prev_act)


def _ml_pre(x, norm_g, layer, w_up, w_conv, b_conv, skip, bq, bk, bv, w_gates, b_gates, hist, j,
            tm, stride, tiles_per_seq):
    T, D = x.shape
    inner = w_conv.shape[-1]
    header = hist.shape[1]
    n_tiles = T // tm
    tok = lambda w: pl.BlockSpec((tm, w), lambda i: (i, 0))
    in_specs = [
        tok(D),
        _layer_spec((1, D), layer),
        _layer_spec((D, 2 * inner), j),
        _layer_spec((CONV_W, inner), j),
        _layer_spec((1, inner), j),
        _layer_spec((1, inner), j),
        _const_spec(bq.shape), _const_spec(bk.shape), _const_spec(bv.shape),
        _const_spec(w_gates.shape), _const_spec(b_gates.shape),
        pl.BlockSpec((None, header, inner), lambda i: (i // tiles_per_seq, 0, 0)),
    ]
    args = [x, norm_g.reshape(norm_g.shape[0], 1, D), w_up, w_conv,
            b_conv.reshape(b_conv.shape[0], 1, inner), skip.reshape(skip.shape[0], 1, inner),
            bq, bk, bv, w_gates, b_gates, hist]
    out_shape = [
        jax.ShapeDtypeStruct((T, inner), BF16),
        jax.ShapeDtypeStruct((T, inner), BF16),
        jax.ShapeDtypeStruct((T, inner), BF16),
        jax.ShapeDtypeStruct((T, inner), BF16),
        jax.ShapeDtypeStruct((T, inner), BF16),
        jax.ShapeDtypeStruct((T, 2 * LANE), F32),
        jax.ShapeDtypeStruct((n_tiles, header, inner), F32),
    ]
    out_specs = [tok(inner), tok(inner), tok(inner), tok(inner), tok(inner), tok(2 * LANE),
                 pl.BlockSpec((None, header, inner), lambda i: (i, 0, 0))]
    body = functools.partial(_ml_pre_body, tm=tm, inner=inner, header=header, stride=stride,
                             tiles_per_seq=tiles_per_seq)
    return pl.pallas_call(
        body,
        out_shape=tuple(out_shape),
        grid=(n_tiles,),
        in_specs=in_specs,
        out_specs=tuple(out_specs),
        scratch_shapes=[pltpu.VMEM((inner // MXU_DIM, header + tm, MXU_DIM), F32)],
        compiler_params=_cparams(1),
        name="ml_pre",
    )(*args)


def _log_sigmoid(x):
    return jnp.minimum(x, 0.0) - jnp.log(1.0 + jnp.exp(-jnp.abs(x)))


def _token_scan(x, op, fill):
    n = x.shape[0]
    tok = lax.broadcasted_iota(jnp.int32, x.shape, 0)
    d = 1
    while d < n:
        x = op(x, jnp.where(tok >= d, pltpu.roll(x, d, axis=0), fill))
        d *= 2
    return x


def _gate_vectors(gates, m_prev, L):
    ig = gates[:, :LANE]
    bt = _token_scan(_log_sigmoid(gates[:, LANE:]), jnp.add, 0.0)
    a = ig - bt
    m = bt + jnp.maximum(_token_scan(a, jnp.maximum, NEG), m_prev)
    dcol = bt - m
    bl, ml = bt[L - 1:L, :], m[L - 1:L, :]
    return dict(a=a, dcol=dcol, w_inter=jnp.exp(m_prev + dcol), exp_neg_m=jnp.exp(-m),
                gs=jnp.exp(bl - bt + ig - ml) * HEAD_DIM ** -0.5,
                g_inter=jnp.exp(m_prev + bl - ml), ml=ml)


def _ml_core_body(*refs, cfgs):
    n_in = [9 if has_state else 6 for (_, _, _, has_state) in cfgs]
    pos_out = sum(n_in)
    pos_in = 0
    stages = []
    for (L, nb, nc, has_state), k_in in zip(cfgs, n_in):
        stages.append(_ml_core_group(refs[pos_in:pos_in + k_in], refs[pos_out:pos_out + 4],
                                     L, nb, nc, has_state))
        pos_in += k_in
        pos_out += 4
    while stages:
        stages = [st for st in stages if next(st, 'done') != 'done']


def _ml_core_group(in_refs, out_refs, L, nb, nc, has_state):
    h_ref, c_ref, n_ref, m_ref = out_refs
    if has_state:
        q_ref, k_ref, v_ref, gates_ref, gain_ref, c0_ref, n0_ref, m0_ref, _ = in_refs
    else:
        q_ref, k_ref, v_ref, gates_ref, gain_ref, _ = in_refs
    if has_state and nc == 1:
        c_prev, n_prev_ref, m_prev_ref = c0_ref, n0_ref, m0_ref
    else:
        c_prev, n_prev_ref, m_prev_ref = c_ref, n_ref, m_ref

        @pl.when(pl.program_id(0) % nc == 0)
        def _():
            if has_state:
                c_ref[...] = c0_ref[...]
                n_ref[...] = n0_ref[...]
                m_ref[...] = m0_ref[...]
            else:
                c_ref[...] = jnp.zeros_like(c_ref)
                n_ref[...] = jnp.zeros_like(n_ref)
                m_ref[...] = jnp.zeros_like(m_ref)

    log_scale = -0.5 * math.log(HEAD_DIM)
    row = lax.broadcasted_iota(jnp.int32, (L, L), 0)
    col = lax.broadcasted_iota(jnp.int32, (L, L), 1)
    causal = col <= row
    heads = range(N_HEADS)
    seqs = range(nb)
    hcols = [slice(h * HEAD_DIM, (h + 1) * HEAD_DIM) for h in heads]
    ones_rows = jnp.ones((SUBLANE, L), BF16)
    n_prev = [n_prev_ref[s] for s in seqs]


    qc = [[_dot(q_ref[s, :, hcols[h]], c_prev[s, h].astype(BF16)) for h in heads] for s in seqs]
    yield

    gvs, a_rows = [], []
    for s in seqs:
        gv = _gate_vectors(gates_ref[s], m_prev_ref[s, 0:1, :], L)
        m_ref[s] = jnp.broadcast_to(gv['ml'], m_ref.shape[1:])
        a_shift = gv['a'] + log_scale
        if L % LANE == 0:
            a_t = a_shift.T
            a_rows.append([a_t[h:h + 1, :] for h in heads])
        else:
            a_rows.append([jnp.sum(jnp.where(row == col, a_shift[:, h:h + 1], 0.0), axis=0,
                                   keepdims=True) for h in heads])
        gvs.append(gv)
    yield

    for s in seqs:
        hcol = lambda name, h: gvs[s][name][:, h:h + 1]
        for h in heads:
            qh, kh, vh = q_ref[s, :, hcols[h]], k_ref[s, :, hcols[h]], v_ref[s, :, hcols[h]]

            g_inter = hcol('g_inter', h)
            kg = kh * hcol('gs', h).astype(BF16)
            upd = lax.dot_general(kg, vh, (((0,), (0,)), ((), ())), preferred_element_type=F32)
            c_ref[s, h] = g_inter * c_prev[s, h] + upd
            n_ref[s, h:h + 1, :] = (g_inter * n_prev[s][h:h + 1, :]
                                    + _dot(ones_rows, kg)[0:1, :])
            yield

            w_inter = hcol('w_inter', h)
            w_intra = jnp.exp(jnp.where(causal, hcol('dcol', h) + a_rows[s][h], NEG))
            sc = lax.dot_general(qh, kh, (((1,), (1,)), ((), ())),
                                 preferred_element_type=F32) * w_intra
            num = _dot(sc.astype(BF16), vh) + qc[s][h] * w_inter
            qn = jnp.sum((qh * n_prev[s][h:h + 1, :].astype(BF16)).astype(F32), axis=1,
                         keepdims=True)
            nq = jnp.sum(sc, axis=1, keepdims=True) + w_inter * qn
            den = jnp.maximum(jnp.abs(nq), hcol('exp_neg_m', h))
            mu = jnp.mean(num, axis=-1, keepdims=True)
            hc = num - mu
            var = jnp.mean(hc * hc, axis=-1, keepdims=True)
            h_ref[s, :, hcols[h]] = (hc * lax.rsqrt(var + EPS * den * den)
                                     * gain_ref[:, hcols[h]]).astype(BF16)
            yield


def _ml_core(groups, j):
    ios = [_core_group_io(g, j) for g in groups]
    steps = {io['steps'] for io in ios}
    assert len(steps) == 1, "token groups must have the same number of grid steps"
    in_specs, args, out_specs, out_shape, aliases = [], [], [], [], {}
    for gi, io in enumerate(ios):
        if io['alias_in'] is not None:
            aliases[len(args) + io['alias_in']] = N_CORE_OUT * gi + 1
        in_specs += io['in_specs']
        args += io['args']
        out_specs += io['out_specs']
        out_shape += io['out_shape']
    outs = pl.pallas_call(
        functools.partial(_ml_core_body, cfgs=tuple(io['cfg'] for io in ios)),
        out_shape=tuple(out_shape),
        grid=(steps.pop(),),
        in_specs=in_specs,
        out_specs=tuple(out_specs),
        input_output_aliases=aliases,
        compiler_params=_cparams(1, VMEM_LIMIT_CORE),
        name="ml_core",
    )(*args)
    return [_core_group_result(g, outs[N_CORE_OUT * gi:N_CORE_OUT * (gi + 1)])
            for gi, g in enumerate(groups)]


N_CORE_OUT = 4


def _core_group_io(g, j):
    n_ml = 2
    B, nc, L, nb = g['B'], g['nc'], g['L'], g['nb']
    inner = g['q'].shape[-1]
    has_state = g['state'] is not None
    seq3 = lambda a: a.reshape(B, nc * L, a.shape[-1])
    tok = lambda w: pl.BlockSpec((nb, L, w), lambda t: (t // nc, t % nc, 0))
    per_seq = lambda shape: pl.BlockSpec((nb,) + shape, lambda t: (t // nc,) + (0,) * len(shape))
    per_seq_layer = lambda shape: pl.BlockSpec(
        (None, nb) + shape, lambda t: (j, t // nc) + (0,) * len(shape))
    c_block = per_seq_layer((N_HEADS, HEAD_DIM, HEAD_DIM))
    in_specs = [tok(inner), tok(inner), tok(inner), tok(2 * LANE), _layer_spec((1, inner), j)]
    args = [seq3(g['q']), seq3(g['k']), seq3(g['v']), seq3(g['gates']),
            g['gain'].reshape(g['gain'].shape[0], 1, inner)]
    if has_state:
        in_specs += [c_block, per_seq_layer((N_HEADS, HEAD_DIM)), per_seq((SUBLANE, LANE))]
        args += list(g['state'])
    in_specs.append(pl.BlockSpec(memory_space=pl.ANY))
    alias_in = None
    if g['c_all'] is not None:
        args.append(g['c_all'])
        alias_in = len(args) - 1
    else:
        args.append(jnp.zeros((SUBLANE, LANE), F32))
    out_shape = [jax.ShapeDtypeStruct((B, nc * L, inner), BF16),
                 jax.ShapeDtypeStruct((n_ml, B, N_HEADS, HEAD_DIM, HEAD_DIM), F32),
                 jax.ShapeDtypeStruct((B, N_HEADS, HEAD_DIM), F32),
                 jax.ShapeDtypeStruct((B, SUBLANE, LANE), F32)]
    out_specs = [tok(inner), c_block, per_seq((N_HEADS, HEAD_DIM)), per_seq((SUBLANE, LANE))]
    return dict(in_specs=in_specs, args=args, out_specs=out_specs, out_shape=out_shape,
                cfg=(L, nb, nc, has_state), alias_in=alias_in, steps=B // nb * nc)


def _core_group_result(g, outs):
    hn, c_all, n, m = outs
    return hn.reshape(g['q'].shape), c_all, n, m


def _gelu(x):
    return 0.5 * x * (1.0 + lax.erf(x * (2.0 ** -0.5)))


def _cm_body(*refs, width, sub, emit_vn):
    if emit_vn:
        x_ref, g_ref, win_ref, bin_ref, lg_ref, wmix_ref, bmix_ref, wout_ref, o_ref, vn_ref = refs
    else:
        x_ref, g_ref, win_ref, bin_ref, lg_ref, wmix_ref, bmix_ref, wout_ref, o_ref = refs
    gd = width // CM_GROUPS
    n_sub = x_ref.shape[0] // sub

    def in_proj(s):
        x = x_ref[s * sub:(s + 1) * sub, :]
        xn = _rmsnorm(x, g_ref[...]).astype(BF16)
        v = _gelu(_dot(xn, win_ref[:, width:]) + bin_ref[:, width:])
        u = _gelu(_dot(xn, win_ref[:, :width]) + bin_ref[:, :width])
        return x, u, v

    def gate_and_out(s, x, u, v):
        rows = slice(s * sub, (s + 1) * sub)
        mu = jnp.mean(v, axis=-1, keepdims=True)
        vc = v - mu
        var = jnp.mean(vc * vc, axis=-1, keepdims=True)
        vn = vc * lax.rsqrt(var + EPS) * lg_ref[...]
        if emit_vn:
            vn_ref[rows, :] = vn
        vnb = vn.astype(BF16)
        bmix = bmix_ref[...]
        parts = []
        for g in range(CM_GROUPS):
            cols = slice(g * gd, (g + 1) * gd)
            mix = _dot(wmix_ref[g], vnb[:, cols]) + bmix[:, g:g + 1]
            parts.append((u[:, cols] * mix).astype(BF16))
        o_ref[rows, :] = x + _dot(jnp.concatenate(parts, axis=-1), wout_ref[...])

    nxt = in_proj(0)
    for s in range(n_sub):
        cur = nxt
        if s + 1 < n_sub:
            nxt = in_proj(s + 1)
        gate_and_out(s, *cur)


def _chunk_mlp(x, norm_g, layer, w_in, b_in, ln_g, wmix, bmix, w_out, j, tm, emit_vn):
    T, D = x.shape
    width = ln_g.shape[-1]
    body = functools.partial(_cm_body, width=width, sub=wmix.shape[1], emit_vn=emit_vn)
    tok = lambda w: pl.BlockSpec((tm, w), lambda i: (i, 0))
    out_shape = [jax.ShapeDtypeStruct((T, D), F32)]
    out_specs = [tok(D)]
    if emit_vn:
        out_shape.append(jax.ShapeDtypeStruct((T, width), F32))
        out_specs.append(tok(width))
    return pl.pallas_call(
        body,
        out_shape=tuple(out_shape),
        grid=(T // tm,),
        in_specs=[tok(D), _layer_spec((1, D), layer),
                  _layer_spec((D, 2 * width), j), _layer_spec((1, 2 * width), j),
                  _layer_spec((1, width), j),
                  _const_spec(wmix.shape), _const_spec(bmix.shape),
                  _layer_spec((width, D), j)],
        out_specs=tuple(out_specs),
        compiler_params=_cparams(1),
        name="chunk_mlp",
    )(x, norm_g.reshape(norm_g.shape[0], 1, D), w_in, b_in.reshape(b_in.shape[0], 1, 2 * width),
      ln_g.reshape(ln_g.shape[0], 1, width), wmix, bmix, w_out)


def _blockdiag_dense(w):
    rows = w.reshape(-1, MXU_DIM, QKV_BLOCK)
    r = lax.broadcasted_iota(jnp.int32, (MXU_DIM, MXU_DIM), 0)
    c = lax.broadcasted_iota(jnp.int32, (MXU_DIM, MXU_DIM), 1)
    spread = (lax.broadcasted_iota(jnp.int32, (QKV_BLOCK, MXU_DIM), 1) % QKV_BLOCK
              == lax.broadcasted_iota(jnp.int32, (QKV_BLOCK, MXU_DIM), 0)).astype(w.dtype)
    dense = jnp.einsum('grd,dc->grc', rows, spread, precision=lax.Precision.HIGHEST)
    return jnp.where(r // QKV_BLOCK == c // QKV_BLOCK, dense, 0)


def _gate_weights(w_ig, b_ig, w_fg, b_fg, dense_qkv):
    n_tiles = dense_qkv[0].shape[0]
    w = jnp.concatenate([w_ig, w_fg], axis=1).reshape(3, n_tiles, MXU_DIM, 2 * N_HEADS)
    folded = jnp.einsum('mgrc,mgch->mgrh', jnp.stack(dense_qkv), w,
                        precision=lax.Precision.HIGHEST)
    folded = folded.reshape(3, n_tiles * MXU_DIM, 2 * N_HEADS)
    lane = lax.broadcasted_iota(jnp.int32, (2 * N_HEADS, 2 * LANE), 1)
    src = lax.broadcasted_iota(jnp.int32, (2 * N_HEADS, 2 * LANE), 0)
    place = (lane == (src % N_HEADS) + LANE * (src // N_HEADS)).astype(w_ig.dtype)
    w_all = jnp.dot(jnp.concatenate([folded[0] + folded[1], folded[2]], axis=0), place,
                    precision=lax.Precision.HIGHEST)
    bpad = jnp.zeros((LANE - N_HEADS,), b_ig.dtype)
    b = jnp.concatenate([b_ig, bpad, b_fg, bpad])
    return w_all.astype(BF16), b.reshape(1, 2 * LANE)


def _trunks(xs, grps, p):
    depth = p['norm_ff1'].shape[0]
    sts = [dict(x=x, c_all=None, n=[], m=[], conv=[], v=[]) for x in xs]

    ffn_w = {'cur': tuple(w[0].astype(BF16) for w in p['ffn1_w'])}

    def ffn_round(norm_key, i, nxt, use_mixer=False, final_g=None):
        new_w = None
        for gi, (grp, st) in enumerate(zip(grps, sts)):
            mixer = st['mixer'] if use_mixer else None
            tm = grp['tm'] if mixer is None else grp['tm_mixer']
            cast = nxt if gi == 0 else None
            out = _ffn(st['x'], p[norm_key], ffn_w['cur'], i, tm, final_g=final_g, mixer=mixer,
                       cast=cast)
            if cast is not None:
                out, new_w = out
            st['x'] = out
        ffn_w['cur'] = new_w

    for i in range(depth):
        j = i // 2
        ffn_round('norm_ff1', i, (p['ffn2_w'], i))
        if i % 2 == 0:
            core_in = []
            for grp, st in zip(grps, sts):
                q, k, v, xc, z, gates, tail = _ml_pre(
                    st['x'], p['norm_mix'], i, p['ml_w_up'], p['ml_w_conv'], p['ml_b_conv'],
                    p['ml_skip'], p['bq'][j], p['bk'][j], p['bv'][j], p['w_gates'][j],
                    p['b_gates'][j], grp['hist'][j], j, grp['pre_tm'], grp['stride'],
                    grp['tiles_per_seq'])
                st['conv'].append(grp['conv_out'](tail))
                st['xc'], st['z'] = xc, z
                q, k, v, gates = grp['to_core'](q, k, v, gates)
                core_in.append(dict(q=q, k=k, v=v, gates=gates, gain=p['ml_hn_g'], B=grp['B'],
                                    nc=grp['nc'], L=grp['L'], nb=grp['nb'], state=grp['state'](j),
                                    c_all=st['c_all']))
            for grp, st, (hn, c_all, n, m) in zip(grps, sts, _ml_core(core_in, j)):
                st['c_all'] = c_all
                st['n'].append(n)
                st['m'].append(m[:, 0, :N_HEADS])
                st['mixer'] = (grp['from_core'](hn), st['xc'], st['z'], p['ml_w_down'], j)
        else:
            for grp, st in zip(grps, sts):
                outs = _chunk_mlp(st['x'], p['norm_mix'], i, p['cm_w_in'], p['cm_b_in'], p['cm_ln_g'],
                                  grp['wmix'][j], grp['bmix'][j], p['cm_w_out'], j, grp['cm_tm'],
                                  emit_vn=grp['emit_vn'])
                st['x'] = outs[0]
                st['v'].append(outs[1] if grp['emit_vn'] else None)
                st['mixer'] = None
        last = i == depth - 1
        ffn_round('norm_ff2', i, None if last else (p['ffn1_w'], i + 1), use_mixer=True,
                  final_g=p['norm_final'] if last else None)
    return [(st['x'], st['c_all'], jnp.stack(st['n']), jnp.stack(st['m']), jnp.stack(st['conv']),
             st['v']) for st in sts]


def kernel(x_prompt, x_sample, state_C, state_n, state_m, state_conv, norm_ff1, norm_mix, norm_ff2, norm_final, ffn1_w_gate, ffn1_w_up, ffn1_w_down, ffn2_w_gate, ffn2_w_up, ffn2_w_down, ml_w_up, ml_w_conv, ml_b_conv, ml_w_q, ml_w_k, ml_w_v, ml_w_ig, ml_b_ig, ml_w_fg, ml_b_fg, ml_hn_g, ml_skip, ml_w_down, cm_w_in, cm_b_in, cm_ln_g, cm_w_s, cm_b_s, cm_w_out):
    bp, sp, d_model = x_prompt.shape
    bs, ss, _ = x_sample.shape
    n_ml, inner = ml_w_conv.shape[0], ml_w_conv.shape[-1]
    n_cm, width = cm_ln_g.shape
    hist_rows = CONV_W - 1
    sb = min(64, bs)

    dense_qkv = [tuple(_blockdiag_dense(w[j]) for w in (ml_w_q, ml_w_k, ml_w_v)) for j in range(n_ml)]
    gate_wb = [_gate_weights(ml_w_ig[j], ml_b_ig[j], ml_w_fg[j], ml_b_fg[j], dense_qkv[j])
               for j in range(n_ml)]
    p = {
        'norm_ff1': norm_ff1, 'norm_mix': norm_mix, 'norm_ff2': norm_ff2, 'norm_final': norm_final,
        'ffn1_w': (ffn1_w_gate, ffn1_w_up, ffn1_w_down),
        'ffn2_w': (ffn2_w_gate, ffn2_w_up, ffn2_w_down),
        'ml_w_up': ml_w_up.astype(BF16), 'ml_w_conv': ml_w_conv, 'ml_b_conv': ml_b_conv,
        'bq': [d[0].astype(BF16) for d in dense_qkv], 'bk': [d[1].astype(BF16) for d in dense_qkv],
        'bv': [d[2].astype(BF16) for d in dense_qkv],
        'w_gates': [wb[0] for wb in gate_wb], 'b_gates': [wb[1] for wb in gate_wb],
        'ml_hn_g': ml_hn_g, 'ml_skip': ml_skip, 'ml_w_down': ml_w_down.astype(BF16),
        'cm_w_in': cm_w_in.astype(BF16), 'cm_b_in': cm_b_in, 'cm_ln_g': cm_ln_g,
        'cm_w_out': cm_w_out.astype(BF16),
    }

    def mix_weights(L, kron_left):
        causal = jnp.tril(jnp.ones((L, L), dtype=bool))
        ws = jnp.where(causal, cm_w_s[:, :, :L, :L], 0)
        bs_ = cm_b_s[:, :, :L]
        if kron_left:
            reps = kron_left
            eye = jnp.eye(reps, dtype=ws.dtype)
            wm = ws[:, :, None, :, None, :] * eye[None, None, :, None, :, None]
            wm = wm.reshape(n_cm, CM_GROUPS, reps * L, reps * L)
            bm = jnp.tile(bs_, (1, 1, reps))
        else:
            eye = jnp.eye(sb, dtype=ws.dtype)
            wm = ws[:, :, :, None, :, None] * eye[None, None, None, :, None, :]
            wm = wm.reshape(n_cm, CM_GROUPS, L * sb, L * sb)
            bm = jnp.repeat(bs_, sb, axis=2)
        bm = jnp.swapaxes(bm, 1, 2)
        bm = jnp.concatenate([bm, jnp.zeros(bm.shape[:2] + (LANE - CM_GROUPS,), bm.dtype)], axis=2)
        return wm.astype(BF16), bm

    tm_p = 1024
    pre_tm_p = 512
    cm_tm_p = 1024
    chunk_p = min(256, sp)
    cm_chunk_p = min(128, sp)
    wmix_p, bmix_p = mix_weights(cm_chunk_p, MXU_DIM // cm_chunk_p)
    grp_p = {
        'tm': tm_p, 'tm_mixer': 512, 'pre_tm': pre_tm_p, 'cm_tm': cm_tm_p, 'emit_vn': False,
        'stride': 1,
        'tiles_per_seq': sp // pre_tm_p,
        'hist': [jnp.zeros((bp, SUBLANE, inner), F32)] * n_ml,
        'conv_out': lambda tail: tail.reshape(bp, sp // pre_tm_p, SUBLANE, inner)[:, -1, SUBLANE - hist_rows:, :],
        'to_core': lambda q, k, v, g: (q, k, v, g),
        'from_core': lambda hn: hn,
        'state': lambda j: None,
        'B': bp, 'nc': sp // chunk_p, 'L': chunk_p, 'nb': 1,
        'wmix': wmix_p, 'bmix': bmix_p,
    }

    ts = bs * ss
    n_sg = bs // sb
    l_pad = SUBLANE

    def tile_major(a):
        steps, feat = a.shape[1], a.shape[2]
        return a.reshape(n_sg, sb, steps, feat).transpose(0, 2, 1, 3).reshape(n_sg * steps * sb, feat)

    def batch_major(a, steps):
        feat = a.shape[-1]
        return a.reshape(n_sg, steps, sb, feat).transpose(0, 2, 1, 3).reshape(bs, steps, feat)

    def to_core(q, k, v, g):
        def seq_major(a, fill):
            a = batch_major(a, ss)
            pad = jnp.broadcast_to(fill.astype(a.dtype), (bs, l_pad - ss, a.shape[-1]))
            return jnp.concatenate([a, pad], axis=1).reshape(bs * l_pad, a.shape[-1])
        zero = jnp.zeros((inner,), F32)
        lane_pad = jnp.zeros((LANE - N_HEADS,), F32)
        gfill = jnp.concatenate([jnp.full((N_HEADS,), NEG, F32), lane_pad,
                                 jnp.full((N_HEADS,), -NEG, F32), lane_pad])
        return seq_major(q, zero), seq_major(k, zero), seq_major(v, zero), seq_major(g, gfill)

    def from_core(hn):
        return tile_major(hn.reshape(bs, l_pad, inner)[:, :ss])

    def sample_state(j):
        m0 = jnp.pad(state_m[j], ((0, 0), (0, LANE - N_HEADS)))
        return state_C, state_n, jnp.broadcast_to(m0[:, None, :], (bs, SUBLANE, LANE))

    wmix_s, bmix_s = mix_weights(ss, 0)
    tile_s = ss * sb
    hist_s = [tile_major(state_conv[j]).reshape(n_sg, hist_rows * sb, inner) for j in range(n_ml)]
    grp_s = {
        'tm': tile_s, 'tm_mixer': tile_s, 'pre_tm': tile_s, 'cm_tm': tile_s, 'emit_vn': True,
        'stride': sb,
        'tiles_per_seq': 1,
        'hist': hist_s,
        'conv_out': lambda tail: batch_major(tail.reshape(n_sg * hist_rows * sb, inner), hist_rows),
        'to_core': to_core, 'from_core': from_core, 'state': sample_state,
        'B': bs, 'nc': 1, 'L': l_pad, 'nb': 2,
        'wmix': wmix_s, 'bmix': bmix_s,
    }
    ((yp, c_p, n_p, m_p, conv_p, _), (ys, c_s, n_s, m_s, conv_s, v_rows)) = _trunks(
        [x_prompt.reshape(bp * sp, d_model), tile_major(x_sample)], [grp_p, grp_s], p)

    y_prompt = yp.reshape(bp, sp, d_model)
    y_sample = batch_major(ys, ss)
    v_sample = jnp.stack([batch_major(vr, ss) for vr in v_rows])
    return (y_prompt, y_sample, c_p, n_p, m_p, conv_p, c_s, n_s, m_s, conv_s, v_sample)
```
